```python
import jax
import jax.numpy as jnp
from jax import lax
import numpy as np

D_MODEL = 2048
BATCH = 4
SEQ = 2048
DEPTH = 2

GRID_W = 64
CTX_LEN = 256
NORM_EPS = 1e-6
N_BRANCH = 4
BRANCH_W = D_MODEL // N_BRANCH

FNO_GROUPS = 4
FNO_GROUP_W = BRANCH_W // FNO_GROUPS

RWKV_HEAD = 64
RWKV_HEADS = BRANCH_W // RWKV_HEAD
N_DIR = 2
DECAY_LORA = 64
AICL_LORA = 64
GATE_LORA = 128
DIR_LORA_W = DECAY_LORA + AICL_LORA
SHIFT_W = 3 * BRANCH_W + DIR_LORA_W
GN_EPS = 64e-5

ATT_HEAD = 64
ATT_HEADS = BRANCH_W // ATT_HEAD
ATT_KV_HEADS = 2
ATT_REP = ATT_HEADS // ATT_KV_HEADS
ATT_KV_W = ATT_KV_HEADS * ATT_HEAD
WINDOW = 128
BLOCK = 128
ROPE_BASE = 10000.0
NEG_INF = -1e30

CONV_K = 31
LN_EPS = 1e-5

MLP_HIDDEN = 4 * D_MODEL

O_RKV = 0
O_LORA = O_RKV + 3 * BRANCH_W
O_KV = O_LORA + N_DIR * DIR_LORA_W
CTX_STATE_COLS = O_KV + 2 * ATT_KV_W
O_G = CTX_STATE_COLS
O_Q = O_G + GATE_LORA
O_FNO = O_Q + BRANCH_W
O_CONV = O_FNO + BRANCH_W
O_GATE = O_CONV + 2 * BRANCH_W
IN_W = O_GATE + N_BRANCH * D_MODEL

F32 = jnp.float32

kernel_name = 'hybrid_fourier_rwkv7_swa_conformer_dit'


def to_heads(t, n):
    return t.reshape(t.shape[:-1] + (n, t.shape[-1] // n))


def rmsnorm(t, g):
    t32 = t.astype(F32)
    y = t32 * lax.rsqrt(jnp.mean(t32 * t32, axis=-1, keepdims=True) + NORM_EPS)
    return (y * g.astype(F32)).astype(t.dtype)


def token_shift(t, forward):
    if forward:
        return jnp.pad(t[:, :-1], ((0, 0), (1, 0), (0, 0)))
    return jnp.pad(t[:, 1:], ((0, 0), (0, 1), (0, 0)))


def fourier_mix(u):
    B, L, _ = u.shape
    g = u.astype(F32).reshape(B, L, FNO_GROUPS, FNO_GROUP_W)
    f = jnp.fft.fft2(g, axes=(1, 3), norm='ortho').real
    return f.reshape(B, L, BRANCH_W).astype(u.dtype)


def rwkv_scan_inputs(p, mu, w0, w_up, a0, a_up, k_k, k_a):
    rkv = p[..., :3 * BRANCH_W]
    per_dir = []
    for d in range(N_DIR):
        lo = O_LORA + d * DIR_LORA_W
        f = jnp.concatenate([rkv, p[..., lo:lo + DIR_LORA_W]], axis=-1)
        f = f + mu[d] * (token_shift(f, forward=(d == 0)) - f)
        r, k, v, wl, al = jnp.split(
            f, [BRANCH_W, 2 * BRANCH_W, 3 * BRANCH_W, 3 * BRANCH_W + DECAY_LORA], axis=-1)
        w_raw = (w0[d] + jnp.tanh(wl) @ w_up[d]).astype(F32)
        decay = jnp.exp(-jnp.exp(-jax.nn.softplus(-w_raw) - 0.5))
        a = jax.nn.sigmoid((a0[d] + al @ a_up[d]).astype(F32))
        kk = to_heads((k * k_k).astype(F32), RWKV_HEADS)
        kk = kk / jnp.maximum(jnp.sqrt(jnp.sum(kk * kk, axis=-1, keepdims=True)), 1e-12)
        k_rep = k.astype(F32) * (1.0 + (a - 1.0) * k_a.astype(F32))
        per_dir.append([to_heads(r.astype(F32), RWKV_HEADS), to_heads(decay, RWKV_HEADS),
                        to_heads(k_rep, RWKV_HEADS), to_heads(v.astype(F32), RWKV_HEADS),
                        kk, to_heads(a, RWKV_HEADS)])
    return tuple(jnp.stack([per_dir[0][i], per_dir[1][i]]) for i in range(6))


def wkv7_scan(state0, ins, emit):
    def time_major(t):
        t = jnp.stack([t[0], jnp.flip(t[1], axis=1)])
        return jnp.moveaxis(t, 2, 0)

    xs = tuple(time_major(t) for t in ins)

    def step(S, inp):
        r, w, k, v, kk, a = inp
        sa = jnp.einsum('dbhvk,dbhk->dbhv', S, -kk)
        S = (S * w[..., None, :] + sa[..., :, None] * (kk * a)[..., None, :]
             + v[..., :, None] * k[..., None, :])
        y = jnp.einsum('dbhvk,dbhk->dbhv', S, r) if emit else None
        return S, y

    S, ys = lax.scan(step, state0, xs)
    if emit:
        ys = jnp.moveaxis(ys, 0, 2)
        ys = ys[0] + jnp.flip(ys[1], axis=1)
    return S, ys


def rwkv_readout(y, ins, g_in, g_up, r_k, lnx_g, lnx_b, dtype):
    r, _, k, v, _, _ = ins
    B, L = y.shape[:2]
    mean = jnp.mean(y, axis=-1, keepdims=True)
    var = jnp.mean(jnp.square(y - mean), axis=-1, keepdims=True)
    yn = ((y - mean) * lax.rsqrt(var + GN_EPS)).reshape(B, L, BRANCH_W)
    yn = yn * lnx_g.astype(F32) + lnx_b.astype(F32)
    bonus = jnp.sum(jnp.sum(r * k * r_k[:, None, None].astype(F32), axis=-1, keepdims=True) * v, axis=0)
    g = (jax.nn.sigmoid(g_in) @ g_up).astype(F32)
    return ((yn + bonus.reshape(B, L, BRANCH_W)) * g).astype(dtype)


def axial_rope(t, row_ids, col_ids):
    half = t.shape[-1] // 2

    def rotate(u, pos):
        nf = u.shape[-1] // 2
        inv = ROPE_BASE ** (-jnp.arange(nf, dtype=F32) / nf)
        ang = pos.astype(F32)[:, None] * inv[None, :]
        cos = jnp.cos(ang)[None, :, None, :]
        sin = jnp.sin(ang)[None, :, None, :]
        u1 = u[..., :nf].astype(F32)
        u2 = u[..., nf:].astype(F32)
        return jnp.concatenate([u1 * cos - u2 * sin, u2 * cos + u1 * sin], axis=-1)

    out = jnp.concatenate([rotate(t[..., :half], row_ids), rotate(t[..., half:], col_ids)], axis=-1)
    return out.astype(t.dtype)


def window_attention(q, k, v, kc, vc, sink):
    B, S, H, dh = q.shape
    nb = S // BLOCK
    scale = dh ** -0.5
    qb = q.reshape(B, nb, BLOCK, ATT_KV_HEADS, ATT_REP, dh)

    def band(t):
        tp = jnp.pad(t, ((0, 0), (BLOCK, BLOCK), (0, 0), (0, 0)))
        tp = tp.reshape(B, nb + 2, BLOCK, ATT_KV_HEADS, dh)
        return jnp.concatenate([tp[:, :-2], tp[:, 1:-1], tp[:, 2:]], axis=2)

    kw, vw = band(k), band(v)
    s_loc = jnp.einsum('bnqgrd,bnkgd->bngrqk', qb, kw).astype(F32) * scale
    blk = jnp.arange(nb)[:, None, None] * BLOCK
    qpos = blk + jnp.arange(BLOCK)[None, :, None]
    kpos = blk - BLOCK + jnp.arange(3 * BLOCK)[None, None, :]
    valid = (jnp.abs(kpos - qpos) <= WINDOW) & (kpos >= 0) & (kpos < S)
    s_loc = jnp.where(valid[None, :, None, None], s_loc, NEG_INF)
    s_ctx = jnp.einsum('bnqgrd,bcgd->bngrqc', qb, kc).astype(F32) * scale
    s_sink = jnp.broadcast_to(sink.astype(F32).reshape(1, 1, ATT_KV_HEADS, ATT_REP, 1, 1),
                              s_loc.shape[:-1] + (1,))
    p = jax.nn.softmax(jnp.concatenate([s_loc, s_ctx, s_sink], axis=-1), axis=-1)
    nk = 3 * BLOCK
    nc = kc.shape[1]
    o = (jnp.einsum('bngrqk,bnkgd->bnqgrd', p[..., :nk].astype(v.dtype), vw)
         + jnp.einsum('bngrqc,bcgd->bnqgrd', p[..., nk:nk + nc].astype(vc.dtype), vc))
    return o.reshape(B, S, H * dh)


def context_attention(qc, kc, vc, sink):
    B, C, H, dh = qc.shape
    qg = qc.reshape(B, C, ATT_KV_HEADS, ATT_REP, dh)
    s = jnp.einsum('bqgrd,bkgd->bgrqk', qg, kc).astype(F32) * (dh ** -0.5)
    s_sink = jnp.broadcast_to(sink.astype(F32).reshape(1, ATT_KV_HEADS, ATT_REP, 1, 1), s.shape[:-1] + (1,))
    p = jax.nn.softmax(jnp.concatenate([s, s_sink], axis=-1), axis=-1)
    o = jnp.einsum('bgrqk,bkgd->bqgrd', p[..., :C].astype(vc.dtype), vc)
    return o.reshape(B, C, H * dh)


def conformer_conv(u, dw, dw_b, ln_g, ln_b):
    a, b = jnp.split(u, 2, axis=-1)
    h = a * jax.nn.sigmoid(b)
    h = lax.conv_general_dilated(h, dw[:, None, :].astype(h.dtype), (1,), 'SAME',
                                 dimension_numbers=('NWC', 'WIO', 'NWC'),
                                 feature_group_count=BRANCH_W) + dw_b
    h32 = h.astype(F32)
    mean = jnp.mean(h32, axis=-1, keepdims=True)
    var = jnp.mean(jnp.square(h32 - mean), axis=-1, keepdims=True)
    h32 = (h32 - mean) * lax.rsqrt(var + LN_EPS) * ln_g.astype(F32) + ln_b.astype(F32)
    return (h32 * jax.nn.sigmoid(h32)).astype(u.dtype)


def kv_heads(p):
    k = to_heads(p[..., O_KV:O_KV + ATT_KV_W], ATT_KV_HEADS)
    v = to_heads(p[..., O_KV + ATT_KV_W:CTX_STATE_COLS], ATT_KV_HEADS)
    return k, v


def branch_stack(p, ins, y, att, lp):
    fno = fourier_mix(p[..., O_FNO:O_FNO + BRANCH_W])
    rw = rwkv_readout(y, ins, p[..., O_G:O_G + GATE_LORA], lp['rwkv_g_up'], lp['rwkv_r_k'],
                      lp['rwkv_lnx_g'], lp['rwkv_lnx_b'], p.dtype)
    cv = conformer_conv(p[..., O_CONV:O_CONV + 2 * BRANCH_W], lp['conv_dw'], lp['conv_dw_b'],
                        lp['conv_ln_g'], lp['conv_ln_b'])
    return jnp.stack([fno, rw, att.astype(p.dtype), cv], axis=2)


def merge_branches(feats, gate_logits, w_branch, w_out):
    B, L = feats.shape[:2]
    proj = jnp.einsum('blif,ifd->blid', feats, w_branch)
    gate = jax.nn.sigmoid(gate_logits.reshape(B, L, N_BRANCH, D_MODEL))
    return jnp.einsum('blid,de->ble', proj * gate, w_out)


def hybrid_mixer(hx, hc, row_ids, col_ids, lp, with_ctx):
    B = hx.shape[0]
    px = hx @ lp['w_in']
    pc = hc @ (lp['w_in'] if with_ctx else lp['w_in'][:, :CTX_STATE_COLS])
    scan_args = (lp['rwkv_mu'], lp['rwkv_w0'], lp['rwkv_w_up'], lp['rwkv_a0'], lp['rwkv_a_up'],
                 lp['rwkv_k_k'], lp['rwkv_k_a'])
    ins_c = rwkv_scan_inputs(pc[..., :O_KV], *scan_args)
    ins_x = rwkv_scan_inputs(px[..., :O_KV], *scan_args)
    state0 = jnp.zeros((N_DIR, B, RWKV_HEADS, RWKV_HEAD, RWKV_HEAD), F32)
    state_c, y_c = wkv7_scan(state0, ins_c, emit=with_ctx)
    _, y_x = wkv7_scan(state_c, ins_x, emit=True)
    kc, vc = kv_heads(pc)
    kx, vx = kv_heads(px)
    qx = axial_rope(to_heads(px[..., O_Q:O_Q + BRANCH_W], ATT_HEADS), row_ids, col_ids)
    kx = axial_rope(kx, row_ids, col_ids)
    att_x = window_attention(qx, kx, vx, kc, vc, lp['att_sink'])
    out_x = merge_branches(branch_stack(px, ins_x, y_x, att_x, lp), px[..., O_GATE:],
                           lp['w_branch'], lp['w_out'])
    if not with_ctx:
        return out_x, None
    qc = to_heads(pc[..., O_Q:O_Q + BRANCH_W], ATT_HEADS)
    att_c = context_attention(qc, kc, vc, lp['att_sink'])
    out_c = merge_branches(branch_stack(pc, ins_c, y_c, att_c, lp), pc[..., O_GATE:],
                           lp['w_branch'], lp['w_out'])
    return out_x, out_c


def sq_relu_mlp(h, w1, w2):
    return jnp.square(jax.nn.relu(h @ w1)) @ w2


def setup_inputs(seed: int = 0) -> dict:
    key = jax.random.key(seed)
    ks = jax.random.split(key, 32)
    D = D_MODEL

    def nrm(k, shape, s):
        return jax.random.normal(k, shape, F32) * s

    return {
        'x': nrm(ks[0], (BATCH, SEQ, D), 1.0),
        'c': nrm(ks[1], (BATCH, D), 1.0),
        'ctx': nrm(ks[2], (BATCH, CTX_LEN, D), 1.0),
        'c_ctx': nrm(ks[3], (D,), 1.0),
        'ada_w': nrm(ks[4], (DEPTH, D, 6 * D), 0.5 * D ** -0.5),
        'ada_b': nrm(ks[5], (DEPTH, 6 * D), 0.02),
        'norm1_g': 1.0 + nrm(ks[6], (DEPTH, D), 0.02),
        'norm2_g': 1.0 + nrm(ks[7], (DEPTH, D), 0.02),
        'w_in': nrm(ks[8], (DEPTH, D, IN_W), D ** -0.5),
        'rwkv_mu': jax.random.uniform(ks[9], (DEPTH, N_DIR, SHIFT_W), F32),
        'rwkv_w0': nrm(ks[10], (DEPTH, N_DIR, BRANCH_W), 0.5),
        'rwkv_w_up': nrm(ks[11], (DEPTH, N_DIR, DECAY_LORA, BRANCH_W), DECAY_LORA ** -0.5),
        'rwkv_a0': nrm(ks[12], (DEPTH, N_DIR, BRANCH_W), 0.5),
        'rwkv_a_up': nrm(ks[13], (DEPTH, N_DIR, AICL_LORA, BRANCH_W), AICL_LORA ** -0.5),
        'rwkv_k_k': 0.85 + nrm(ks[14], (DEPTH, BRANCH_W), 0.05),
        'rwkv_k_a': 1.0 + nrm(ks[15], (DEPTH, BRANCH_W), 0.05),
        'rwkv_r_k': nrm(ks[16], (DEPTH, N_DIR, RWKV_HEADS, RWKV_HEAD), 0.1),
        'rwkv_g_up': nrm(ks[17], (DEPTH, GATE_LORA, BRANCH_W), GATE_LORA ** -0.5),
        'rwkv_lnx_g': 1.0 + nrm(ks[18], (DEPTH, BRANCH_W), 0.02),
        'rwkv_lnx_b': nrm(ks[19], (DEPTH, BRANCH_W), 0.02),
        'att_sink': nrm(ks[20], (DEPTH, ATT_HEADS), 1.0),
        'conv_dw': nrm(ks[21], (DEPTH, CONV_K, BRANCH_W), CONV_K ** -0.5),
        'conv_dw_b': nrm(ks[22], (DEPTH, BRANCH_W), 0.02),
        'conv_ln_g': 1.0 + nrm(ks[23], (DEPTH, BRANCH_W), 0.02),
        'conv_ln_b': nrm(ks[24], (DEPTH, BRANCH_W), 0.02),
        'w_branch': nrm(ks[25], (DEPTH, N_BRANCH, BRANCH_W, D), BRANCH_W ** -0.5),
        'w_out': nrm(ks[26], (DEPTH, D, D), D ** -0.5),
        'w_mlp1': nrm(ks[27], (DEPTH, D, MLP_HIDDEN), D ** -0.5),
        'w_mlp2': nrm(ks[28], (DEPTH, MLP_HIDDEN, D), MLP_HIDDEN ** -0.5),
        'final_g': 1.0 + nrm(ks[29], (D,), 0.02),
    }


def reference(x, c, ctx, c_ctx, ada_w, ada_b, norm1_g, norm2_g, w_in, rwkv_mu, rwkv_w0, rwkv_w_up,
              rwkv_a0, rwkv_a_up, rwkv_k_k, rwkv_k_a, rwkv_r_k, rwkv_g_up, rwkv_lnx_g, rwkv_lnx_b,
              att_sink, conv_dw, conv_dw_b, conv_ln_g, conv_ln_b, w_branch, w_out, w_mlp1, w_mlp2,
              final_g):
    D = D_MODEL
    rows = x.shape[1] // GRID_W
    row_ids = jnp.repeat(jnp.arange(rows, dtype=jnp.int32), GRID_W)
    col_ids = jnp.tile(jnp.arange(GRID_W, dtype=jnp.int32), rows)
    silu_c = jax.nn.silu(c)
    silu_cc = jax.nn.silu(c_ctx)
    h_ctx = ctx
    for l in range(DEPTH):
        with_ctx = l < DEPTH - 1
        lp = {
            'w_in': w_in[l], 'rwkv_mu': rwkv_mu[l], 'rwkv_w0': rwkv_w0[l], 'rwkv_w_up': rwkv_w_up[l],
            'rwkv_a0': rwkv_a0[l], 'rwkv_a_up': rwkv_a_up[l], 'rwkv_k_k': rwkv_k_k[l],
            'rwkv_k_a': rwkv_k_a[l], 'rwkv_r_k': rwkv_r_k[l], 'rwkv_g_up': rwkv_g_up[l],
            'rwkv_lnx_g': rwkv_lnx_g[l], 'rwkv_lnx_b': rwkv_lnx_b[l], 'att_sink': att_sink[l],
            'conv_dw': conv_dw[l], 'conv_dw_b': conv_dw_b[l], 'conv_ln_g': conv_ln_g[l],
            'conv_ln_b': conv_ln_b[l], 'w_branch': w_branch[l], 'w_out': w_out[l],
        }
        mod_x = silu_c @ ada_w[l] + ada_b[l]
        sh1, sc1, g1, sh2, sc2, g2 = jnp.split(mod_x[:, None, :], 6, axis=-1)
        n_mod = 6 if with_ctx else 2
        mod_c = jnp.split(silu_cc @ ada_w[l][:, :n_mod * D] + ada_b[l][:n_mod * D], n_mod)
        hx = rmsnorm(x, norm1_g[l]) * (1.0 + sc1) + sh1
        hc = rmsnorm(h_ctx, norm1_g[l]) * (1.0 + mod_c[1]) + mod_c[0]
        mix_x, mix_c = hybrid_mixer(hx, hc, row_ids, col_ids, lp, with_ctx)
        x = x + g1 * mix_x
        x = x + g2 * sq_relu_mlp(rmsnorm(x, norm2_g[l]) * (1.0 + sc2) + sh2, w_mlp1[l], w_mlp2[l])
        if with_ctx:
            h_ctx = h_ctx + mod_c[2] * mix_c
            hc2 = rmsnorm(h_ctx, norm2_g[l]) * (1.0 + mod_c[4]) + mod_c[3]
            h_ctx = h_ctx + mod_c[5] * sq_relu_mlp(hc2, w_mlp1[l], w_mlp2[l])
    return rmsnorm(x, final_g)
```

```python
import functools
import math

import numpy as np
import jax
import jax.numpy as jnp
from jax import lax
from jax.experimental import pallas as pl
from jax.experimental.pallas import tpu as pltpu

F32 = jnp.float32
BF16 = jnp.bfloat16

D_MODEL = 2048
GRID_W = 64
NORM_EPS = 1e-6
N_BRANCH = 4
BRANCH_W = D_MODEL // N_BRANCH
FNO_GROUPS = 4
FNO_GROUP_W = BRANCH_W // FNO_GROUPS
RWKV_HEAD = 64
RWKV_HEADS = BRANCH_W // RWKV_HEAD
N_DIR = 2
DECAY_LORA = 64
AICL_LORA = 64
GATE_LORA = 128
DIR_LORA_W = DECAY_LORA + AICL_LORA
SHIFT_W = 3 * BRANCH_W + DIR_LORA_W
GN_EPS = 64e-5
ATT_HEAD = 64
ATT_HEADS = BRANCH_W // ATT_HEAD
ATT_KV_HEADS = 2
ATT_REP = ATT_HEADS // ATT_KV_HEADS
ATT_KV_W = ATT_KV_HEADS * ATT_HEAD
WINDOW = 128
BLOCK = 128
ROPE_BASE = 10000.0
NEG_INF = -1e30
CONV_K = 31
CONV_PAD = (CONV_K - 1) // 2
LN_EPS = 1e-5
MLP_HIDDEN = 4 * D_MODEL

O_LORA = 3 * BRANCH_W
O_KV = O_LORA + N_DIR * DIR_LORA_W
CTX_STATE_COLS = O_KV + 2 * ATT_KV_W
O_G = CTX_STATE_COLS
O_Q = O_G + GATE_LORA
O_FNO = O_Q + BRANCH_W
O_CONV = O_FNO + BRANCH_W
O_GATE = O_CONV + 2 * BRANCH_W
IN_W = O_GATE + N_BRANCH * D_MODEL

LANES = 128
ROW_TILE = 256
SCAN_CHUNK = 64
PAIR_W = 2 * RWKV_HEAD
N_PAIRS = BRANCH_W // PAIR_W
HALO = 16
ATT_SLAB = 64
VMEM_LIMIT = 56 * 1024 * 1024

NT_DIMS = (((1,), (1,)), ((), ()))
NN_DIMS = (((1,), (0,)), ((), ()))
TN_DIMS = (((0,), (0,)), ((), ()))


def _params(sem):
    return pltpu.CompilerParams(dimension_semantics=sem, vmem_limit_bytes=VMEM_LIMIT)


def _split2(a):
    hi = a.astype(BF16)
    lo = (a - hi.astype(F32)).astype(BF16)
    return hi, lo


def _dot3(a, b, dims=NN_DIMS):
    ah, al = _split2(a)
    bh, bl = _split2(b)
    dg = functools.partial(lax.dot_general, dimension_numbers=dims, preferred_element_type=F32)
    return dg(ah, bh) + (dg(ah, bl) + dg(al, bh))


def _dot_sel(a, sel_bf16):
    hi = a.astype(BF16)
    r1 = a - hi.astype(F32)
    mid = r1.astype(BF16)
    lo = (r1 - mid.astype(F32)).astype(BF16)
    dg = functools.partial(jnp.dot, preferred_element_type=F32)
    return dg(hi, sel_bf16) + (dg(mid, sel_bf16) + dg(lo, sel_bf16))


def _sigmoid(x):
    return 1.0 / (1.0 + jnp.exp(-x))


def _ada_kernel(a_ref, w_ref, b_ref, o_ref):
    a = a_ref[...]
    s = a * _sigmoid(a)
    o_ref[...] = jnp.dot(s, w_ref[...], preferred_element_type=F32,
                         precision=lax.Precision.HIGHEST) + b_ref[...]


def _ada_mod(cond, w, b):
    n = w.shape[1]
    tn = 1024
    return pl.pallas_call(
        _ada_kernel,
        grid=(n // tn,),
        in_specs=[pl.BlockSpec((8, D_MODEL), lambda j: (0, 0)),
                  pl.BlockSpec((D_MODEL, tn), lambda j: (0, j)),
                  pl.BlockSpec((1, tn), lambda j: (0, j))],
        out_specs=pl.BlockSpec((8, tn), lambda j: (0, j)),
        out_shape=jax.ShapeDtypeStruct((8, n), F32),
        compiler_params=_params(("parallel",)),
        name="ada_mod",
    )(cond, w, b.reshape(1, n))


class _Geom:
    def __init__(self, batch, ctx_len, seq):
        assert ctx_len % ROW_TILE == 0 and seq % ROW_TILE == 0
        assert seq % GRID_W == 0 and seq % BLOCK == 0 and ctx_len % BLOCK == 0
        self.batch = batch
        self.ctx_len = ctx_len
        self.seq = seq
        self.lt = ctx_len + seq
        self.m = batch * self.lt
        self.tiles_b = self.lt // ROW_TILE
        self.ctx_tiles = ctx_len // ROW_TILE
        self.tiles = batch * self.tiles_b

    def mod_row(self, i):
        return 2 * (i // self.tiles_b) + ((i % self.tiles_b) >= self.ctx_tiles).astype(jnp.int32)


def _norm_kernel(x_ref, g_ref, *rest, rows):
    x = x_ref[...]
    y = x * lax.rsqrt(jnp.mean(x * x, axis=-1, keepdims=True) + NORM_EPS) * g_ref[...]
    if rows is None:
        (o_ref,) = rest
    else:
        mod_ref, o_ref = rest
        mod = mod_ref[0]
        y = y * (1.0 + mod[rows[1]:rows[1] + 1]) + mod[rows[0]:rows[0] + 1]
    o_ref[...] = y.astype(o_ref.dtype)


def _norm_mod(geom, x, g, modtab, rows):
    return pl.pallas_call(
        functools.partial(_norm_kernel, rows=rows),
        grid=(geom.tiles,),
        in_specs=[pl.BlockSpec((ROW_TILE, D_MODEL), lambda i: (i, 0)),
                  pl.BlockSpec((1, D_MODEL), lambda i: (0, 0)),
                  pl.BlockSpec((1, 6, D_MODEL), lambda i: (geom.mod_row(i), 0, 0))],
        out_specs=pl.BlockSpec((ROW_TILE, D_MODEL), lambda i: (i, 0)),
        out_shape=jax.ShapeDtypeStruct((geom.m, D_MODEL), BF16),
        compiler_params=_params(("parallel",)),
        name="norm_mod",
    )(x, g.reshape(1, D_MODEL), modtab)


def _final_norm(geom, x, g):
    per_b = geom.seq // ROW_TILE

    def in_map(i):
        return ((i // per_b) * geom.tiles_b + geom.ctx_tiles + i % per_b, 0)

    return pl.pallas_call(
        functools.partial(_norm_kernel, rows=None),
        grid=(geom.batch * per_b,),
        in_specs=[pl.BlockSpec((ROW_TILE, D_MODEL), in_map),
                  pl.BlockSpec((1, D_MODEL), lambda i: (0, 0))],
        out_specs=pl.BlockSpec((ROW_TILE, D_MODEL), lambda i: (i, 0)),
        out_shape=jax.ShapeDtypeStruct((geom.batch * geom.seq, D_MODEL), F32),
        compiler_params=_params(("parallel",)),
        name="final_norm",
    )(x, g.reshape(1, D_MODEL))


def _mm_kernel(a_ref, b_ref, *rest, epilogue, gate_row):
    if epilogue == "resid":
        res_ref, *mod_refs, o_ref, acc_ref = rest
    else:
        o_ref, acc_ref = rest
    k = pl.program_id(2)

    @pl.when(k == 0)
    def _():
        acc_ref[...] = jnp.zeros_like(acc_ref)

    acc_ref[...] += jnp.dot(a_ref[...], b_ref[...], preferred_element_type=F32)

    @pl.when(k == pl.num_programs(2) - 1)
    def _():
        if epilogue == "resid":
            for s, mod_ref in enumerate(mod_refs):
                rs = slice(s * ROW_TILE, (s + 1) * ROW_TILE)
                o_ref[rs] = res_ref[rs] + mod_ref[0][gate_row:gate_row + 1] * acc_ref[rs]
            return
        acc = acc_ref[...]
        if epilogue == "sigmoid":
            acc = _sigmoid(acc)
        elif epilogue == "relu2":
            acc = jnp.square(jnp.maximum(acc, 0.0))
        o_ref[...] = acc.astype(o_ref.dtype)


def _pick_tile(n, cap, unit=LANES):
    t = (min(cap, n) // unit) * unit
    while n % t:
        t -= unit
    return t


def _matmul(a, b, out_dtype, epilogue="none", geom=None, res=None, modtab=None, gate_row=0,
            tm_cap=512, tn_cap=1024, tk_cap=2048, name="matmul"):
    m, kdim = a.shape
    n = b.shape[1]
    tm = _pick_tile(m, tm_cap, ROW_TILE)
    tn = _pick_tile(n, tn_cap)
    tk = _pick_tile(kdim, tk_cap)
    in_specs = [pl.BlockSpec((tm, tk), lambda i, j, k: (i, k)),
                pl.BlockSpec((tk, tn), lambda i, j, k: (k, j))]
    args = [a, b]
    if epilogue == "resid":
        n_sub = tm // ROW_TILE
        in_specs.append(pl.BlockSpec((tm, tn), lambda i, j, k: (i, j)))
        args.append(res)
        for s in range(n_sub):
            in_specs.append(pl.BlockSpec(
                (1, 6, tn), lambda i, j, k, s=s: (geom.mod_row(i * n_sub + s), 0, j)))
            args.append(modtab)
    return pl.pallas_call(
        functools.partial(_mm_kernel, epilogue=epilogue, gate_row=gate_row),
        grid=(m // tm, n // tn, kdim // tk),
        in_specs=in_specs,
        out_specs=pl.BlockSpec((tm, tn), lambda i, j, k: (i, j)),
        out_shape=jax.ShapeDtypeStruct((m, n), out_dtype),
        scratch_shapes=[pltpu.VMEM((tm, tn), F32)],
        compiler_params=_params(("parallel", "parallel", "arbitrary")),
        name=name,
    )(*args)


def _head_sum_matrix(width, head):
    idx = np.arange(width) // head
    return jnp.asarray((idx[:, None] == idx[None, :]).astype(np.float32), dtype=BF16)


def _prep_kernel(p_ref, lora_ref, halo_ref, hlora_ref, mu_ref, w0_ref, wup_ref, a0_ref, aup_ref,
                 kk_ref, ka_ref, hs_ref, r_o, lw_o, k_o, v_o, kkn_o, b_o, *, tiles_b, ctx_tiles):
    d = pl.program_id(0)
    j = pl.program_id(1) % tiles_b
    fwd = d == 0
    f = jnp.concatenate([p_ref[...], lora_ref[...]], axis=-1)
    t = f.shape[0]
    halo = jnp.concatenate([halo_ref[...], hlora_ref[...]], axis=-1)
    at_start = (j == 0) | (j == ctx_tiles)
    at_end = (j == ctx_tiles - 1) | (j == tiles_b - 1)
    edge = jnp.where(fwd, halo[7:8], halo[0:1])
    edge = jnp.where((fwd & at_start) | (jnp.logical_not(fwd) & at_end), 0.0, edge)
    row = lax.broadcasted_iota(jnp.int32, (t, 1), 0)
    prev = jnp.where(row == 0, edge, pltpu.roll(f, 1, 0))
    nxt = jnp.where(row == t - 1, edge, pltpu.roll(f, t - 1, 0))
    f = f + mu_ref[0] * (jnp.where(fwd, prev, nxt) - f)

    r = f[:, 0:BRANCH_W]
    k = f[:, BRANCH_W:2 * BRANCH_W]
    v = f[:, 2 * BRANCH_W:3 * BRANCH_W]
    wl = f[:, 3 * BRANCH_W:3 * BRANCH_W + DECAY_LORA]
    al = f[:, 3 * BRANCH_W + DECAY_LORA:SHIFT_W]
    w_raw = w0_ref[0] + _dot3(jnp.tanh(wl), wup_ref[0])
    lw = -math.exp(-0.5) * _sigmoid(w_raw)
    a = _sigmoid(a0_ref[0] + _dot3(al, aup_ref[0]))
    kk = k * kk_ref[...]
    norm = jnp.sqrt(_dot_sel(kk * kk, hs_ref[...]))
    kk = kk / jnp.maximum(norm, 1e-12)
    r_o[0] = r
    lw_o[0] = lw
    k_o[0] = k * (1.0 + (a - 1.0) * ka_ref[...])
    v_o[0] = v
    kkn_o[0] = kk
    b_o[0] = kk * a


def _rwkv_prep(geom, p1, lp):
    m = geom.m
    t = ROW_TILE
    tb = geom.tiles_b
    rkv_w = 3 * BRANCH_W
    lora_blk0 = O_LORA // DIR_LORA_W
    n_blk8 = m // 8

    def halo_idx(d, i):
        before = jnp.maximum(i * (t // 8) - 1, 0)
        after = jnp.minimum((i + 1) * (t // 8), n_blk8 - 1)
        return jnp.where(d == 0, before, after)

    out = jax.ShapeDtypeStruct((N_DIR, m, BRANCH_W), F32)
    ospec = pl.BlockSpec((1, t, BRANCH_W), lambda d, i: (d, i, 0))
    vec = lambda a: a.reshape(1, BRANCH_W)
    dvec = pl.BlockSpec((1, 1, BRANCH_W), lambda d, i: (d, 0, 0))
    return pl.pallas_call(
        functools.partial(_prep_kernel, tiles_b=tb, ctx_tiles=geom.ctx_tiles),
        grid=(N_DIR, geom.tiles),
        in_specs=[pl.BlockSpec((t, rkv_w), lambda d, i: (i, 0)),
                  pl.BlockSpec((t, DIR_LORA_W), lambda d, i: (i, lora_blk0 + d)),
                  pl.BlockSpec((8, rkv_w), lambda d, i: (halo_idx(d, i), 0)),
                  pl.BlockSpec((8, DIR_LORA_W), lambda d, i: (halo_idx(d, i), lora_blk0 + d)),
                  pl.BlockSpec((1, 1, SHIFT_W), lambda d, i: (d, 0, 0)),
                  dvec,
                  pl.BlockSpec((1, DECAY_LORA, BRANCH_W), lambda d, i: (d, 0, 0)),
                  dvec,
                  pl.BlockSpec((1, AICL_LORA, BRANCH_W), lambda d, i: (d, 0, 0)),
                  pl.BlockSpec((1, BRANCH_W), lambda d, i: (0, 0)),
                  pl.BlockSpec((1, BRANCH_W), lambda d, i: (0, 0)),
                  pl.BlockSpec((BRANCH_W, BRANCH_W), lambda d, i: (0, 0))],
        out_specs=[ospec] * 6,
        out_shape=[out] * 6,
        compiler_params=_params(("parallel", "parallel")),
        name="rwkv_prep",
    )(p1, p1, p1, p1, lp["rwkv_mu"].reshape(N_DIR, 1, SHIFT_W),
      lp["rwkv_w0"].reshape(N_DIR, 1, BRANCH_W), lp["rwkv_w_up"],
      lp["rwkv_a0"].reshape(N_DIR, 1, BRANCH_W), lp["rwkv_a_up"],
      vec(lp["rwkv_k_k"]), vec(lp["rwkv_k_a"]), _head_sum_matrix(BRANCH_W, RWKV_HEAD))


def _scan_kernel(*refs):
    (r_f, r_b, lw_f, lw_b, k_f, k_b, v_f, v_b, kk_f, kk_b, b_f, b_b, yf_ref, yb_ref, h_ref) = refs
    c = pl.program_id(1)
    C = SCAN_CHUNK
    W = 2 * C

    @pl.when(c == 0)
    def _():
        h_ref[...] = jnp.zeros_like(h_ref)

    rr = lax.broadcasted_iota(jnp.int32, (C, C), 0)
    cc = lax.broadcasted_iota(jnp.int32, (C, C), 1)
    lane = lax.broadcasted_iota(jnp.int32, (1, PAIR_W), 1)
    m_a = (lane < RWKV_HEAD).astype(F32)
    m_b = 1.0 - m_a
    r2 = lax.broadcasted_iota(jnp.int32, (W, W), 0)
    c2 = lax.broadcasted_iota(jnp.int32, (W, W), 1)
    same = (r2 // C) == (c2 // C)
    eye = (r2 == c2).astype(F32)

    def pairs(x):
        return [jnp.concatenate([x[:, p * PAIR_W:(p + 1) * PAIR_W] * m_a,
                                 x[:, p * PAIR_W:(p + 1) * PAIR_W] * m_b], axis=0) for p in range(N_PAIRS)]

    stacks = {name: [] for name in ("a", "b", "k", "r", "v", "bc", "kc", "pt")}
    strict, incl = [], []
    for sgn, (r_ref, lw_ref, k_ref, v_ref, kk_ref, b_ref) in (
            (1, (r_f, lw_f, k_f, v_f, kk_f, b_f)), (-1, (r_b, lw_b, k_b, v_b, kk_b, b_b))):
        lw = lw_ref[0]
        tri = jnp.where((rr - cc) * sgn >= 0, 1.0, 0.0).astype(BF16)
        lp_in = _dot_sel_lhs(tri, lw)
        tot = jnp.sum(lw, axis=0, keepdims=True)
        e_neg = jnp.exp(-lp_in)
        e_chk = jnp.exp(tot - lp_in)
        p_tot = jnp.exp(tot)
        stacks["a"] += pairs(-kk_ref[0] * jnp.exp(lp_in - lw))
        stacks["b"] += pairs(b_ref[0] * e_neg)
        stacks["k"] += pairs(k_ref[0] * e_neg)
        stacks["r"] += pairs(r_ref[0] * jnp.exp(lp_in))
        stacks["v"] += pairs(v_ref[0])
        stacks["bc"] += pairs(b_ref[0] * e_chk)
        stacks["kc"] += pairs(k_ref[0] * e_chk)
        stacks["pt"] += [p_tot[:, p * PAIR_W:(p + 1) * PAIR_W] for p in range(N_PAIRS)]
        dt = (r2 % C - c2 % C) * sgn
        strict += [(same & (dt > 0)).astype(F32)] * N_PAIRS
        incl += [(same & (dt >= 0)).astype(F32)] * N_PAIRS
    a_s, b_s, k_s, r_s, v_s, bc_s, kc_s, p_tot = (jnp.stack(stacks[n]) for n in
                                                  ("a", "b", "k", "r", "v", "bc", "kc", "pt"))
    strict = jnp.stack(strict)
    incl = jnp.stack(incl)

    big = _bdot3(jnp.concatenate([a_s, r_s], axis=1), jnp.concatenate([b_s, k_s], axis=1), BNT_DIMS)
    l_ab = big[:, :W, :W] * strict
    l_ak = big[:, :W, W:] * strict
    m_rb = big[:, W:, :W] * incl
    m_rk = big[:, W:, W:] * incl
    t_inv = eye + l_ab
    pw = l_ab
    for _ in range(int(math.log2(C)) - 1):
        pw = _bdot3(pw, pw)
        t_inv = t_inv + _bdot3(t_inv, pw)
    x1 = _bdot3(t_inv, jnp.concatenate([a_s, _bdot3(l_ak, v_s)], axis=2))
    x2 = _bdot3(m_rb, x1)
    r_hat = r_s + x2[:, :, :PAIR_W]
    y0 = x2[:, :, PAIR_W:] + _bdot3(m_rk, v_s)
    x3 = _bdot3(bc_s, x1, BTN_DIMS)
    g = eye * p_tot + x3[:, :, :PAIR_W]
    h_inc = x3[:, :, PAIR_W:] + _bdot3(kc_s, v_s, BTN_DIMS)
    x4 = _bdot3(jnp.concatenate([r_hat, g], axis=1), h_ref[...])
    ys = x4[:, :W] + y0
    h_ref[...] = x4[:, W:] + h_inc
    for d, y_ref in enumerate((yf_ref, yb_ref)):
        for p in range(N_PAIRS):
            y_ref[:, p * PAIR_W:(p + 1) * PAIR_W] = ys[d * N_PAIRS + p, :C] + ys[d * N_PAIRS + p, C:]


BNN_DIMS = (((2,), (1,)), ((0,), (0,)))
BNT_DIMS = (((2,), (2,)), ((0,), (0,)))
BTN_DIMS = (((1,), (1,)), ((0,), (0,)))


def _bdot3(a, b, dims=BNN_DIMS):
    ah, al = _split2(a)
    bh, bl = _split2(b)
    dg = functools.partial(lax.dot_general, dimension_numbers=dims, preferred_element_type=F32)
    free = 2 if dims[0][0] == (1,) else 1
    n = a.shape[free]
    both = dg(jnp.concatenate([ah, al], axis=free), bh)
    return (both[:, :n] + both[:, n:]) + dg(ah, bl)


def _dot_sel_lhs(sel_bf16, a):
    hi = a.astype(BF16)
    r1 = a - hi.astype(F32)
    mid = r1.astype(BF16)
    lo = (r1 - mid.astype(F32)).astype(BF16)
    dg = functools.partial(jnp.dot, preferred_element_type=F32)
    return dg(sel_bf16, hi) + (dg(sel_bf16, mid) + dg(sel_bf16, lo))


def _rwkv_scan(geom, ins):
    C = SCAN_CHUNK
    nch = geom.lt // C
    nctx = geom.ctx_len // C

    def rev(c):
        return jnp.where(c < nctx, nctx - 1 - c, nch - 1 + nctx - c)

    fwd = pl.BlockSpec((1, C, BRANCH_W), lambda b, c: (0, b * nch + c, 0))
    bwd = pl.BlockSpec((1, C, BRANCH_W), lambda b, c: (1, b * nch + rev(c), 0))
    out = jax.ShapeDtypeStruct((geom.m, BRANCH_W), F32)
    return pl.pallas_call(
        _scan_kernel,
        grid=(geom.batch, nch),
        in_specs=[fwd, bwd] * 6,
        out_specs=[pl.BlockSpec((C, BRANCH_W), lambda b, c: (b * nch + c, 0)),
                   pl.BlockSpec((C, BRANCH_W), lambda b, c: (b * nch + rev(c), 0))],
        out_shape=[out, out],
        scratch_shapes=[pltpu.VMEM((N_DIR * N_PAIRS, PAIR_W, PAIR_W), F32)],
        compiler_params=_params(("parallel", "arbitrary")),
        name="rwkv_scan",
    )(*[a for a in ins for _ in range(N_DIR)])


def _readout_kernel(yf_ref, yb_ref, r_ref, k_ref, v_ref, p2_ref, gup_ref, rk_ref, lg_ref, lb_ref, hs_ref, o_ref):
    hs = hs_ref[...]
    y = yf_ref[...] + yb_ref[...]
    inv_n = 1.0 / RWKV_HEAD
    mean = _dot_sel(y, hs) * inv_n
    yc = y - mean
    var = _dot_sel(yc * yc, hs) * inv_n
    yn = yc * lax.rsqrt(var + GN_EPS) * lg_ref[...] + lb_ref[...]
    bonus = jnp.zeros_like(y)
    for d in range(N_DIR):
        bonus = bonus + _dot_sel(r_ref[d] * k_ref[d] * rk_ref[d:d + 1], hs) * v_ref[d]
    g = _dot3(_sigmoid(p2_ref[:, 0:GATE_LORA]), gup_ref[...])
    o_ref[...] = ((yn + bonus) * g).astype(o_ref.dtype)


def _rwkv_readout(geom, y, ins, p2, lp):
    t = ROW_TILE
    dspec = pl.BlockSpec((N_DIR, t, BRANCH_W), lambda i: (0, i, 0))
    vspec = pl.BlockSpec((1, BRANCH_W), lambda i: (0, 0))
    r, _, k, v, _, _ = ins
    return pl.pallas_call(
        _readout_kernel,
        grid=(geom.tiles,),
        in_specs=[pl.BlockSpec((t, BRANCH_W), lambda i: (i, 0)),
                  pl.BlockSpec((t, BRANCH_W), lambda i: (i, 0)),
                  dspec, dspec, dspec,
                  pl.BlockSpec((t, GATE_LORA + BRANCH_W), lambda i: (i, 0)),
                  pl.BlockSpec((GATE_LORA, BRANCH_W), lambda i: (0, 0)),
                  pl.BlockSpec((N_DIR, BRANCH_W), lambda i: (0, 0)),
                  vspec, vspec,
                  pl.BlockSpec((BRANCH_W, BRANCH_W), lambda i: (0, 0))],
        out_specs=pl.BlockSpec((t, BRANCH_W), lambda i: (i, 0)),
        out_shape=jax.ShapeDtypeStruct((geom.m, BRANCH_W), BF16),
        compiler_params=_params(("parallel",)),
        name="rwkv_readout",
    )(y[0], y[1], r, k, v, p2, lp["rwkv_g_up"], lp["rwkv_r_k"].reshape(N_DIR, BRANCH_W),
      lp["rwkv_lnx_g"].reshape(1, BRANCH_W), lp["rwkv_lnx_b"].reshape(1, BRANCH_W),
      _head_sum_matrix(BRANCH_W, RWKV_HEAD))


def _rope_tables(geom):
    half = ATT_HEAD // 2
    nf = half // 2
    inv = ROPE_BASE ** (-jnp.arange(nf, dtype=F32) / nf)
    pos = jnp.arange(geom.seq, dtype=jnp.int32)
    row_ang = (pos // GRID_W).astype(F32)[:, None] * inv[None, :]
    col_ang = (pos % GRID_W).astype(F32)[:, None] * inv[None, :]
    cos = jnp.concatenate([jnp.cos(row_ang)] * 2 + [jnp.cos(col_ang)] * 2, axis=-1)
    sin = jnp.concatenate([-jnp.sin(row_ang), jnp.sin(row_ang), -jnp.sin(col_ang), jnp.sin(col_ang)], axis=-1)
    cos = jnp.concatenate([jnp.ones((geom.ctx_len, ATT_HEAD), F32), cos], axis=0)
    sin = jnp.concatenate([jnp.zeros((geom.ctx_len, ATT_HEAD), F32), sin], axis=0)
    return cos, sin


def _rotate(t, cos, sin):
    w = t.shape[-1]
    nf = ATT_HEAD // 4
    lane = lax.broadcasted_iota(jnp.int32, (1, w), 1)
    partner = jnp.where((lane % (2 * nf)) < nf, pltpu.roll(t, w - nf, 1), pltpu.roll(t, nf, 1))
    return t * cos + partner * sin


def _rope_kernel(p2_ref, kv_ref, cq_ref, sq_ref, ck_ref, sk_ref, q_o, k_o, v_o):
    q = p2_ref[:, GATE_LORA:]
    q_o[...] = (_rotate(q, cq_ref[...], sq_ref[...]) * (ATT_HEAD ** -0.5)).astype(q_o.dtype)
    kv = kv_ref[...]
    k_o[...] = _rotate(kv[:, :ATT_KV_W], ck_ref[...], sk_ref[...]).astype(k_o.dtype)
    v_o[...] = kv[:, ATT_KV_W:].astype(v_o.dtype)


def _rope(geom, p1, p2):
    t = ROW_TILE
    tb = geom.tiles_b
    cos, sin = _rope_tables(geom)
    cq, sq = jnp.tile(cos, (1, ATT_HEADS)), jnp.tile(sin, (1, ATT_HEADS))
    ck, sk = jnp.tile(cos, (1, ATT_KV_HEADS)), jnp.tile(sin, (1, ATT_KV_HEADS))
    qspec = pl.BlockSpec((t, BRANCH_W), lambda i: (i % tb, 0))
    kspec = pl.BlockSpec((t, ATT_KV_W), lambda i: (i % tb, 0))
    m = geom.m
    return pl.pallas_call(
        _rope_kernel,
        grid=(geom.tiles,),
        in_specs=[pl.BlockSpec((t, GATE_LORA + BRANCH_W), lambda i: (i, 0)),
                  pl.BlockSpec((t, 2 * ATT_KV_W), lambda i: (i, O_KV // (2 * ATT_KV_W))),
                  qspec, qspec, kspec, kspec],
        out_specs=[pl.BlockSpec((t, BRANCH_W), lambda i: (i, 0)),
                   pl.BlockSpec((t, ATT_KV_W), lambda i: (i, 0)),
                   pl.BlockSpec((t, ATT_KV_W), lambda i: (i, 0))],
        out_shape=[jax.ShapeDtypeStruct((m, BRANCH_W), BF16),
                   jax.ShapeDtypeStruct((m, ATT_KV_W), BF16),
                   jax.ShapeDtypeStruct((m, ATT_KV_W), BF16)],
        compiler_params=_params(("parallel",)),
        name="rope",
    )(p2, p1, cq, sq, ck, sk)


def _attn_kernel(q_ref, kc_ref, vc_ref, kp_ref, ko_ref, kn_ref, vp_ref, vo_ref, vn_ref, sink_ref, o_ref,
                 s_ref, p_ref, *, ctx_blocks, blocks_b):
    j = pl.program_id(1)
    q = q_ref[...]
    k_all = jnp.concatenate([kp_ref[...], ko_ref[...], kn_ref[...], kc_ref[...]], axis=0)
    v_all = jnp.concatenate([vp_ref[...], vo_ref[...], vn_ref[...], vc_ref[...]], axis=0)
    nloc = 3 * BLOCK
    slab = ATT_SLAB
    qi0 = lax.broadcasted_iota(jnp.int32, (slab, nloc), 0)
    ki = lax.broadcasted_iota(jnp.int32, (slab, nloc), 1)
    never = 4 * BLOCK
    prev_off = jnp.where(j > ctx_blocks, 0, never)
    own_hi = jnp.where(j >= ctx_blocks, 2 * BLOCK, BLOCK)
    next_off = 2 * BLOCK - jnp.where((j >= ctx_blocks) & (j < blocks_b - 1), 0, never)
    outs = []
    for g in range(ATT_KV_HEADS):
        gs = slice(g * ATT_HEAD, (g + 1) * ATT_HEAD)
        qg = jnp.concatenate([q[:, (g * ATT_REP + h) * ATT_HEAD:(g * ATT_REP + h + 1) * ATT_HEAD]
                              for h in range(ATT_REP)], axis=0)
        s_ref[g] = lax.dot_general(qg, k_all[:, gs], NT_DIMS, preferred_element_type=F32)
        dens = []
        for blk in range(ATT_REP * BLOCK // slab):
            rs = slice(blk * slab, (blk + 1) * slab)
            head = g * ATT_REP + blk * slab // BLOCK
            qi = qi0 + (blk * slab) % BLOCK
            valid = (((ki < BLOCK) & (ki >= qi + prev_off)) | ((ki >= BLOCK) & (ki < own_hi))
                     | ((ki >= 2 * BLOCK) & (ki <= qi + next_off)))
            s_loc = jnp.where(valid, s_ref[g, rs, :nloc], NEG_INF)
            s_ctx = s_ref[g, rs, nloc:]
            sink = sink_ref[head:head + 1, 0:1]
            mx = jnp.maximum(jnp.maximum(jnp.max(s_loc, axis=-1, keepdims=True),
                                         jnp.max(s_ctx, axis=-1, keepdims=True)), sink)
            e_loc = jnp.exp(s_loc - mx)
            e_ctx = jnp.exp(s_ctx - mx)
            dens.append(jnp.sum(e_loc, axis=-1, keepdims=True) + jnp.sum(e_ctx, axis=-1, keepdims=True)
                        + jnp.exp(sink - mx))
            p_ref[g, rs, :nloc] = e_loc.astype(BF16)
            p_ref[g, rs, nloc:] = e_ctx.astype(BF16)
        o = jnp.dot(p_ref[g], v_all[:, gs], preferred_element_type=F32) / jnp.concatenate(dens, axis=0)
        outs += [o[h * BLOCK:(h + 1) * BLOCK] for h in range(ATT_REP)]
    o_ref[...] = jnp.concatenate(outs, axis=-1).astype(o_ref.dtype)


def _attention(geom, q, k, v, sink):
    nb = geom.lt // BLOCK
    cb = geom.ctx_len // BLOCK
    row = lambda b, j: (b * nb + j, 0)
    prev = lambda b, j: (b * nb + jnp.maximum(j - 1, 0), 0)
    nxt = lambda b, j: (b * nb + jnp.minimum(j + 1, nb - 1), 0)
    ctx = lambda b, j: (b * (geom.lt // geom.ctx_len), 0)
    assert geom.lt % geom.ctx_len == 0
    kvs = lambda f: pl.BlockSpec((BLOCK, ATT_KV_W), f)
    cspec = pl.BlockSpec((geom.ctx_len, ATT_KV_W), ctx)
    sink_tab = jnp.broadcast_to(sink.astype(F32)[:, None], (ATT_HEADS, LANES))
    return pl.pallas_call(
        functools.partial(_attn_kernel, ctx_blocks=cb, blocks_b=nb),
        grid=(geom.batch, nb),
        in_specs=[pl.BlockSpec((BLOCK, BRANCH_W), row), cspec, cspec,
                  kvs(prev), kvs(row), kvs(nxt), kvs(prev), kvs(row), kvs(nxt),
                  pl.BlockSpec((ATT_HEADS, LANES), lambda b, j: (0, 0))],
        out_specs=pl.BlockSpec((BLOCK, BRANCH_W), row),
        out_shape=jax.ShapeDtypeStruct((geom.m, BRANCH_W), BF16),
        scratch_shapes=[pltpu.VMEM((ATT_KV_HEADS, ATT_REP * BLOCK, 3 * BLOCK + geom.ctx_len), F32),
                        pltpu.VMEM((ATT_KV_HEADS, ATT_REP * BLOCK, 3 * BLOCK + geom.ctx_len), BF16)],
        compiler_params=_params(("parallel", "parallel")),
        name="attention",
    )(q, k, v, k, k, k, v, v, v, sink_tab)


def _conv_kernel(u_ref, up_ref, un_ref, dw_ref, db_ref, lg_ref, lb_ref, o_ref, hp_ref, *, tiles_b, ctx_tiles):
    j = pl.program_id(0) % tiles_b
    t = u_ref.shape[0]

    def glu(u):
        return u[:, :BRANCH_W] * _sigmoid(u[:, BRANCH_W:])

    at_start = (j == 0) | (j == ctx_tiles)
    at_end = (j == ctx_tiles - 1) | (j == tiles_b - 1)
    hp_ref[0:HALO] = jnp.where(at_start, 0.0, glu(up_ref[...]))
    hp_ref[HALO:HALO + t] = glu(u_ref[...])
    hp_ref[HALO + t:] = jnp.where(at_end, 0.0, glu(un_ref[...]))
    acc = jnp.zeros((t, BRANCH_W), F32) + db_ref[...]
    for tap in range(CONV_K):
        acc = acc + hp_ref[pl.ds(HALO - CONV_PAD + tap, t)] * dw_ref[tap:tap + 1]
    mean = jnp.mean(acc, axis=-1, keepdims=True)
    cen = acc - mean
    var = jnp.mean(cen * cen, axis=-1, keepdims=True)
    h = cen * lax.rsqrt(var + LN_EPS) * lg_ref[...] + lb_ref[...]
    o_ref[...] = (h * _sigmoid(h)).astype(o_ref.dtype)


def _conv(geom, p4, lp):
    t = ROW_TILE
    nh = geom.m // HALO
    vspec = pl.BlockSpec((1, BRANCH_W), lambda i: (0, 0))
    return pl.pallas_call(
        functools.partial(_conv_kernel, tiles_b=geom.tiles_b, ctx_tiles=geom.ctx_tiles),
        grid=(geom.tiles,),
        in_specs=[pl.BlockSpec((t, 2 * BRANCH_W), lambda i: (i, 0)),
                  pl.BlockSpec((HALO, 2 * BRANCH_W), lambda i: (jnp.maximum(i * (t // HALO) - 1, 0), 0)),
                  pl.BlockSpec((HALO, 2 * BRANCH_W), lambda i: (jnp.minimum((i + 1) * (t // HALO), nh - 1), 0)),
                  pl.BlockSpec((CONV_K, BRANCH_W), lambda i: (0, 0)),
                  vspec, vspec, vspec],
        out_specs=pl.BlockSpec((t, BRANCH_W), lambda i: (i, 0)),
        out_shape=jax.ShapeDtypeStruct((geom.m, BRANCH_W), BF16),
        scratch_shapes=[pltpu.VMEM((t + 2 * HALO, BRANCH_W), F32)],
        compiler_params=_params(("parallel",)),
        name="conformer_conv",
    )(p4, p4, p4, lp["conv_dw"], lp["conv_dw_b"].reshape(1, BRANCH_W),
      lp["conv_ln_g"].reshape(1, BRANCH_W), lp["conv_ln_b"].reshape(1, BRANCH_W))


def _dft_cos_sin(n, scale):
    idx = np.arange(n, dtype=np.int64)
    ang = 2.0 * np.pi * ((idx[:, None] * idx[None, :]) % n).astype(np.float64) / n
    return np.cos(ang) * scale, np.sin(ang) * scale


def _channel_dft():
    c, s = _dft_cos_sin(FNO_GROUP_W, FNO_GROUP_W ** -0.5)
    eye = np.eye(FNO_GROUPS)
    return jnp.asarray(np.concatenate([np.kron(eye, c), np.kron(eye, s)], axis=1), dtype=F32).astype(BF16)


def _dft_pos_kernel(c_ref, s_ref, gc_ref, gs_ref, o_ref):
    o_ref[0] = (jnp.dot(c_ref[...], gc_ref[0], preferred_element_type=F32)
                + jnp.dot(s_ref[...], gs_ref[0], preferred_element_type=F32)).astype(o_ref.dtype)


def _fourier(u):
    bsz, length, _ = u.shape
    gcs = _matmul(u.reshape(bsz * length, BRANCH_W), _channel_dft(), BF16, name="dft_channels")
    gcs = gcs.reshape(bsz, length, 2 * BRANCH_W)
    c, s = _dft_cos_sin(length, length ** -0.5)
    tm = _pick_tile(length, 512, ROW_TILE)
    return pl.pallas_call(
        _dft_pos_kernel,
        grid=(length // tm, bsz),
        in_specs=[pl.BlockSpec((tm, length), lambda i, b: (i, 0)),
                  pl.BlockSpec((tm, length), lambda i, b: (i, 0)),
                  pl.BlockSpec((1, length, BRANCH_W), lambda i, b: (b, 0, 0)),
                  pl.BlockSpec((1, length, BRANCH_W), lambda i, b: (b, 0, 1))],
        out_specs=pl.BlockSpec((1, tm, BRANCH_W), lambda i, b: (b, i, 0)),
        out_shape=jax.ShapeDtypeStruct((bsz, length, BRANCH_W), BF16),
        compiler_params=_params(("parallel", "parallel")),
        name="dft_positions",
    )(jnp.asarray(c, dtype=F32).astype(BF16), jnp.asarray(-s, dtype=F32).astype(BF16), gcs, gcs)


def _merge_kernel(f0, f1, f2, f3, w_ref, g0, g1, g2, g3, o_ref):
    acc = None
    for i, (f, g) in enumerate(((f0, g0), (f1, g1), (f2, g2), (f3, g3))):
        term = jnp.dot(f[...], w_ref[i], preferred_element_type=F32) * g[...].astype(F32)
        acc = term if acc is None else acc + term
    o_ref[...] = acc.astype(o_ref.dtype)


def _merge(geom, feats, w_branch, gate):
    m = geom.m
    tm, tn = _pick_tile(m, 512, ROW_TILE), 1024
    nblk = D_MODEL // tn
    fspec = pl.BlockSpec((tm, BRANCH_W), lambda i, j: (i, 0))
    gspec = lambda br: pl.BlockSpec((tm, tn), lambda i, j: (i, br * nblk + j))
    return pl.pallas_call(
        _merge_kernel,
        grid=(m // tm, nblk),
        in_specs=[fspec] * 4 + [pl.BlockSpec((N_BRANCH, BRANCH_W, tn), lambda i, j: (0, 0, j))]
        + [gspec(br) for br in range(N_BRANCH)],
        out_specs=pl.BlockSpec((tm, tn), lambda i, j: (i, j)),
        out_shape=jax.ShapeDtypeStruct((m, D_MODEL), BF16),
        compiler_params=_params(("parallel", "parallel")),
        name="branch_merge",
    )(*feats, w_branch, gate, gate, gate, gate)


def _mixer(geom, h, xall, modtab, lp):
    w_in = lp["w_in"].astype(BF16)
    p1 = _matmul(h, w_in[:, :CTX_STATE_COLS], F32, name="in_proj_state")
    p2 = _matmul(h, w_in[:, O_G:O_FNO], F32, name="in_proj_gq")
    p3 = _matmul(h, w_in[:, O_FNO:O_CONV], BF16, name="in_proj_fno")
    p4 = _matmul(h, w_in[:, O_CONV:O_GATE], F32, name="in_proj_conv")
    gate = _matmul(h, w_in[:, O_GATE:], BF16, epilogue="sigmoid", name="in_proj_gate")

    ins = _rwkv_prep(geom, p1, lp)
    y = _rwkv_scan(geom, ins)
    rw = _rwkv_readout(geom, y, ins, p2, lp)

    q, k, v = _rope(geom, p1, p2)
    att = _attention(geom, q, k, v, lp["att_sink"])

    cv = _conv(geom, p4, lp)

    p3 = p3.reshape(geom.batch, geom.lt, BRANCH_W)
    fno = jnp.concatenate([_fourier(p3[:, :geom.ctx_len]), _fourier(p3[:, geom.ctx_len:])], axis=1)
    fno = fno.reshape(geom.m, BRANCH_W)

    mixed = _merge(geom, (fno, rw, att, cv), lp["w_branch"].astype(BF16), gate)
    return _matmul(mixed, lp["w_out"].astype(BF16), F32, epilogue="resid", geom=geom, res=xall,
                   modtab=modtab, gate_row=2, name="out_proj")


def kernel(x, c, ctx, c_ctx, ada_w, ada_b, norm1_g, norm2_g, w_in, rwkv_mu, rwkv_w0, rwkv_w_up, rwkv_a0, rwkv_a_up, rwkv_k_k, rwkv_k_a, rwkv_r_k, rwkv_g_up, rwkv_lnx_g, rwkv_lnx_b, att_sink, conv_dw, conv_dw_b, conv_ln_g, conv_ln_b, w_branch, w_out, w_mlp1, w_mlp2, final_g):
    batch, seq, _ = x.shape
    geom = _Geom(batch, ctx.shape[1], seq)
    depth = w_in.shape[0]
    assert batch + 1 <= 8
    cond = jnp.zeros((8, D_MODEL), F32).at[:batch].set(c).at[batch].set(c_ctx)
    xall = jnp.concatenate([ctx, x], axis=1).reshape(geom.m, D_MODEL)
    for l in range(depth):
        lp = {
            "w_in": w_in[l], "rwkv_mu": rwkv_mu[l], "rwkv_w0": rwkv_w0[l], "rwkv_w_up": rwkv_w_up[l],
            "rwkv_a0": rwkv_a0[l], "rwkv_a_up": rwkv_a_up[l], "rwkv_k_k": rwkv_k_k[l],
            "rwkv_k_a": rwkv_k_a[l], "rwkv_r_k": rwkv_r_k[l], "rwkv_g_up": rwkv_g_up[l],
            "rwkv_lnx_g": rwkv_lnx_g[l], "rwkv_lnx_b": rwkv_lnx_b[l], "att_sink": att_sink[l],
            "conv_dw": conv_dw[l], "conv_dw_b": conv_dw_b[l], "conv_ln_g": conv_ln_g[l],
            "conv_ln_b": conv_ln_b[l], "w_branch": w_branch[l], "w_out": w_out[l],
        }
        mod = _ada_mod(cond, ada_w[l], ada_b[l])
        mod_x = mod[:batch].reshape(batch, 1, 6, D_MODEL)
        mod_c = jnp.broadcast_to(mod[batch].reshape(1, 1, 6, D_MODEL), (batch, 1, 6, D_MODEL))
        modtab = jnp.concatenate([mod_c, mod_x], axis=1).reshape(2 * batch, 6, D_MODEL)
        h = _norm_mod(geom, xall, norm1_g[l], modtab, rows=(0, 1))
        xall = _mixer(geom, h, xall, modtab, lp)
        h2 = _norm_mod(geom, xall, norm2_g[l], modtab, rows=(3, 4))
        hid = _matmul(h2, w_mlp1[l].astype(BF16), BF16, epilogue="relu2", name="mlp_up")
        xall = _matmul(hid, w_mlp2[l].astype(BF16), F32, epilogue="resid", geom=geom, res=xall,
                       modtab=modtab, gate_row=5, name="mlp_down")
    return _final_norm(geom, xall, final_g).reshape(batch, seq, D_MODEL)
```

```python
import functools
import math

import numpy as np
import jax
import jax.numpy as jnp
from jax import lax
from jax.experimental import pallas as pl
from jax.experimental.pallas import tpu as pltpu

F32 = jnp.float32
BF16 = jnp.bfloat16

D_MODEL = 2048
GRID_W = 64
NORM_EPS = 1e-6
N_BRANCH = 4
BRANCH_W = D_MODEL // N_BRANCH
FNO_GROUPS = 4
FNO_GROUP_W = BRANCH_W // FNO_GROUPS
RWKV_HEAD = 64
RWKV_HEADS = BRANCH_W // RWKV_HEAD
N_DIR = 2
DECAY_LORA = 64
AICL_LORA = 64
GATE_LORA = 128
DIR_LORA_W = DECAY_LORA + AICL_LORA
SHIFT_W = 3 * BRANCH_W + DIR_LORA_W
GN_EPS = 64e-5
ATT_HEAD = 64
ATT_HEADS = BRANCH_W // ATT_HEAD
ATT_KV_HEADS = 2
ATT_REP = ATT_HEADS // ATT_KV_HEADS
ATT_KV_W = ATT_KV_HEADS * ATT_HEAD
WINDOW = 128
BLOCK = 128
ROPE_BASE = 10000.0
NEG_INF = -1e30
CONV_K = 31
CONV_PAD = (CONV_K - 1) // 2
LN_EPS = 1e-5
MLP_HIDDEN = 4 * D_MODEL

O_LORA = 3 * BRANCH_W
O_KV = O_LORA + N_DIR * DIR_LORA_W
CTX_STATE_COLS = O_KV + 2 * ATT_KV_W
O_G = CTX_STATE_COLS
O_Q = O_G + GATE_LORA
O_FNO = O_Q + BRANCH_W
O_CONV = O_FNO + BRANCH_W
O_GATE = O_CONV + 2 * BRANCH_W
IN_W = O_GATE + N_BRANCH * D_MODEL

LANES = 128
ROW_TILE = 256
SCAN_CHUNK = 64
PAIR_W = 2 * RWKV_HEAD
N_PAIRS = BRANCH_W // PAIR_W
HALO = 16
ATT_SLAB = 64
VMEM_LIMIT = 56 * 1024 * 1024

NT_DIMS = (((1,), (1,)), ((), ()))
NN_DIMS = (((1,), (0,)), ((), ()))
TN_DIMS = (((0,), (0,)), ((), ()))


def _params(sem):
    return pltpu.CompilerParams(dimension_semantics=sem, vmem_limit_bytes=VMEM_LIMIT)


def _split2(a):
    hi = a.astype(BF16)
    lo = (a - hi.astype(F32)).astype(BF16)
    return hi, lo


def _dot3(a, b, dims=NN_DIMS):
    ah, al = _split2(a)
    bh, bl = _split2(b)
    dg = functools.partial(lax.dot_general, dimension_numbers=dims, preferred_element_type=F32)
    return dg(ah, bh) + (dg(ah, bl) + dg(al, bh))


def _dot_sel(a, sel_bf16):
    hi = a.astype(BF16)
    r1 = a - hi.astype(F32)
    mid = r1.astype(BF16)
    lo = (r1 - mid.astype(F32)).astype(BF16)
    dg = functools.partial(jnp.dot, preferred_element_type=F32)
    return dg(hi, sel_bf16) + (dg(mid, sel_bf16) + dg(lo, sel_bf16))


def _sigmoid(x):
    return 1.0 / (1.0 + jnp.exp(-x))


def _ada_kernel(a_ref, w_ref, b_ref, o_ref):
    a = a_ref[...]
    s = a * _sigmoid(a)
    o_ref[...] = jnp.dot(s, w_ref[...], preferred_element_type=F32,
                         precision=lax.Precision.HIGHEST) + b_ref[...]


def _ada_mod(cond, w, b):
    n = w.shape[1]
    tn = 1024
    return pl.pallas_call(
        _ada_kernel,
        grid=(n // tn,),
        in_specs=[pl.BlockSpec((8, D_MODEL), lambda j: (0, 0)),
                  pl.BlockSpec((D_MODEL, tn), lambda j: (0, j)),
                  pl.BlockSpec((1, tn), lambda j: (0, j))],
        out_specs=pl.BlockSpec((8, tn), lambda j: (0, j)),
        out_shape=jax.ShapeDtypeStruct((8, n), F32),
        compiler_params=_params(("parallel",)),
        name="ada_mod",
    )(cond, w, b.reshape(1, n))


class _Geom:
    def __init__(self, batch, ctx_len, seq):
        assert ctx_len % ROW_TILE == 0 and seq % ROW_TILE == 0
        assert seq % GRID_W == 0 and seq % BLOCK == 0 and ctx_len % BLOCK == 0
        self.batch = batch
        self.ctx_len = ctx_len
        self.seq = seq
        self.lt = ctx_len + seq
        self.m = batch * self.lt
        self.tiles_b = self.lt // ROW_TILE
        self.ctx_tiles = ctx_len // ROW_TILE
        self.tiles = batch * self.tiles_b

    def mod_row(self, i):
        return 2 * (i // self.tiles_b) + ((i % self.tiles_b) >= self.ctx_tiles).astype(jnp.int32)


def _norm_kernel(x_ref, g_ref, *rest, rows):
    x = x_ref[...]
    y = x * lax.rsqrt(jnp.mean(x * x, axis=-1, keepdims=True) + NORM_EPS) * g_ref[...]
    if rows is None:
        (o_ref,) = rest
    else:
        mod_ref, o_ref = rest
        mod = mod_ref[0]
        y = y * (1.0 + mod[rows[1]:rows[1] + 1]) + mod[rows[0]:rows[0] + 1]
    o_ref[...] = y.astype(o_ref.dtype)


def _norm_mod(geom, x, g, modtab, rows):
    return pl.pallas_call(
        functools.partial(_norm_kernel, rows=rows),
        grid=(geom.tiles,),
        in_specs=[pl.BlockSpec((ROW_TILE, D_MODEL), lambda i: (i, 0)),
                  pl.BlockSpec((1, D_MODEL), lambda i: (0, 0)),
                  pl.BlockSpec((1, 6, D_MODEL), lambda i: (geom.mod_row(i), 0, 0))],
        out_specs=pl.BlockSpec((ROW_TILE, D_MODEL), lambda i: (i, 0)),
        out_shape=jax.ShapeDtypeStruct((geom.m, D_MODEL), BF16),
        compiler_params=_params(("parallel",)),
        name="norm_mod",
    )(x, g.reshape(1, D_MODEL), modtab)


def _final_norm(geom, x, g):
    per_b = geom.seq // ROW_TILE

    def in_map(i):
        return ((i // per_b) * geom.tiles_b + geom.ctx_tiles + i % per_b, 0)

    return pl.pallas_call(
        functools.partial(_norm_kernel, rows=None),
        grid=(geom.batch * per_b,),
        in_specs=[pl.BlockSpec((ROW_TILE, D_MODEL), in_map),
                  pl.BlockSpec((1, D_MODEL), lambda i: (0, 0))],
        out_specs=pl.BlockSpec((ROW_TILE, D_MODEL), lambda i: (i, 0)),
        out_shape=jax.ShapeDtypeStruct((geom.batch * geom.seq, D_MODEL), F32),
        compiler_params=_params(("parallel",)),
        name="final_norm",
    )(x, g.reshape(1, D_MODEL))


def _mm_store(acc, res_ref, mod_refs, o_ref, epilogue, gate_row):
    if epilogue == "resid":
        for s, mod_ref in enumerate(mod_refs):
            rs = slice(s * ROW_TILE, (s + 1) * ROW_TILE)
            o_ref[rs] = res_ref[rs] + mod_ref[0][gate_row:gate_row + 1] * acc[rs]
        return
    acc = acc[...]
    if epilogue == "sigmoid":
        acc = _sigmoid(acc)
    elif epilogue == "relu2":
        acc = jnp.square(jnp.maximum(acc, 0.0))
    o_ref[...] = acc.astype(o_ref.dtype)


def _mm_kernel(a_ref, b_ref, *rest, epilogue, gate_row, nk):
    res_ref, mod_refs = (rest[0], rest[1:-2 if nk > 1 else -1]) if epilogue == "resid" else (None, ())
    if nk == 1:
        o_ref = rest[-1]
        acc = jnp.dot(a_ref[...], b_ref[...], preferred_element_type=F32)
        _mm_store(acc, res_ref, mod_refs, o_ref, epilogue, gate_row)
        return
    o_ref, acc_ref = rest[-2:]
    k = pl.program_id(2)

    @pl.when(k == 0)
    def _():
        acc_ref[...] = jnp.zeros_like(acc_ref)

    acc_ref[...] += jnp.dot(a_ref[...], b_ref[...], preferred_element_type=F32)

    @pl.when(k == nk - 1)
    def _():
        _mm_store(acc_ref, res_ref, mod_refs, o_ref, epilogue, gate_row)


def _mm_w32_kernel(a_ref, w_ref, *rest, epilogue, gate_row):
    res_ref, mod_refs = (rest[0], rest[1:-2]) if epilogue == "resid" else (None, ())
    o_ref, wb_ref = rest[-2:]

    @pl.when(pl.program_id(1) == 0)
    def _():
        wb_ref[...] = w_ref[0].astype(BF16)

    acc = jnp.dot(a_ref[...], wb_ref[...], preferred_element_type=F32)
    _mm_store(acc, res_ref, mod_refs, o_ref, epilogue, gate_row)


def _pick_tile(n, cap, unit=LANES):
    t = (min(cap, n) // unit) * unit
    while n % t:
        t -= unit
    return t


def _resid_operands(geom, res, modtab, tm, tn, row_col):
    n_sub = tm // ROW_TILE
    specs = [pl.BlockSpec((tm, tn), lambda *g: row_col(*g))]
    args = [res]
    for s in range(n_sub):
        specs.append(pl.BlockSpec(
            (1, 6, tn), lambda *g, s=s: (geom.mod_row(row_col(*g)[0] * n_sub + s), 0, row_col(*g)[1])))
        args.append(modtab)
    return specs, args


def _matmul(a, b, out_dtype, epilogue="none", geom=None, res=None, modtab=None, gate_row=0,
            tm_cap=1024, tn_cap=1024, tk_cap=2048, name="matmul"):
    m, kdim = a.shape
    n = b.shape[1]
    tm = _pick_tile(m, tm_cap, ROW_TILE)
    tn = _pick_tile(n, tn_cap)
    tk = _pick_tile(kdim, tk_cap)
    nk = kdim // tk
    in_specs = [pl.BlockSpec((tm, tk), lambda i, j, k: (i, k)),
                pl.BlockSpec((tk, tn), lambda i, j, k: (k, j))]
    args = [a, b]
    if epilogue == "resid":
        specs, extra = _resid_operands(geom, res, modtab, tm, tn, lambda i, j, k: (i, j))
        in_specs += specs
        args += extra
    return pl.pallas_call(
        functools.partial(_mm_kernel, epilogue=epilogue, gate_row=gate_row, nk=nk),
        grid=(m // tm, n // tn, nk),
        in_specs=in_specs,
        out_specs=pl.BlockSpec((tm, tn), lambda i, j, k: (i, j)),
        out_shape=jax.ShapeDtypeStruct((m, n), out_dtype),
        scratch_shapes=[pltpu.VMEM((tm, tn), F32)] if nk > 1 else [],
        compiler_params=_params(("parallel", "parallel", "arbitrary")),
        name=name,
    )(*args)


def _matmul_w32(a, w, layer, out_dtype, epilogue="none", geom=None, res=None, modtab=None, gate_row=0,
                tm_cap=1024, tn_cap=1024, name="matmul_w32"):
    m, kdim = a.shape
    n = w.shape[2]
    tm = _pick_tile(m, tm_cap, ROW_TILE)
    tn = _pick_tile(n, tn_cap)
    in_specs = [pl.BlockSpec((tm, kdim), lambda j, i: (i, 0)),
                pl.BlockSpec((1, kdim, tn), lambda j, i: (layer, 0, j))]
    args = [a, w]
    if epilogue == "resid":
        specs, extra = _resid_operands(geom, res, modtab, tm, tn, lambda j, i: (i, j))
        in_specs += specs
        args += extra
    return pl.pallas_call(
        functools.partial(_mm_w32_kernel, epilogue=epilogue, gate_row=gate_row),
        grid=(n // tn, m // tm),
        in_specs=in_specs,
        out_specs=pl.BlockSpec((tm, tn), lambda j, i: (i, j)),
        out_shape=jax.ShapeDtypeStruct((m, n), out_dtype),
        scratch_shapes=[pltpu.VMEM((kdim, tn), BF16)],
        compiler_params=_params(("parallel", "arbitrary")),
        name=name,
    )(*args)


def _head_sum_matrix(width, head):
    idx = np.arange(width) // head
    return jnp.asarray((idx[:, None] == idx[None, :]).astype(np.float32), dtype=BF16)


def _prep_kernel(p_ref, lora_ref, halo_ref, hlora_ref, mu_ref, w0_ref, wup_ref, a0_ref, aup_ref,
                 kk_ref, ka_ref, hs_ref, r_o, lw_o, k_o, v_o, kkn_o, b_o, *, tiles_b, ctx_tiles):
    d = pl.program_id(0)
    j = pl.program_id(1) % tiles_b
    fwd = d == 0
    f = jnp.concatenate([p_ref[...], lora_ref[...]], axis=-1)
    t = f.shape[0]
    halo = jnp.concatenate([halo_ref[...], hlora_ref[...]], axis=-1)
    at_start = (j == 0) | (j == ctx_tiles)
    at_end = (j == ctx_tiles - 1) | (j == tiles_b - 1)
    edge = jnp.where(fwd, halo[7:8], halo[0:1])
    edge = jnp.where((fwd & at_start) | (jnp.logical_not(fwd) & at_end), 0.0, edge)
    row = lax.broadcasted_iota(jnp.int32, (t, 1), 0)
    prev = jnp.where(row == 0, edge, pltpu.roll(f, 1, 0))
    nxt = jnp.where(row == t - 1, edge, pltpu.roll(f, t - 1, 0))
    f = f + mu_ref[0] * (jnp.where(fwd, prev, nxt) - f)

    r = f[:, 0:BRANCH_W]
    k = f[:, BRANCH_W:2 * BRANCH_W]
    v = f[:, 2 * BRANCH_W:3 * BRANCH_W]
    wl = f[:, 3 * BRANCH_W:3 * BRANCH_W + DECAY_LORA]
    al = f[:, 3 * BRANCH_W + DECAY_LORA:SHIFT_W]
    w_raw = w0_ref[0] + _dot3(jnp.tanh(wl), wup_ref[0])
    lw = -math.exp(-0.5) * _sigmoid(w_raw)
    a = _sigmoid(a0_ref[0] + _dot3(al, aup_ref[0]))
    kk = k * kk_ref[...]
    norm = jnp.sqrt(_dot_sel(kk * kk, hs_ref[...]))
    kk = kk / jnp.maximum(norm, 1e-12)
    r_o[0] = r
    lw_o[0] = lw
    k_o[0] = k * (1.0 + (a - 1.0) * ka_ref[...])
    v_o[0] = v
    kkn_o[0] = kk
    b_o[0] = kk * a


def _rwkv_prep(geom, p1, lp):
    m = geom.m
    t = ROW_TILE
    tb = geom.tiles_b
    rkv_w = 3 * BRANCH_W
    lora_blk0 = O_LORA // DIR_LORA_W
    n_blk8 = m // 8

    def halo_idx(d, i):
        before = jnp.maximum(i * (t // 8) - 1, 0)
        after = jnp.minimum((i + 1) * (t // 8), n_blk8 - 1)
        return jnp.where(d == 0, before, after)

    out = jax.ShapeDtypeStruct((N_DIR, m, BRANCH_W), F32)
    ospec = pl.BlockSpec((1, t, BRANCH_W), lambda d, i: (d, i, 0))
    vec = lambda a: a.reshape(1, BRANCH_W)
    dvec = pl.BlockSpec((1, 1, BRANCH_W), lambda d, i: (d, 0, 0))
    return pl.pallas_call(
        functools.partial(_prep_kernel, tiles_b=tb, ctx_tiles=geom.ctx_tiles),
        grid=(N_DIR, geom.tiles),
        in_specs=[pl.BlockSpec((t, rkv_w), lambda d, i: (i, 0)),
                  pl.BlockSpec((t, DIR_LORA_W), lambda d, i: (i, lora_blk0 + d)),
                  pl.BlockSpec((8, rkv_w), lambda d, i: (halo_idx(d, i), 0)),
                  pl.BlockSpec((8, DIR_LORA_W), lambda d, i: (halo_idx(d, i), lora_blk0 + d)),
                  pl.BlockSpec((1, 1, SHIFT_W), lambda d, i: (d, 0, 0)),
                  dvec,
                  pl.BlockSpec((1, DECAY_LORA, BRANCH_W), lambda d, i: (d, 0, 0)),
                  dvec,
                  pl.BlockSpec((1, AICL_LORA, BRANCH_W), lambda d, i: (d, 0, 0)),
                  pl.BlockSpec((1, BRANCH_W), lambda d, i: (0, 0)),
                  pl.BlockSpec((1, BRANCH_W), lambda d, i: (0, 0)),
                  pl.BlockSpec((BRANCH_W, BRANCH_W), lambda d, i: (0, 0))],
        out_specs=[ospec] * 6,
        out_shape=[out] * 6,
        compiler_params=_params(("parallel", "parallel")),
        name="rwkv_prep",
    )(p1, p1, p1, p1, lp["rwkv_mu"].reshape(N_DIR, 1, SHIFT_W),
      lp["rwkv_w0"].reshape(N_DIR, 1, BRANCH_W), lp["rwkv_w_up"],
      lp["rwkv_a0"].reshape(N_DIR, 1, BRANCH_W), lp["rwkv_a_up"],
      vec(lp["rwkv_k_k"]), vec(lp["rwkv_k_a"]), _head_sum_matrix(BRANCH_W, RWKV_HEAD))


def _scan_kernel(*refs):
    (r_f, r_b, lw_f, lw_b, k_f, k_b, v_f, v_b, kk_f, kk_b, b_f, b_b, yf_ref, yb_ref, h_ref) = refs
    c = pl.program_id(1)
    C = SCAN_CHUNK
    W = 2 * C

    @pl.when(c == 0)
    def _():
        h_ref[...] = jnp.zeros_like(h_ref)

    rr = lax.broadcasted_iota(jnp.int32, (C, C), 0)
    cc = lax.broadcasted_iota(jnp.int32, (C, C), 1)
    lane = lax.broadcasted_iota(jnp.int32, (1, PAIR_W), 1)
    m_a = (lane < RWKV_HEAD).astype(F32)
    m_b = 1.0 - m_a
    r2 = lax.broadcasted_iota(jnp.int32, (W, W), 0)
    c2 = lax.broadcasted_iota(jnp.int32, (W, W), 1)
    same = (r2 // C) == (c2 // C)
    eye = (r2 == c2).astype(F32)

    def pairs(x):
        return [jnp.concatenate([x[:, p * PAIR_W:(p + 1) * PAIR_W] * m_a,
                                 x[:, p * PAIR_W:(p + 1) * PAIR_W] * m_b], axis=0) for p in range(N_PAIRS)]

    stacks = {name: [] for name in ("a", "b", "k", "r", "v", "bc", "kc", "pt")}
    strict, incl = [], []
    for sgn, (r_ref, lw_ref, k_ref, v_ref, kk_ref, b_ref) in (
            (1, (r_f, lw_f, k_f, v_f, kk_f, b_f)), (-1, (r_b, lw_b, k_b, v_b, kk_b, b_b))):
        lw = lw_ref[0]
        tri = jnp.where((rr - cc) * sgn >= 0, 1.0, 0.0).astype(BF16)
        lp_in = _dot_sel_lhs(tri, lw)
        tot = jnp.sum(lw, axis=0, keepdims=True)
        e_neg = jnp.exp(-lp_in)
        e_chk = jnp.exp(tot - lp_in)
        p_tot = jnp.exp(tot)
        stacks["a"] += pairs(-kk_ref[0] * jnp.exp(lp_in - lw))
        stacks["b"] += pairs(b_ref[0] * e_neg)
        stacks["k"] += pairs(k_ref[0] * e_neg)
        stacks["r"] += pairs(r_ref[0] * jnp.exp(lp_in))
        stacks["v"] += pairs(v_ref[0])
        stacks["bc"] += pairs(b_ref[0] * e_chk)
        stacks["kc"] += pairs(k_ref[0] * e_chk)
        stacks["pt"] += [p_tot[:, p * PAIR_W:(p + 1) * PAIR_W] for p in range(N_PAIRS)]
        dt = (r2 % C - c2 % C) * sgn
        strict += [(same & (dt > 0)).astype(F32)] * N_PAIRS
        incl += [(same & (dt >= 0)).astype(F32)] * N_PAIRS
    a_s, b_s, k_s, r_s, v_s, bc_s, kc_s, p_tot = (jnp.stack(stacks[n]) for n in
                                                  ("a", "b", "k", "r", "v", "bc", "kc", "pt"))
    strict = jnp.stack(strict)
    incl = jnp.stack(incl)

    a_b, b_b, k_b, r_b, v_b, bc_b, kc_b = (x.astype(BF16) for x in (a_s, b_s, k_s, r_s, v_s, bc_s, kc_s))
    big = _bdot(jnp.concatenate([a_b, r_b], axis=1), jnp.concatenate([b_b, k_b], axis=1), BNT_DIMS)
    l_ab = big[:, :W, :W] * strict
    l_ak = big[:, :W, W:] * strict
    m_rb = (big[:, W:, :W] * incl).astype(BF16)
    m_rk = big[:, W:, W:] * incl
    t_inv = eye + l_ab
    pw = l_ab
    for _ in range(int(math.log2(C)) - 1):
        pw_b = pw.astype(BF16)
        pw = _bdot(pw_b, pw_b)
        t_inv = t_inv + _bdot(t_inv, pw)
    x1 = _bdot(t_inv, jnp.concatenate([a_b, _bdot(l_ak, v_b).astype(BF16)], axis=2)).astype(BF16)
    x2 = _bdot(m_rb, x1)
    r_hat = r_s + x2[:, :, :PAIR_W]
    y0 = x2[:, :, PAIR_W:] + _bdot(m_rk, v_b)
    x3 = _bdot(bc_b, x1, BTN_DIMS)
    g = eye * p_tot + x3[:, :, :PAIR_W]
    h_inc = x3[:, :, PAIR_W:] + _bdot(kc_b, v_b, BTN_DIMS)
    x4 = _bdot(jnp.concatenate([r_hat, g], axis=1), h_ref[...])
    ys = x4[:, :W] + y0
    h_ref[...] = x4[:, W:] + h_inc
    for d, y_ref in enumerate((yf_ref, yb_ref)):
        for p in range(N_PAIRS):
            y_ref[:, p * PAIR_W:(p + 1) * PAIR_W] = ys[d * N_PAIRS + p, :C] + ys[d * N_PAIRS + p, C:]


BNN_DIMS = (((2,), (1,)), ((0,), (0,)))
BNT_DIMS = (((2,), (2,)), ((0,), (0,)))
BTN_DIMS = (((1,), (1,)), ((0,), (0,)))


def _bdot(a, b, dims=BNN_DIMS):
    return lax.dot_general(a.astype(BF16), b.astype(BF16), dims, preferred_element_type=F32)


def _dot_sel_lhs(sel_bf16, a):
    hi = a.astype(BF16)
    r1 = a - hi.astype(F32)
    mid = r1.astype(BF16)
    lo = (r1 - mid.astype(F32)).astype(BF16)
    dg = functools.partial(jnp.dot, preferred_element_type=F32)
    return dg(sel_bf16, hi) + (dg(sel_bf16, mid) + dg(sel_bf16, lo))


def _rwkv_scan(geom, ins):
    C = SCAN_CHUNK
    nch = geom.lt // C
    nctx = geom.ctx_len // C

    def rev(c):
        return jnp.where(c < nctx, nctx - 1 - c, nch - 1 + nctx - c)

    fwd = pl.BlockSpec((1, C, BRANCH_W), lambda b, c: (0, b * nch + c, 0))
    bwd = pl.BlockSpec((1, C, BRANCH_W), lambda b, c: (1, b * nch + rev(c), 0))
    out = jax.ShapeDtypeStruct((geom.m, BRANCH_W), F32)
    return pl.pallas_call(
        _scan_kernel,
        grid=(geom.batch, nch),
        in_specs=[fwd, bwd] * 6,
        out_specs=[pl.BlockSpec((C, BRANCH_W), lambda b, c: (b * nch + c, 0)),
                   pl.BlockSpec((C, BRANCH_W), lambda b, c: (b * nch + rev(c), 0))],
        out_shape=[out, out],
        scratch_shapes=[pltpu.VMEM((N_DIR * N_PAIRS, PAIR_W, PAIR_W), F32)],
        compiler_params=_params(("parallel", "arbitrary")),
        name="rwkv_scan",
    )(*[a for a in ins for _ in range(N_DIR)])


def _readout_kernel(yf_ref, yb_ref, r_ref, k_ref, v_ref, p2_ref, gup_ref, rk_ref, lg_ref, lb_ref, hs_ref, o_ref):
    hs = hs_ref[...]
    y = yf_ref[...] + yb_ref[...]
    inv_n = 1.0 / RWKV_HEAD
    mean = _dot_sel(y, hs) * inv_n
    yc = y - mean
    var = _dot_sel(yc * yc, hs) * inv_n
    yn = yc * lax.rsqrt(var + GN_EPS) * lg_ref[...] + lb_ref[...]
    bonus = jnp.zeros_like(y)
    for d in range(N_DIR):
        bonus = bonus + _dot_sel(r_ref[d] * k_ref[d] * rk_ref[d:d + 1], hs) * v_ref[d]
    g = _dot3(_sigmoid(p2_ref[:, 0:GATE_LORA]), gup_ref[...])
    o_ref[...] = ((yn + bonus) * g).astype(o_ref.dtype)


def _rwkv_readout(geom, y, ins, p2, lp):
    t = ROW_TILE
    dspec = pl.BlockSpec((N_DIR, t, BRANCH_W), lambda i: (0, i, 0))
    vspec = pl.BlockSpec((1, BRANCH_W), lambda i: (0, 0))
    r, _, k, v, _, _ = ins
    return pl.pallas_call(
        _readout_kernel,
        grid=(geom.tiles,),
        in_specs=[pl.BlockSpec((t, BRANCH_W), lambda i: (i, 0)),
                  pl.BlockSpec((t, BRANCH_W), lambda i: (i, 0)),
                  dspec, dspec, dspec,
                  pl.BlockSpec((t, GATE_LORA + BRANCH_W), lambda i: (i, 0)),
                  pl.BlockSpec((GATE_LORA, BRANCH_W), lambda i: (0, 0)),
                  pl.BlockSpec((N_DIR, BRANCH_W), lambda i: (0, 0)),
                  vspec, vspec,
                  pl.BlockSpec((BRANCH_W, BRANCH_W), lambda i: (0, 0))],
        out_specs=pl.BlockSpec((t, BRANCH_W), lambda i: (i, 0)),
        out_shape=jax.ShapeDtypeStruct((geom.m, BRANCH_W), BF16),
        compiler_params=_params(("parallel",)),
        name="rwkv_readout",
    )(y[0], y[1], r, k, v, p2, lp["rwkv_g_up"], lp["rwkv_r_k"].reshape(N_DIR, BRANCH_W),
      lp["rwkv_lnx_g"].reshape(1, BRANCH_W), lp["rwkv_lnx_b"].reshape(1, BRANCH_W),
      _head_sum_matrix(BRANCH_W, RWKV_HEAD))


def _rope_tables(geom):
    half = ATT_HEAD // 2
    nf = half // 2
    inv = ROPE_BASE ** (-jnp.arange(nf, dtype=F32) / nf)
    pos = jnp.arange(geom.seq, dtype=jnp.int32)
    row_ang = (pos // GRID_W).astype(F32)[:, None] * inv[None, :]
    col_ang = (pos % GRID_W).astype(F32)[:, None] * inv[None, :]
    cos = jnp.concatenate([jnp.cos(row_ang)] * 2 + [jnp.cos(col_ang)] * 2, axis=-1)
    sin = jnp.concatenate([-jnp.sin(row_ang), jnp.sin(row_ang), -jnp.sin(col_ang), jnp.sin(col_ang)], axis=-1)
    cos = jnp.concatenate([jnp.ones((geom.ctx_len, ATT_HEAD), F32), cos], axis=0)
    sin = jnp.concatenate([jnp.zeros((geom.ctx_len, ATT_HEAD), F32), sin], axis=0)
    return cos, sin


def _rotate(t, cos, sin):
    w = t.shape[-1]
    nf = ATT_HEAD // 4
    lane = lax.broadcasted_iota(jnp.int32, (1, w), 1)
    partner = jnp.where((lane % (2 * nf)) < nf, pltpu.roll(t, w - nf, 1), pltpu.roll(t, nf, 1))
    return t * cos + partner * sin


def _rope_kernel(p2_ref, kv_ref, cq_ref, sq_ref, ck_ref, sk_ref, q_o, k_o, v_o):
    q = p2_ref[:, GATE_LORA:]
    q_o[...] = (_rotate(q, cq_ref[...], sq_ref[...]) * (ATT_HEAD ** -0.5)).astype(q_o.dtype)
    kv = kv_ref[...]
    k_o[...] = _rotate(kv[:, :ATT_KV_W], ck_ref[...], sk_ref[...]).astype(k_o.dtype)
    v_o[...] = kv[:, ATT_KV_W:].astype(v_o.dtype)


def _rope(geom, p1, p2):
    t = ROW_TILE
    tb = geom.tiles_b
    cos, sin = _rope_tables(geom)
    cq, sq = jnp.tile(cos, (1, ATT_HEADS)), jnp.tile(sin, (1, ATT_HEADS))
    ck, sk = jnp.tile(cos, (1, ATT_KV_HEADS)), jnp.tile(sin, (1, ATT_KV_HEADS))
    qspec = pl.BlockSpec((t, BRANCH_W), lambda i: (i % tb, 0))
    kspec = pl.BlockSpec((t, ATT_KV_W), lambda i: (i % tb, 0))
    m = geom.m
    return pl.pallas_call(
        _rope_kernel,
        grid=(geom.tiles,),
        in_specs=[pl.BlockSpec((t, GATE_LORA + BRANCH_W), lambda i: (i, 0)),
                  pl.BlockSpec((t, 2 * ATT_KV_W), lambda i: (i, O_KV // (2 * ATT_KV_W))),
                  qspec, qspec, kspec, kspec],
        out_specs=[pl.BlockSpec((t, BRANCH_W), lambda i: (i, 0)),
                   pl.BlockSpec((t, ATT_KV_W), lambda i: (i, 0)),
                   pl.BlockSpec((t, ATT_KV_W), lambda i: (i, 0))],
        out_shape=[jax.ShapeDtypeStruct((m, BRANCH_W), BF16),
                   jax.ShapeDtypeStruct((m, ATT_KV_W), BF16),
                   jax.ShapeDtypeStruct((m, ATT_KV_W), BF16)],
        compiler_params=_params(("parallel",)),
        name="rope",
    )(p2, p1, cq, sq, ck, sk)


def _attn_kernel(q_ref, kc_ref, vc_ref, kp_ref, ko_ref, kn_ref, vp_ref, vo_ref, vn_ref, sink_ref, o_ref,
                 s_ref, p_ref, *, ctx_blocks, blocks_b):
    j = pl.program_id(1)
    q = q_ref[...]
    k_all = jnp.concatenate([kp_ref[...], ko_ref[...], kn_ref[...], kc_ref[...]], axis=0)
    v_all = jnp.concatenate([vp_ref[...], vo_ref[...], vn_ref[...], vc_ref[...]], axis=0)
    nloc = 3 * BLOCK
    slab = ATT_SLAB
    qi0 = lax.broadcasted_iota(jnp.int32, (slab, nloc), 0)
    ki = lax.broadcasted_iota(jnp.int32, (slab, nloc), 1)
    never = 4 * BLOCK
    prev_off = jnp.where(j > ctx_blocks, 0, never)
    own_hi = jnp.where(j >= ctx_blocks, 2 * BLOCK, BLOCK)
    next_off = 2 * BLOCK - jnp.where((j >= ctx_blocks) & (j < blocks_b - 1), 0, never)
    outs = []
    for g in range(ATT_KV_HEADS):
        gs = slice(g * ATT_HEAD, (g + 1) * ATT_HEAD)
        qg = jnp.concatenate([q[:, (g * ATT_REP + h) * ATT_HEAD:(g * ATT_REP + h + 1) * ATT_HEAD]
                              for h in range(ATT_REP)], axis=0)
        s_ref[g] = lax.dot_general(qg, k_all[:, gs], NT_DIMS, preferred_element_type=F32)
        dens = []
        for blk in range(ATT_REP * BLOCK // slab):
            rs = slice(blk * slab, (blk + 1) * slab)
            head = g * ATT_REP + blk * slab // BLOCK
            qi = qi0 + (blk * slab) % BLOCK
            valid = (((ki < BLOCK) & (ki >= qi + prev_off)) | ((ki >= BLOCK) & (ki < own_hi))
                     | ((ki >= 2 * BLOCK) & (ki <= qi + next_off)))
            s_loc = jnp.where(valid, s_ref[g, rs, :nloc], NEG_INF)
            s_ctx = s_ref[g, rs, nloc:]
            sink = sink_ref[head:head + 1, 0:1]
            mx = jnp.maximum(jnp.maximum(jnp.max(s_loc, axis=-1, keepdims=True),
                                         jnp.max(s_ctx, axis=-1, keepdims=True)), sink)
            e_loc = jnp.exp(s_loc - mx)
            e_ctx = jnp.exp(s_ctx - mx)
            dens.append(jnp.sum(e_loc, axis=-1, keepdims=True) + jnp.sum(e_ctx, axis=-1, keepdims=True)
                        + jnp.exp(sink - mx))
            p_ref[g, rs, :nloc] = e_loc.astype(BF16)
            p_ref[g, rs, nloc:] = e_ctx.astype(BF16)
        o = jnp.dot(p_ref[g], v_all[:, gs], preferred_element_type=F32) / jnp.concatenate(dens, axis=0)
        outs += [o[h * BLOCK:(h + 1) * BLOCK] for h in range(ATT_REP)]
    o_ref[...] = jnp.concatenate(outs, axis=-1).astype(o_ref.dtype)


def _attention(geom, q, k, v, sink):
    nb = geom.lt // BLOCK
    cb = geom.ctx_len // BLOCK
    row = lambda b, j: (b * nb + j, 0)
    prev = lambda b, j: (b * nb + jnp.maximum(j - 1, 0), 0)
    nxt = lambda b, j: (b * nb + jnp.minimum(j + 1, nb - 1), 0)
    ctx = lambda b, j: (b * (geom.lt // geom.ctx_len), 0)
    assert geom.lt % geom.ctx_len == 0
    kvs = lambda f: pl.BlockSpec((BLOCK, ATT_KV_W), f)
    cspec = pl.BlockSpec((geom.ctx_len, ATT_KV_W), ctx)
    sink_tab = jnp.broadcast_to(sink.astype(F32)[:, None], (ATT_HEADS, LANES))
    return pl.pallas_call(
        functools.partial(_attn_kernel, ctx_blocks=cb, blocks_b=nb),
        grid=(geom.batch, nb),
        in_specs=[pl.BlockSpec((BLOCK, BRANCH_W), row), cspec, cspec,
                  kvs(prev), kvs(row), kvs(nxt), kvs(prev), kvs(row), kvs(nxt),
                  pl.BlockSpec((ATT_HEADS, LANES), lambda b, j: (0, 0))],
        out_specs=pl.BlockSpec((BLOCK, BRANCH_W), row),
        out_shape=jax.ShapeDtypeStruct((geom.m, BRANCH_W), BF16),
        scratch_shapes=[pltpu.VMEM((ATT_KV_HEADS, ATT_REP * BLOCK, 3 * BLOCK + geom.ctx_len), F32),
                        pltpu.VMEM((ATT_KV_HEADS, ATT_REP * BLOCK, 3 * BLOCK + geom.ctx_len), BF16)],
        compiler_params=_params(("parallel", "parallel")),
        name="attention",
    )(q, k, v, k, k, k, v, v, v, sink_tab)


def _conv_kernel(u_ref, up_ref, un_ref, dw_ref, db_ref, lg_ref, lb_ref, o_ref, hp_ref, *, tiles_b, ctx_tiles):
    j = pl.program_id(0) % tiles_b
    t = u_ref.shape[0]

    def glu(u):
        return u[:, :BRANCH_W] * _sigmoid(u[:, BRANCH_W:])

    at_start = (j == 0) | (j == ctx_tiles)
    at_end = (j == ctx_tiles - 1) | (j == tiles_b - 1)
    hp_ref[0:HALO] = jnp.where(at_start, 0.0, glu(up_ref[...]))
    hp_ref[HALO:HALO + t] = glu(u_ref[...])
    hp_ref[HALO + t:] = jnp.where(at_end, 0.0, glu(un_ref[...]))
    acc = jnp.zeros((t, BRANCH_W), F32) + db_ref[...]
    for tap in range(CONV_K):
        acc = acc + hp_ref[pl.ds(HALO - CONV_PAD + tap, t)] * dw_ref[tap:tap + 1]
    mean = jnp.mean(acc, axis=-1, keepdims=True)
    cen = acc - mean
    var = jnp.mean(cen * cen, axis=-1, keepdims=True)
    h = cen * lax.rsqrt(var + LN_EPS) * lg_ref[...] + lb_ref[...]
    o_ref[...] = (h * _sigmoid(h)).astype(o_ref.dtype)


def _conv(geom, p4, lp):
    t = ROW_TILE
    nh = geom.m // HALO
    vspec = pl.BlockSpec((1, BRANCH_W), lambda i: (0, 0))
    return pl.pallas_call(
        functools.partial(_conv_kernel, tiles_b=geom.tiles_b, ctx_tiles=geom.ctx_tiles),
        grid=(geom.tiles,),
        in_specs=[pl.BlockSpec((t, 2 * BRANCH_W), lambda i: (i, 0)),
                  pl.BlockSpec((HALO, 2 * BRANCH_W), lambda i: (jnp.maximum(i * (t // HALO) - 1, 0), 0)),
                  pl.BlockSpec((HALO, 2 * BRANCH_W), lambda i: (jnp.minimum((i + 1) * (t // HALO), nh - 1), 0)),
                  pl.BlockSpec((CONV_K, BRANCH_W), lambda i: (0, 0)),
                  vspec, vspec, vspec],
        out_specs=pl.BlockSpec((t, BRANCH_W), lambda i: (i, 0)),
        out_shape=jax.ShapeDtypeStruct((geom.m, BRANCH_W), BF16),
        scratch_shapes=[pltpu.VMEM((t + 2 * HALO, BRANCH_W), F32)],
        compiler_params=_params(("parallel",)),
        name="conformer_conv",
    )(p4, p4, p4, lp["conv_dw"], lp["conv_dw_b"].reshape(1, BRANCH_W),
      lp["conv_ln_g"].reshape(1, BRANCH_W), lp["conv_ln_b"].reshape(1, BRANCH_W))


def _dft_cos_sin(n, scale):
    idx = np.arange(n, dtype=np.int64)
    ang = 2.0 * np.pi * ((idx[:, None] * idx[None, :]) % n).astype(np.float64) / n
    return np.cos(ang) * scale, np.sin(ang) * scale


def _channel_dft():
    c, s = _dft_cos_sin(FNO_GROUP_W, FNO_GROUP_W ** -0.5)
    eye = np.eye(FNO_GROUPS)
    return jnp.asarray(np.concatenate([np.kron(eye, c), np.kron(eye, s)], axis=1), dtype=F32).astype(BF16)


def _dft_pos_kernel(c_ref, s_ref, gc_ref, gs_ref, o_ref):
    o_ref[0] = (jnp.dot(c_ref[...], gc_ref[0], preferred_element_type=F32)
                + jnp.dot(s_ref[...], gs_ref[0], preferred_element_type=F32)).astype(o_ref.dtype)


def _fourier(u):
    bsz, length, _ = u.shape
    gcs = _matmul(u.reshape(bsz * length, BRANCH_W), _channel_dft(), BF16, name="dft_channels")
    gcs = gcs.reshape(bsz, length, 2 * BRANCH_W)
    c, s = _dft_cos_sin(length, length ** -0.5)
    tm = _pick_tile(length, 512, ROW_TILE)
    return pl.pallas_call(
        _dft_pos_kernel,
        grid=(length // tm, bsz),
        in_specs=[pl.BlockSpec((tm, length), lambda i, b: (i, 0)),
                  pl.BlockSpec((tm, length), lambda i, b: (i, 0)),
                  pl.BlockSpec((1, length, BRANCH_W), lambda i, b: (b, 0, 0)),
                  pl.BlockSpec((1, length, BRANCH_W), lambda i, b: (b, 0, 1))],
        out_specs=pl.BlockSpec((1, tm, BRANCH_W), lambda i, b: (b, i, 0)),
        out_shape=jax.ShapeDtypeStruct((bsz, length, BRANCH_W), BF16),
        compiler_params=_params(("parallel", "parallel")),
        name="dft_positions",
    )(jnp.asarray(c, dtype=F32).astype(BF16), jnp.asarray(-s, dtype=F32).astype(BF16), gcs, gcs)


def _merge_kernel(f0, f1, f2, f3, w_ref, g0, g1, g2, g3, o_ref):
    acc = None
    for i, (f, g) in enumerate(((f0, g0), (f1, g1), (f2, g2), (f3, g3))):
        term = jnp.dot(f[...], w_ref[i], preferred_element_type=F32) * g[...].astype(F32)
        acc = term if acc is None else acc + term
    o_ref[...] = acc.astype(o_ref.dtype)


def _merge(geom, feats, w_branch, gate):
    m = geom.m
    tm, tn = _pick_tile(m, 512, ROW_TILE), 1024
    nblk = D_MODEL // tn
    fspec = pl.BlockSpec((tm, BRANCH_W), lambda i, j: (i, 0))
    gspec = lambda br: pl.BlockSpec((tm, tn), lambda i, j: (i, br * nblk + j))
    return pl.pallas_call(
        _merge_kernel,
        grid=(m // tm, nblk),
        in_specs=[fspec] * 4 + [pl.BlockSpec((N_BRANCH, BRANCH_W, tn), lambda i, j: (0, 0, j))]
        + [gspec(br) for br in range(N_BRANCH)],
        out_specs=pl.BlockSpec((tm, tn), lambda i, j: (i, j)),
        out_shape=jax.ShapeDtypeStruct((m, D_MODEL), BF16),
        compiler_params=_params(("parallel", "parallel")),
        name="branch_merge",
    )(*feats, w_branch, gate, gate, gate, gate)


def _mixer(geom, h, xall, modtab, lp, w_out, layer):
    w_cols = lambda lo, hi: lp["w_in"][:, lo:hi].astype(BF16)
    p1 = _matmul(h, w_cols(0, CTX_STATE_COLS), F32, name="in_proj_state")
    p2 = _matmul(h, w_cols(O_G, O_FNO), F32, name="in_proj_gq")
    p3 = _matmul(h, w_cols(O_FNO, O_CONV), BF16, name="in_proj_fno")
    p4 = _matmul(h, w_cols(O_CONV, O_GATE), F32, name="in_proj_conv")
    gate = _matmul(h, w_cols(O_GATE, IN_W), BF16, epilogue="sigmoid", name="in_proj_gate")

    ins = _rwkv_prep(geom, p1, lp)
    y = _rwkv_scan(geom, ins)
    rw = _rwkv_readout(geom, y, ins, p2, lp)

    q, k, v = _rope(geom, p1, p2)
    att = _attention(geom, q, k, v, lp["att_sink"])

    cv = _conv(geom, p4, lp)

    p3 = p3.reshape(geom.batch, geom.lt, BRANCH_W)
    fno = jnp.concatenate([_fourier(p3[:, :geom.ctx_len]), _fourier(p3[:, geom.ctx_len:])], axis=1)
    fno = fno.reshape(geom.m, BRANCH_W)

    mixed = _merge(geom, (fno, rw, att, cv), lp["w_branch"].astype(BF16), gate)
    return _matmul_w32(mixed, w_out, layer, F32, epilogue="resid", geom=geom, res=xall,
                       modtab=modtab, gate_row=2, name="out_proj")


def kernel(x, c, ctx, c_ctx, ada_w, ada_b, norm1_g, norm2_g, w_in, rwkv_mu, rwkv_w0, rwkv_w_up, rwkv_a0, rwkv_a_up, rwkv_k_k, rwkv_k_a, rwkv_r_k, rwkv_g_up, rwkv_lnx_g, rwkv_lnx_b, att_sink, conv_dw, conv_dw_b, conv_ln_g, conv_ln_b, w_branch, w_out, w_mlp1, w_mlp2, final_g):
    batch, seq, _ = x.shape
    geom = _Geom(batch, ctx.shape[1], seq)
    depth = w_in.shape[0]
    assert batch + 1 <= 8
    cond = jnp.zeros((8, D_MODEL), F32).at[:batch].set(c).at[batch].set(c_ctx)
    xall = jnp.concatenate([ctx, x], axis=1).reshape(geom.m, D_MODEL)
    for l in range(depth):
        lp = {
            "w_in": w_in[l], "rwkv_mu": rwkv_mu[l], "rwkv_w0": rwkv_w0[l], "rwkv_w_up": rwkv_w_up[l],
            "rwkv_a0": rwkv_a0[l], "rwkv_a_up": rwkv_a_up[l], "rwkv_k_k": rwkv_k_k[l],
            "rwkv_k_a": rwkv_k_a[l], "rwkv_r_k": rwkv_r_k[l], "rwkv_g_up": rwkv_g_up[l],
            "rwkv_lnx_g": rwkv_lnx_g[l], "rwkv_lnx_b": rwkv_lnx_b[l], "att_sink": att_sink[l],
            "conv_dw": conv_dw[l], "conv_dw_b": conv_dw_b[l], "conv_ln_g": conv_ln_g[l],
            "conv_ln_b": conv_ln_b[l], "w_branch": w_branch[l],
        }
        mod = _ada_mod(cond, ada_w[l], ada_b[l])
        mod_x = mod[:batch].reshape(batch, 1, 6, D_MODEL)
        mod_c = jnp.broadcast_to(mod[batch].reshape(1, 1, 6, D_MODEL), (batch, 1, 6, D_MODEL))
        modtab = jnp.concatenate([mod_c, mod_x], axis=1).reshape(2 * batch, 6, D_MODEL)
        h = _norm_mod(geom, xall, norm1_g[l], modtab, rows=(0, 1))
        xall = _mixer(geom, h, xall, modtab, lp, w_out, l)
        h2 = _norm_mod(geom, xall, norm2_g[l], modtab, rows=(3, 4))
        hid = _matmul_w32(h2, w_mlp1, l, BF16, epilogue="relu2", name="mlp_up")
        xall = _matmul(hid, w_mlp2[l].astype(BF16), F32, epilogue="resid", geom=geom, res=xall,
                       modtab=modtab, gate_row=5, name="mlp_down")
    return _final_norm(geom, xall, final_g).reshape(batch, seq, D_MODEL)
```

```python
import functools
import math

import numpy as np
import jax
import jax.numpy as jnp
from jax import lax
from jax.experimental import pallas as pl
from jax.experimental.pallas import tpu as pltpu

F32 = jnp.float32
BF16 = jnp.bfloat16

D_MODEL = 2048
GRID_W = 64
NORM_EPS = 1e-6
N_BRANCH = 4
BRANCH_W = D_MODEL // N_BRANCH
FNO_GROUPS = 4
FNO_GROUP_W = BRANCH_W // FNO_GROUPS
RWKV_HEAD = 64
RWKV_HEADS = BRANCH_W // RWKV_HEAD
N_DIR = 2
DECAY_LORA = 64
AICL_LORA = 64
GATE_LORA = 128
DIR_LORA_W = DECAY_LORA + AICL_LORA
SHIFT_W = 3 * BRANCH_W + DIR_LORA_W
GN_EPS = 64e-5
ATT_HEAD = 64
ATT_HEADS = BRANCH_W // ATT_HEAD
ATT_KV_HEADS = 2
ATT_REP = ATT_HEADS // ATT_KV_HEADS
ATT_KV_W = ATT_KV_HEADS * ATT_HEAD
WINDOW = 128
BLOCK = 128
ROPE_BASE = 10000.0
NEG_INF = -1e30
CONV_K = 31
CONV_PAD = (CONV_K - 1) // 2
LN_EPS = 1e-5
MLP_HIDDEN = 4 * D_MODEL

O_LORA = 3 * BRANCH_W
O_KV = O_LORA + N_DIR * DIR_LORA_W
CTX_STATE_COLS = O_KV + 2 * ATT_KV_W
O_G = CTX_STATE_COLS
O_Q = O_G + GATE_LORA
O_FNO = O_Q + BRANCH_W
O_CONV = O_FNO + BRANCH_W
O_GATE = O_CONV + 2 * BRANCH_W
IN_W = O_GATE + N_BRANCH * D_MODEL

LANES = 128
ROW_TILE = 256
SCAN_CHUNK = 64
PAIR_W = 2 * RWKV_HEAD
N_PAIRS = BRANCH_W // PAIR_W
HALO = 16
ATT_SLAB = 64
SUBLANES = 8
CONV_ROWS = 32
VMEM_LIMIT = 56 * 1024 * 1024

NT_DIMS = (((1,), (1,)), ((), ()))
NN_DIMS = (((1,), (0,)), ((), ()))
TN_DIMS = (((0,), (0,)), ((), ()))


def _params(sem):
    return pltpu.CompilerParams(dimension_semantics=sem, vmem_limit_bytes=VMEM_LIMIT)


def _split2(a):
    hi = a.astype(BF16)
    lo = (a - hi.astype(F32)).astype(BF16)
    return hi, lo


def _dot3(a, b, dims=NN_DIMS):
    ah, al = _split2(a)
    bh, bl = _split2(b)
    dg = functools.partial(lax.dot_general, dimension_numbers=dims, preferred_element_type=F32)
    return dg(ah, bh) + (dg(ah, bl) + dg(al, bh))


def _dot_sel(a, sel_bf16):
    hi = a.astype(BF16)
    r1 = a - hi.astype(F32)
    mid = r1.astype(BF16)
    lo = (r1 - mid.astype(F32)).astype(BF16)
    dg = functools.partial(jnp.dot, preferred_element_type=F32)
    return dg(hi, sel_bf16) + (dg(mid, sel_bf16) + dg(lo, sel_bf16))


def _sigmoid(x):
    return 1.0 / (1.0 + jnp.exp(-x))


def _ada_kernel(a_ref, w_ref, b_ref, o_ref):
    a = a_ref[...]
    s = a * _sigmoid(a)
    o_ref[...] = jnp.dot(s, w_ref[0], preferred_element_type=F32,
                         precision=lax.Precision.HIGHEST) + b_ref[...]


def _ada_mod(cond, w, layer, b):
    n = w.shape[2]
    tn = 1024
    return pl.pallas_call(
        _ada_kernel,
        grid=(n // tn,),
        in_specs=[pl.BlockSpec((8, D_MODEL), lambda j: (0, 0)),
                  pl.BlockSpec((1, D_MODEL, tn), lambda j: (layer, 0, j)),
                  pl.BlockSpec((1, tn), lambda j: (0, j))],
        out_specs=pl.BlockSpec((8, tn), lambda j: (0, j)),
        out_shape=jax.ShapeDtypeStruct((8, n), F32),
        compiler_params=_params(("parallel",)),
        name="ada_mod",
    )(cond, w, b.reshape(1, n))


class _Geom:
    def __init__(self, batch, ctx_len, seq):
        assert ctx_len % ROW_TILE == 0 and seq % ROW_TILE == 0
        assert seq % GRID_W == 0 and seq % BLOCK == 0 and ctx_len % BLOCK == 0
        self.batch = batch
        self.ctx_len = ctx_len
        self.seq = seq
        self.lt = ctx_len + seq
        self.m = batch * self.lt
        self.tiles_b = self.lt // ROW_TILE
        self.ctx_tiles = ctx_len // ROW_TILE
        self.tiles = batch * self.tiles_b

    def mod_row(self, i):
        return 2 * (i // self.tiles_b) + ((i % self.tiles_b) >= self.ctx_tiles).astype(jnp.int32)


def _norm_kernel(x_ref, g_ref, *rest, rows):
    x = x_ref[...]
    y = x * lax.rsqrt(jnp.mean(x * x, axis=-1, keepdims=True) + NORM_EPS) * g_ref[...]
    if rows is None:
        (o_ref,) = rest
    else:
        mod_ref, o_ref = rest
        mod = mod_ref[0]
        y = y * (1.0 + mod[rows[1]:rows[1] + 1]) + mod[rows[0]:rows[0] + 1]
    o_ref[...] = y.astype(o_ref.dtype)


def _norm_mod(geom, x, g, modtab, rows):
    return pl.pallas_call(
        functools.partial(_norm_kernel, rows=rows),
        grid=(geom.tiles,),
        in_specs=[pl.BlockSpec((ROW_TILE, D_MODEL), lambda i: (i, 0)),
                  pl.BlockSpec((1, D_MODEL), lambda i: (0, 0)),
                  pl.BlockSpec((1, 6, D_MODEL), lambda i: (geom.mod_row(i), 0, 0))],
        out_specs=pl.BlockSpec((ROW_TILE, D_MODEL), lambda i: (i, 0)),
        out_shape=jax.ShapeDtypeStruct((geom.m, D_MODEL), BF16),
        compiler_params=_params(("parallel",)),
        name="norm_mod",
    )(x, g.reshape(1, D_MODEL), modtab)


def _final_norm(geom, x, g):
    per_b = geom.seq // ROW_TILE

    def in_map(i):
        return ((i // per_b) * geom.tiles_b + geom.ctx_tiles + i % per_b, 0)

    return pl.pallas_call(
        functools.partial(_norm_kernel, rows=None),
        grid=(geom.batch * per_b,),
        in_specs=[pl.BlockSpec((ROW_TILE, D_MODEL), in_map),
                  pl.BlockSpec((1, D_MODEL), lambda i: (0, 0))],
        out_specs=pl.BlockSpec((ROW_TILE, D_MODEL), lambda i: (i, 0)),
        out_shape=jax.ShapeDtypeStruct((geom.batch * geom.seq, D_MODEL), F32),
        compiler_params=_params(("parallel",)),
        name="final_norm",
    )(x, g.reshape(1, D_MODEL))


def _mm_store(acc, res_ref, mod_refs, o_ref, epilogue, gate_row, norm=None):
    if epilogue == "resid":
        for s, mod_ref in enumerate(mod_refs):
            rs = slice(s * ROW_TILE, (s + 1) * ROW_TILE)
            xn = res_ref[rs] + mod_ref[0][gate_row:gate_row + 1] * acc[rs]
            o_ref[rs] = xn
            if norm is not None:
                g_ref, nmod_refs, h_ref, rows = norm
                nmod = nmod_refs[s][0]
                y = xn * lax.rsqrt(jnp.mean(xn * xn, axis=-1, keepdims=True) + NORM_EPS) * g_ref[...]
                h_ref[rs] = (y * (1.0 + nmod[rows[1]:rows[1] + 1]) + nmod[rows[0]:rows[0] + 1]).astype(h_ref.dtype)
        return
    acc = acc[...]
    if epilogue == "sigmoid":
        acc = _sigmoid(acc)
    elif epilogue == "relu2":
        acc = jnp.square(jnp.maximum(acc, 0.0))
    o_ref[...] = acc.astype(o_ref.dtype)


def _mm_kernel(a_ref, b_ref, *rest, epilogue, gate_row, nk, n_sub, norm_rows):
    rest = list(rest)
    res_ref, mod_refs, norm = None, (), None
    if epilogue == "resid":
        res_ref, mod_refs, rest = rest[0], rest[1:1 + n_sub], rest[1 + n_sub:]
        if norm_rows is not None:
            norm = (rest[0], rest[1:1 + n_sub], rest[2 + n_sub], norm_rows)
            rest = [rest[1 + n_sub]] + rest[3 + n_sub:]
    o_ref = rest[0]
    if nk == 1:
        acc = jnp.dot(a_ref[...], b_ref[...], preferred_element_type=F32)
        _mm_store(acc, res_ref, mod_refs, o_ref, epilogue, gate_row, norm)
        return
    acc_ref = rest[1]
    k = pl.program_id(2)

    @pl.when(k == 0)
    def _():
        acc_ref[...] = jnp.zeros_like(acc_ref)

    acc_ref[...] += jnp.dot(a_ref[...], b_ref[...], preferred_element_type=F32)

    @pl.when(k == nk - 1)
    def _():
        _mm_store(acc_ref, res_ref, mod_refs, o_ref, epilogue, gate_row, norm)


def _mm_w32_kernel(a_ref, w_ref, *rest, epilogue, gate_row):
    res_ref, mod_refs = (rest[0], rest[1:-2]) if epilogue == "resid" else (None, ())
    o_ref, wb_ref = rest[-2:]

    @pl.when(pl.program_id(1) == 0)
    def _():
        wb_ref[...] = w_ref[0].astype(BF16)

    acc = jnp.dot(a_ref[...], wb_ref[...], preferred_element_type=F32)
    _mm_store(acc, res_ref, mod_refs, o_ref, epilogue, gate_row)


def _pick_tile(n, cap, unit=LANES):
    t = (min(cap, n) // unit) * unit
    while n % t:
        t -= unit
    return t


def _resid_operands(geom, res, modtab, tm, tn, row_col):
    n_sub = tm // ROW_TILE
    specs = [pl.BlockSpec((tm, tn), lambda *g: row_col(*g))]
    args = [res]
    for s in range(n_sub):
        specs.append(pl.BlockSpec(
            (1, 6, tn), lambda *g, s=s: (geom.mod_row(row_col(*g)[0] * n_sub + s), 0, row_col(*g)[1])))
        args.append(modtab)
    return specs, args


def _matmul(a, b, out_dtype, epilogue="none", geom=None, res=None, modtab=None, gate_row=0, norm=None,
            tm_cap=1024, tn_cap=1024, tk_cap=2048, name="matmul"):
    m, kdim = a.shape
    n = b.shape[1]
    tm = _pick_tile(m, tm_cap, ROW_TILE)
    tn = n if norm is not None else _pick_tile(n, tn_cap)
    tk = _pick_tile(kdim, tk_cap)
    nk = kdim // tk
    n_sub = tm // ROW_TILE
    in_specs = [pl.BlockSpec((tm, tk), lambda i, j, k: (i, k)),
                pl.BlockSpec((tk, tn), lambda i, j, k: (k, j))]
    args = [a, b]
    out_specs = pl.BlockSpec((tm, tn), lambda i, j, k: (i, j))
    out_shape = jax.ShapeDtypeStruct((m, n), out_dtype)
    if epilogue == "resid":
        specs, extra = _resid_operands(geom, res, modtab, tm, tn, lambda i, j, k: (i, j))
        in_specs += specs
        args += extra
        if norm is not None:
            gain, nmodtab, _ = norm
            specs, extra = _resid_operands(geom, gain.reshape(1, n), nmodtab, tm, tn, lambda i, j, k: (i, j))
            in_specs += [pl.BlockSpec((1, tn), lambda i, j, k: (0, 0))] + specs[1:]
            args += extra
            out_specs = [out_specs, pl.BlockSpec((tm, tn), lambda i, j, k: (i, j))]
            out_shape = [out_shape, jax.ShapeDtypeStruct((m, n), BF16)]
    return pl.pallas_call(
        functools.partial(_mm_kernel, epilogue=epilogue, gate_row=gate_row, nk=nk, n_sub=n_sub,
                          norm_rows=None if norm is None else norm[2]),
        grid=(m // tm, n // tn, nk),
        in_specs=in_specs,
        out_specs=out_specs,
        out_shape=out_shape,
        scratch_shapes=[pltpu.VMEM((tm, tn), F32)] if nk > 1 else [],
        compiler_params=_params(("parallel", "parallel", "arbitrary")),
        name=name,
    )(*args)


def _matmul_w32(a, w, layer, out_dtype, epilogue="none", geom=None, res=None, modtab=None, gate_row=0,
                cols=None, tm_cap=1024, tn_cap=1024, name="matmul_w32"):
    m, kdim = a.shape
    col0, n = cols if cols is not None else (0, w.shape[2])
    assert col0 % LANES == 0
    tm = _pick_tile(m, tm_cap, ROW_TILE)
    tn = _pick_tile(n, tn_cap)
    in_specs = [pl.BlockSpec((tm, kdim), lambda j, i: (i, 0)),
                pl.BlockSpec((pl.Element(1), pl.Element(kdim), pl.Element(tn)),
                             lambda j, i: (layer, 0, pl.multiple_of(col0 + j * tn, LANES)))]
    args = [a, w]
    if epilogue == "resid":
        specs, extra = _resid_operands(geom, res, modtab, tm, tn, lambda j, i: (i, j))
        in_specs += specs
        args += extra
    return pl.pallas_call(
        functools.partial(_mm_w32_kernel, epilogue=epilogue, gate_row=gate_row),
        grid=(n // tn, m // tm),
        in_specs=in_specs,
        out_specs=pl.BlockSpec((tm, tn), lambda j, i: (i, j)),
        out_shape=jax.ShapeDtypeStruct((m, n), out_dtype),
        scratch_shapes=[pltpu.VMEM((kdim, tn), BF16)],
        compiler_params=_params(("parallel", "arbitrary")),
        name=name,
    )(*args)


def _head_sum_matrix(width, head):
    idx = np.arange(width) // head
    return jnp.asarray((idx[:, None] == idx[None, :]).astype(np.float32), dtype=BF16)


def _prep_kernel(p_ref, lora_ref, halo_ref, hlora_ref, mu_ref, w0_ref, wup_ref, a0_ref, aup_ref,
                 kk_ref, ka_ref, hs_ref, r_o, lw_o, k_o, v_o, kkn_o, b_o, *, tiles_b, ctx_tiles):
    d = pl.program_id(0)
    j = pl.program_id(1) % tiles_b
    fwd = d == 0
    f = jnp.concatenate([p_ref[...], lora_ref[...]], axis=-1)
    t = f.shape[0]
    halo = jnp.concatenate([halo_ref[...], hlora_ref[...]], axis=-1)
    at_start = (j == 0) | (j == ctx_tiles)
    at_end = (j == ctx_tiles - 1) | (j == tiles_b - 1)
    edge = jnp.where(fwd, halo[7:8], halo[0:1])
    edge = jnp.where((fwd & at_start) | (jnp.logical_not(fwd) & at_end), 0.0, edge)
    row = lax.broadcasted_iota(jnp.int32, (t, 1), 0)
    prev = jnp.where(row == 0, edge, pltpu.roll(f, 1, 0))
    nxt = jnp.where(row == t - 1, edge, pltpu.roll(f, t - 1, 0))
    f = f + mu_ref[0] * (jnp.where(fwd, prev, nxt) - f)

    r = f[:, 0:BRANCH_W]
    k = f[:, BRANCH_W:2 * BRANCH_W]
    v = f[:, 2 * BRANCH_W:3 * BRANCH_W]
    wl = f[:, 3 * BRANCH_W:3 * BRANCH_W + DECAY_LORA]
    al = f[:, 3 * BRANCH_W + DECAY_LORA:SHIFT_W]
    w_raw = w0_ref[0] + _dot3(jnp.tanh(wl), wup_ref[0])
    lw = -math.exp(-0.5) * _sigmoid(w_raw)
    a = _sigmoid(a0_ref[0] + _dot3(al, aup_ref[0]))
    kk = k * kk_ref[...]
    norm = jnp.sqrt(_dot_sel(kk * kk, hs_ref[...]))
    kk = kk / jnp.maximum(norm, 1e-12)
    r_o[0] = r
    lw_o[0] = lw
    k_o[0] = k * (1.0 + (a - 1.0) * ka_ref[...])
    v_o[0] = v
    kkn_o[0] = kk
    b_o[0] = kk * a


def _rwkv_prep(geom, p1, lp):
    m = geom.m
    t = ROW_TILE
    tb = geom.tiles_b
    rkv_w = 3 * BRANCH_W
    lora_blk0 = O_LORA // DIR_LORA_W
    n_blk8 = m // 8

    def halo_idx(d, i):
        before = jnp.maximum(i * (t // 8) - 1, 0)
        after = jnp.minimum((i + 1) * (t // 8), n_blk8 - 1)
        return jnp.where(d == 0, before, after)

    out = jax.ShapeDtypeStruct((N_DIR, m, BRANCH_W), F32)
    ospec = pl.BlockSpec((1, t, BRANCH_W), lambda d, i: (d, i, 0))
    vec = lambda a: a.reshape(1, BRANCH_W)
    dvec = pl.BlockSpec((1, 1, BRANCH_W), lambda d, i: (d, 0, 0))
    return pl.pallas_call(
        functools.partial(_prep_kernel, tiles_b=tb, ctx_tiles=geom.ctx_tiles),
        grid=(N_DIR, geom.tiles),
        in_specs=[pl.BlockSpec((t, rkv_w), lambda d, i: (i, 0)),
                  pl.BlockSpec((t, DIR_LORA_W), lambda d, i: (i, lora_blk0 + d)),
                  pl.BlockSpec((8, rkv_w), lambda d, i: (halo_idx(d, i), 0)),
                  pl.BlockSpec((8, DIR_LORA_W), lambda d, i: (halo_idx(d, i), lora_blk0 + d)),
                  pl.BlockSpec((1, 1, SHIFT_W), lambda d, i: (d, 0, 0)),
                  dvec,
                  pl.BlockSpec((1, DECAY_LORA, BRANCH_W), lambda d, i: (d, 0, 0)),
                  dvec,
                  pl.BlockSpec((1, AICL_LORA, BRANCH_W), lambda d, i: (d, 0, 0)),
                  pl.BlockSpec((1, BRANCH_W), lambda d, i: (0, 0)),
                  pl.BlockSpec((1, BRANCH_W), lambda d, i: (0, 0)),
                  pl.BlockSpec((BRANCH_W, BRANCH_W), lambda d, i: (0, 0))],
        out_specs=[ospec] * 6,
        out_shape=[out] * 6,
        compiler_params=_params(("parallel", "parallel")),
        name="rwkv_prep",
    )(p1, p1, p1, p1, lp["rwkv_mu"].reshape(N_DIR, 1, SHIFT_W),
      lp["rwkv_w0"].reshape(N_DIR, 1, BRANCH_W), lp["rwkv_w_up"],
      lp["rwkv_a0"].reshape(N_DIR, 1, BRANCH_W), lp["rwkv_a_up"],
      vec(lp["rwkv_k_k"]), vec(lp["rwkv_k_a"]), _head_sum_matrix(BRANCH_W, RWKV_HEAD))


def _scan_kernel(*refs):
    (r_f, r_b, lw_f, lw_b, k_f, k_b, v_f, v_b, kk_f, kk_b, b_f, b_b, yf_ref, yb_ref, h_ref) = refs
    c = pl.program_id(1)
    C = SCAN_CHUNK
    W = 2 * C

    @pl.when(c == 0)
    def _():
        h_ref[...] = jnp.zeros_like(h_ref)

    rr = lax.broadcasted_iota(jnp.int32, (C, C), 0)
    cc = lax.broadcasted_iota(jnp.int32, (C, C), 1)
    lane = lax.broadcasted_iota(jnp.int32, (1, PAIR_W), 1)
    m_a = (lane < RWKV_HEAD).astype(F32)
    m_b = 1.0 - m_a
    r2 = lax.broadcasted_iota(jnp.int32, (W, W), 0)
    c2 = lax.broadcasted_iota(jnp.int32, (W, W), 1)
    same = (r2 // C) == (c2 // C)
    eye = (r2 == c2).astype(F32)

    def pairs(x):
        return [jnp.concatenate([x[:, p * PAIR_W:(p + 1) * PAIR_W] * m_a,
                                 x[:, p * PAIR_W:(p + 1) * PAIR_W] * m_b], axis=0) for p in range(N_PAIRS)]

    stacks = {name: [] for name in ("a", "b", "k", "r", "v", "bc", "kc", "pt")}
    strict, incl = [], []
    for sgn, (r_ref, lw_ref, k_ref, v_ref, kk_ref, b_ref) in (
            (1, (r_f, lw_f, k_f, v_f, kk_f, b_f)), (-1, (r_b, lw_b, k_b, v_b, kk_b, b_b))):
        lw = lw_ref[0]
        tri = jnp.where((rr - cc) * sgn >= 0, 1.0, 0.0).astype(BF16)
        lp_in = _dot_sel_lhs(tri, lw)
        tot = jnp.sum(lw, axis=0, keepdims=True)
        e_neg = jnp.exp(-lp_in)
        e_chk = jnp.exp(tot - lp_in)
        p_tot = jnp.exp(tot)
        stacks["a"] += pairs(-kk_ref[0] * jnp.exp(lp_in - lw))
        stacks["b"] += pairs(b_ref[0] * e_neg)
        stacks["k"] += pairs(k_ref[0] * e_neg)
        stacks["r"] += pairs(r_ref[0] * jnp.exp(lp_in))
        stacks["v"] += pairs(v_ref[0])
        stacks["bc"] += pairs(b_ref[0] * e_chk)
        stacks["kc"] += pairs(k_ref[0] * e_chk)
        stacks["pt"] += [p_tot[:, p * PAIR_W:(p + 1) * PAIR_W] for p in range(N_PAIRS)]
        dt = (r2 % C - c2 % C) * sgn
        strict += [(same & (dt > 0)).astype(F32)] * N_PAIRS
        incl += [(same & (dt >= 0)).astype(F32)] * N_PAIRS
    a_s, b_s, k_s, r_s, v_s, bc_s, kc_s, p_tot = (jnp.stack(stacks[n]) for n in
                                                  ("a", "b", "k", "r", "v", "bc", "kc", "pt"))
    strict = jnp.stack(strict)
    incl = jnp.stack(incl)

    a_b, b_b, k_b, r_b, v_b, bc_b, kc_b = (x.astype(BF16) for x in (a_s, b_s, k_s, r_s, v_s, bc_s, kc_s))
    big = _bdot(jnp.concatenate([a_b, r_b], axis=1), jnp.concatenate([b_b, k_b], axis=1), BNT_DIMS)
    l_ab = big[:, :W, :W] * strict
    l_ak = big[:, :W, W:] * strict
    m_rb = (big[:, W:, :W] * incl).astype(BF16)
    m_rk = big[:, W:, W:] * incl
    t_inv = eye + l_ab
    pw = l_ab
    for _ in range(int(math.log2(C)) - 1):
        pw_b = pw.astype(BF16)
        pw = _bdot(pw_b, pw_b)
        t_inv = t_inv + _bdot(t_inv, pw)
    x1 = _bdot(t_inv, jnp.concatenate([a_b, _bdot(l_ak, v_b).astype(BF16)], axis=2)).astype(BF16)
    x2 = _bdot(m_rb, x1)
    r_hat = r_s + x2[:, :, :PAIR_W]
    y0 = x2[:, :, PAIR_W:] + _bdot(m_rk, v_b)
    x3 = _bdot(bc_b, x1, BTN_DIMS)
    g = eye * p_tot + x3[:, :, :PAIR_W]
    h_inc = x3[:, :, PAIR_W:] + _bdot(kc_b, v_b, BTN_DIMS)
    x4 = _bdot(jnp.concatenate([r_hat, g], axis=1), h_ref[...])
    ys = x4[:, :W] + y0
    h_ref[...] = x4[:, W:] + h_inc
    for d, y_ref in enumerate((yf_ref, yb_ref)):
        for p in range(N_PAIRS):
            y_ref[:, p * PAIR_W:(p + 1) * PAIR_W] = ys[d * N_PAIRS + p, :C] + ys[d * N_PAIRS + p, C:]


BNN_DIMS = (((2,), (1,)), ((0,), (0,)))
BNT_DIMS = (((2,), (2,)), ((0,), (0,)))
BTN_DIMS = (((1,), (1,)), ((0,), (0,)))


def _bdot(a, b, dims=BNN_DIMS):
    return lax.dot_general(a.astype(BF16), b.astype(BF16), dims, preferred_element_type=F32)


def _dot_sel_lhs(sel_bf16, a):
    hi = a.astype(BF16)
    r1 = a - hi.astype(F32)
    mid = r1.astype(BF16)
    lo = (r1 - mid.astype(F32)).astype(BF16)
    dg = functools.partial(jnp.dot, preferred_element_type=F32)
    return dg(sel_bf16, hi) + (dg(sel_bf16, mid) + dg(sel_bf16, lo))


def _rwkv_scan(geom, ins):
    C = SCAN_CHUNK
    nch = geom.lt // C
    nctx = geom.ctx_len // C

    def rev(c):
        return jnp.where(c < nctx, nctx - 1 - c, nch - 1 + nctx - c)

    fwd = pl.BlockSpec((1, C, BRANCH_W), lambda b, c: (0, b * nch + c, 0))
    bwd = pl.BlockSpec((1, C, BRANCH_W), lambda b, c: (1, b * nch + rev(c), 0))
    out = jax.ShapeDtypeStruct((geom.m, BRANCH_W), F32)
    return pl.pallas_call(
        _scan_kernel,
        grid=(geom.batch, nch),
        in_specs=[fwd, bwd] * 6,
        out_specs=[pl.BlockSpec((C, BRANCH_W), lambda b, c: (b * nch + c, 0)),
                   pl.BlockSpec((C, BRANCH_W), lambda b, c: (b * nch + rev(c), 0))],
        out_shape=[out, out],
        scratch_shapes=[pltpu.VMEM((N_DIR * N_PAIRS, PAIR_W, PAIR_W), F32)],
        compiler_params=_params(("parallel", "arbitrary")),
        name="rwkv_scan",
    )(*[a for a in ins for _ in range(N_DIR)])


def _readout_kernel(yf_ref, yb_ref, r_ref, k_ref, v_ref, p2_ref, gup_ref, rk_ref, lg_ref, lb_ref, hs_ref, o_ref):
    hs = hs_ref[...]
    y = yf_ref[...] + yb_ref[...]
    inv_n = 1.0 / RWKV_HEAD
    mean = _dot_sel(y, hs) * inv_n
    yc = y - mean
    var = _dot_sel(yc * yc, hs) * inv_n
    yn = yc * lax.rsqrt(var + GN_EPS) * lg_ref[...] + lb_ref[...]
    bonus = jnp.zeros_like(y)
    for d in range(N_DIR):
        bonus = bonus + _dot_sel(r_ref[d] * k_ref[d] * rk_ref[d:d + 1], hs) * v_ref[d]
    g = _dot3(_sigmoid(p2_ref[:, 0:GATE_LORA]), gup_ref[...])
    o_ref[...] = ((yn + bonus) * g).astype(o_ref.dtype)


def _rwkv_readout(geom, y, ins, p2, lp):
    t = ROW_TILE
    dspec = pl.BlockSpec((N_DIR, t, BRANCH_W), lambda i: (0, i, 0))
    vspec = pl.BlockSpec((1, BRANCH_W), lambda i: (0, 0))
    r, _, k, v, _, _ = ins
    return pl.pallas_call(
        _readout_kernel,
        grid=(geom.tiles,),
        in_specs=[pl.BlockSpec((t, BRANCH_W), lambda i: (i, 0)),
                  pl.BlockSpec((t, BRANCH_W), lambda i: (i, 0)),
                  dspec, dspec, dspec,
                  pl.BlockSpec((t, GATE_LORA + BRANCH_W), lambda i: (i, 0)),
                  pl.BlockSpec((GATE_LORA, BRANCH_W), lambda i: (0, 0)),
                  pl.BlockSpec((N_DIR, BRANCH_W), lambda i: (0, 0)),
                  vspec, vspec,
                  pl.BlockSpec((BRANCH_W, BRANCH_W), lambda i: (0, 0))],
        out_specs=pl.BlockSpec((t, BRANCH_W), lambda i: (i, 0)),
        out_shape=jax.ShapeDtypeStruct((geom.m, BRANCH_W), BF16),
        compiler_params=_params(("parallel",)),
        name="rwkv_readout",
    )(y[0], y[1], r, k, v, p2, lp["rwkv_g_up"], lp["rwkv_r_k"].reshape(N_DIR, BRANCH_W),
      lp["rwkv_lnx_g"].reshape(1, BRANCH_W), lp["rwkv_lnx_b"].reshape(1, BRANCH_W),
      _head_sum_matrix(BRANCH_W, RWKV_HEAD))


def _rope_tables(geom):
    half = ATT_HEAD // 2
    nf = half // 2
    inv = ROPE_BASE ** (-jnp.arange(nf, dtype=F32) / nf)
    pos = jnp.arange(geom.seq, dtype=jnp.int32)
    row_ang = (pos // GRID_W).astype(F32)[:, None] * inv[None, :]
    col_ang = (pos % GRID_W).astype(F32)[:, None] * inv[None, :]
    cos = jnp.concatenate([jnp.cos(row_ang)] * 2 + [jnp.cos(col_ang)] * 2, axis=-1)
    sin = jnp.concatenate([-jnp.sin(row_ang), jnp.sin(row_ang), -jnp.sin(col_ang), jnp.sin(col_ang)], axis=-1)
    cos = jnp.concatenate([jnp.ones((geom.ctx_len, ATT_HEAD), F32), cos], axis=0)
    sin = jnp.concatenate([jnp.zeros((geom.ctx_len, ATT_HEAD), F32), sin], axis=0)
    return cos, sin


def _rotate(t, cos, sin):
    w = t.shape[-1]
    nf = ATT_HEAD // 4
    lane = lax.broadcasted_iota(jnp.int32, (1, w), 1)
    partner = jnp.where((lane % (2 * nf)) < nf, pltpu.roll(t, w - nf, 1), pltpu.roll(t, nf, 1))
    return t * cos + partner * sin


def _rope_kernel(p2_ref, kv_ref, cq_ref, sq_ref, ck_ref, sk_ref, q_o, k_o, v_o):
    q = p2_ref[:, GATE_LORA:]
    q_o[...] = (_rotate(q, cq_ref[...], sq_ref[...]) * (ATT_HEAD ** -0.5)).astype(q_o.dtype)
    kv = kv_ref[...]
    k_o[...] = _rotate(kv[:, :ATT_KV_W], ck_ref[...], sk_ref[...]).astype(k_o.dtype)
    v_o[...] = kv[:, ATT_KV_W:].astype(v_o.dtype)


def _rope(geom, p1, p2):
    t = ROW_TILE
    tb = geom.tiles_b
    cos, sin = _rope_tables(geom)
    cq, sq = jnp.tile(cos, (1, ATT_HEADS)), jnp.tile(sin, (1, ATT_HEADS))
    ck, sk = jnp.tile(cos, (1, ATT_KV_HEADS)), jnp.tile(sin, (1, ATT_KV_HEADS))
    qspec = pl.BlockSpec((t, BRANCH_W), lambda i: (i % tb, 0))
    kspec = pl.BlockSpec((t, ATT_KV_W), lambda i: (i % tb, 0))
    m = geom.m
    return pl.pallas_call(
        _rope_kernel,
        grid=(geom.tiles,),
        in_specs=[pl.BlockSpec((t, GATE_LORA + BRANCH_W), lambda i: (i, 0)),
                  pl.BlockSpec((t, 2 * ATT_KV_W), lambda i: (i, O_KV // (2 * ATT_KV_W))),
                  qspec, qspec, kspec, kspec],
        out_specs=[pl.BlockSpec((t, BRANCH_W), lambda i: (i, 0)),
                   pl.BlockSpec((t, ATT_KV_W), lambda i: (i, 0)),
                   pl.BlockSpec((t, ATT_KV_W), lambda i: (i, 0))],
        out_shape=[jax.ShapeDtypeStruct((m, BRANCH_W), BF16),
                   jax.ShapeDtypeStruct((m, ATT_KV_W), BF16),
                   jax.ShapeDtypeStruct((m, ATT_KV_W), BF16)],
        compiler_params=_params(("parallel",)),
        name="rope",
    )(p2, p1, cq, sq, ck, sk)


def _attn_kernel(q_ref, kc_ref, vc_ref, kp_ref, ko_ref, kn_ref, vp_ref, vo_ref, vn_ref, sink_ref, o_ref,
                 s_ref, p_ref, *, ctx_blocks, blocks_b):
    j = pl.program_id(1)
    q = q_ref[...]
    k_all = jnp.concatenate([kp_ref[...], ko_ref[...], kn_ref[...], kc_ref[...]], axis=0)
    v_all = jnp.concatenate([vp_ref[...], vo_ref[...], vn_ref[...], vc_ref[...]], axis=0)
    nloc = 3 * BLOCK
    slab = ATT_SLAB
    qi0 = lax.broadcasted_iota(jnp.int32, (slab, nloc), 0)
    ki = lax.broadcasted_iota(jnp.int32, (slab, nloc), 1)
    never = 4 * BLOCK
    prev_off = jnp.where(j > ctx_blocks, 0, never)
    own_hi = jnp.where(j >= ctx_blocks, 2 * BLOCK, BLOCK)
    next_off = 2 * BLOCK - jnp.where((j >= ctx_blocks) & (j < blocks_b - 1), 0, never)
    outs = []
    for g in range(ATT_KV_HEADS):
        gs = slice(g * ATT_HEAD, (g + 1) * ATT_HEAD)
        qg = jnp.concatenate([q[:, (g * ATT_REP + h) * ATT_HEAD:(g * ATT_REP + h + 1) * ATT_HEAD]
                              for h in range(ATT_REP)], axis=0)
        s_ref[g] = lax.dot_general(qg, k_all[:, gs], NT_DIMS, preferred_element_type=F32)
        dens = []
        for blk in range(ATT_REP * BLOCK // slab):
            rs = slice(blk * slab, (blk + 1) * slab)
            head = g * ATT_REP + blk * slab // BLOCK
            qi = qi0 + (blk * slab) % BLOCK
            valid = (((ki < BLOCK) & (ki >= qi + prev_off)) | ((ki >= BLOCK) & (ki < own_hi))
                     | ((ki >= 2 * BLOCK) & (ki <= qi + next_off)))
            s_loc = jnp.where(valid, s_ref[g, rs, :nloc], NEG_INF)
            s_ctx = s_ref[g, rs, nloc:]
            sink = sink_ref[head:head + 1, 0:1]
            mx = jnp.maximum(jnp.maximum(jnp.max(s_loc, axis=-1, keepdims=True),
                                         jnp.max(s_ctx, axis=-1, keepdims=True)), sink)
            e_loc = jnp.exp(s_loc - mx)
            e_ctx = jnp.exp(s_ctx - mx)
            dens.append(jnp.sum(e_loc, axis=-1, keepdims=True) + jnp.sum(e_ctx, axis=-1, keepdims=True)
                        + jnp.exp(sink - mx))
            p_ref[g, rs, :nloc] = e_loc.astype(BF16)
            p_ref[g, rs, nloc:] = e_ctx.astype(BF16)
        o = jnp.dot(p_ref[g], v_all[:, gs], preferred_element_type=F32) / jnp.concatenate(dens, axis=0)
        outs += [o[h * BLOCK:(h + 1) * BLOCK] for h in range(ATT_REP)]
    o_ref[...] = jnp.concatenate(outs, axis=-1).astype(o_ref.dtype)


def _attention(geom, q, k, v, sink):
    nb = geom.lt // BLOCK
    cb = geom.ctx_len // BLOCK
    row = lambda b, j: (b * nb + j, 0)
    prev = lambda b, j: (b * nb + jnp.maximum(j - 1, 0), 0)
    nxt = lambda b, j: (b * nb + jnp.minimum(j + 1, nb - 1), 0)
    ctx = lambda b, j: (b * (geom.lt // geom.ctx_len), 0)
    assert geom.lt % geom.ctx_len == 0
    kvs = lambda f: pl.BlockSpec((BLOCK, ATT_KV_W), f)
    cspec = pl.BlockSpec((geom.ctx_len, ATT_KV_W), ctx)
    sink_tab = jnp.broadcast_to(sink.astype(F32)[:, None], (ATT_HEADS, LANES))
    return pl.pallas_call(
        functools.partial(_attn_kernel, ctx_blocks=cb, blocks_b=nb),
        grid=(geom.batch, nb),
        in_specs=[pl.BlockSpec((BLOCK, BRANCH_W), row), cspec, cspec,
                  kvs(prev), kvs(row), kvs(nxt), kvs(prev), kvs(row), kvs(nxt),
                  pl.BlockSpec((ATT_HEADS, LANES), lambda b, j: (0, 0))],
        out_specs=pl.BlockSpec((BLOCK, BRANCH_W), row),
        out_shape=jax.ShapeDtypeStruct((geom.m, BRANCH_W), BF16),
        scratch_shapes=[pltpu.VMEM((ATT_KV_HEADS, ATT_REP * BLOCK, 3 * BLOCK + geom.ctx_len), F32),
                        pltpu.VMEM((ATT_KV_HEADS, ATT_REP * BLOCK, 3 * BLOCK + geom.ctx_len), BF16)],
        compiler_params=_params(("parallel", "parallel")),
        name="attention",
    )(q, k, v, k, k, k, v, v, v, sink_tab)


def _conv_kernel(u_ref, up_ref, un_ref, dw_ref, db_ref, lg_ref, lb_ref, o_ref, hp_ref, sh_ref,
                 *, tiles_b, ctx_tiles):
    j = pl.program_id(0) % tiles_b
    t = u_ref.shape[0]

    def glu(u):
        return u[:, :BRANCH_W] * _sigmoid(u[:, BRANCH_W:])

    at_start = (j == 0) | (j == ctx_tiles)
    at_end = (j == ctx_tiles - 1) | (j == tiles_b - 1)
    hp_ref[0:HALO] = jnp.where(at_start, 0.0, glu(up_ref[...]))
    hp_ref[HALO:HALO + t] = glu(u_ref[...])
    hp_ref[HALO + t:] = jnp.where(at_end, 0.0, glu(un_ref[...]))
    span = t + 2 * HALO - SUBLANES
    for r in range(1, SUBLANES):
        sh_ref[r, 0:span] = hp_ref[pl.ds(r, span)]
    for chunk in range(t // CONV_ROWS):
        acc = jnp.zeros((CONV_ROWS, BRANCH_W), F32) + db_ref[...]
        for tap in range(CONV_K):
            q, r = divmod(HALO - CONV_PAD + tap, SUBLANES)
            rows = pl.ds(chunk * CONV_ROWS + q * SUBLANES, CONV_ROWS)
            src = hp_ref[rows] if r == 0 else sh_ref[r, rows]
            acc = acc + src * dw_ref[tap:tap + 1]
        mean = jnp.mean(acc, axis=-1, keepdims=True)
        cen = acc - mean
        var = jnp.mean(cen * cen, axis=-1, keepdims=True)
        h = cen * lax.rsqrt(var + LN_EPS) * lg_ref[...] + lb_ref[...]
        o_ref[chunk * CONV_ROWS:(chunk + 1) * CONV_ROWS] = (h * _sigmoid(h)).astype(o_ref.dtype)


def _conv(geom, p4, lp):
    t = ROW_TILE
    nh = geom.m // HALO
    vspec = pl.BlockSpec((1, BRANCH_W), lambda i: (0, 0))
    return pl.pallas_call(
        functools.partial(_conv_kernel, tiles_b=geom.tiles_b, ctx_tiles=geom.ctx_tiles),
        grid=(geom.tiles,),
        in_specs=[pl.BlockSpec((t, 2 * BRANCH_W), lambda i: (i, 0)),
                  pl.BlockSpec((HALO, 2 * BRANCH_W), lambda i: (jnp.maximum(i * (t // HALO) - 1, 0), 0)),
                  pl.BlockSpec((HALO, 2 * BRANCH_W), lambda i: (jnp.minimum((i + 1) * (t // HALO), nh - 1), 0)),
                  pl.BlockSpec((CONV_K, BRANCH_W), lambda i: (0, 0)),
                  vspec, vspec, vspec],
        out_specs=pl.BlockSpec((t, BRANCH_W), lambda i: (i, 0)),
        out_shape=jax.ShapeDtypeStruct((geom.m, BRANCH_W), BF16),
        scratch_shapes=[pltpu.VMEM((t + 2 * HALO, BRANCH_W), F32),
                        pltpu.VMEM((SUBLANES, t + 2 * HALO, BRANCH_W), F32)],
        compiler_params=_params(("parallel",)),
        name="conformer_conv",
    )(p4, p4, p4, lp["conv_dw"], lp["conv_dw_b"].reshape(1, BRANCH_W),
      lp["conv_ln_g"].reshape(1, BRANCH_W), lp["conv_ln_b"].reshape(1, BRANCH_W))


def _dft_cos_sin(n, scale):
    idx = np.arange(n, dtype=np.int64)
    ang = 2.0 * np.pi * ((idx[:, None] * idx[None, :]) % n).astype(np.float64) / n
    return np.cos(ang) * scale, np.sin(ang) * scale


def _channel_dft():
    c, s = _dft_cos_sin(FNO_GROUP_W, FNO_GROUP_W ** -0.5)
    eye = np.eye(FNO_GROUPS)
    return jnp.asarray(np.concatenate([np.kron(eye, c), np.kron(eye, s)], axis=1), dtype=F32).astype(BF16)


def _dft_pos_kernel(c_ref, s_ref, gc_ref, gs_ref, o_ref):
    o_ref[0] = (jnp.dot(c_ref[...], gc_ref[0], preferred_element_type=F32)
                + jnp.dot(s_ref[...], gs_ref[0], preferred_element_type=F32)).astype(o_ref.dtype)


def _fourier(u):
    bsz, length, _ = u.shape
    gcs = _matmul(u.reshape(bsz * length, BRANCH_W), _channel_dft(), BF16, name="dft_channels")
    gcs = gcs.reshape(bsz, length, 2 * BRANCH_W)
    c, s = _dft_cos_sin(length, length ** -0.5)
    tm = _pick_tile(length, 512, ROW_TILE)
    return pl.pallas_call(
        _dft_pos_kernel,
        grid=(length // tm, bsz),
        in_specs=[pl.BlockSpec((tm, length), lambda i, b: (i, 0)),
                  pl.BlockSpec((tm, length), lambda i, b: (i, 0)),
                  pl.BlockSpec((1, length, BRANCH_W), lambda i, b: (b, 0, 0)),
                  pl.BlockSpec((1, length, BRANCH_W), lambda i, b: (b, 0, 1))],
        out_specs=pl.BlockSpec((1, tm, BRANCH_W), lambda i, b: (b, i, 0)),
        out_shape=jax.ShapeDtypeStruct((bsz, length, BRANCH_W), BF16),
        compiler_params=_params(("parallel", "parallel")),
        name="dft_positions",
    )(jnp.asarray(c, dtype=F32).astype(BF16), jnp.asarray(-s, dtype=F32).astype(BF16), gcs, gcs)


def _merge_kernel(f0, f1, f2, f3, w_ref, g0, g1, g2, g3, o_ref):
    acc = None
    for i, (f, g) in enumerate(((f0, g0), (f1, g1), (f2, g2), (f3, g3))):
        term = jnp.dot(f[...], w_ref[i], preferred_element_type=F32) * g[...].astype(F32)
        acc = term if acc is None else acc + term
    o_ref[...] = acc.astype(o_ref.dtype)


def _merge(geom, feats, w_branch, gate):
    m = geom.m
    tm, tn = _pick_tile(m, 512, ROW_TILE), 1024
    nblk = D_MODEL // tn
    fspec = pl.BlockSpec((tm, BRANCH_W), lambda i, j: (i, 0))
    gspec = lambda br: pl.BlockSpec((tm, tn), lambda i, j: (i, br * nblk + j))
    return pl.pallas_call(
        _merge_kernel,
        grid=(m // tm, nblk),
        in_specs=[fspec] * 4 + [pl.BlockSpec((N_BRANCH, BRANCH_W, tn), lambda i, j: (0, 0, j))]
        + [gspec(br) for br in range(N_BRANCH)],
        out_specs=pl.BlockSpec((tm, tn), lambda i, j: (i, j)),
        out_shape=jax.ShapeDtypeStruct((m, D_MODEL), BF16),
        compiler_params=_params(("parallel", "parallel")),
        name="branch_merge",
    )(*feats, w_branch, gate, gate, gate, gate)


def _mixer(geom, h, xall, modtab, lp, w_in, layer, norm2_g):
    proj = lambda lo, hi, dt, name, ep="none": _matmul_w32(h, w_in, layer, dt, epilogue=ep,
                                                           cols=(lo, hi - lo), name=name)
    p1 = proj(0, CTX_STATE_COLS, F32, "in_proj_state")
    p2 = proj(O_G, O_FNO, F32, "in_proj_gq")
    p3 = proj(O_FNO, O_CONV, BF16, "in_proj_fno")
    p4 = proj(O_CONV, O_GATE, F32, "in_proj_conv")
    gate = proj(O_GATE, IN_W, BF16, "in_proj_gate", "sigmoid")

    ins = _rwkv_prep(geom, p1, lp)
    y = _rwkv_scan(geom, ins)
    rw = _rwkv_readout(geom, y, ins, p2, lp)

    q, k, v = _rope(geom, p1, p2)
    att = _attention(geom, q, k, v, lp["att_sink"])

    cv = _conv(geom, p4, lp)

    p3 = p3.reshape(geom.batch, geom.lt, BRANCH_W)
    fno = jnp.concatenate([_fourier(p3[:, :geom.ctx_len]), _fourier(p3[:, geom.ctx_len:])], axis=1)
    fno = fno.reshape(geom.m, BRANCH_W)

    mixed = _merge(geom, (fno, rw, att, cv), lp["w_branch"].astype(BF16), gate)
    return _matmul(mixed, lp["w_out"].astype(BF16), F32, epilogue="resid", geom=geom, res=xall,
                   modtab=modtab, gate_row=2, norm=(norm2_g, modtab, (3, 4)), tm_cap=512, name="out_proj")


def kernel(x, c, ctx, c_ctx, ada_w, ada_b, norm1_g, norm2_g, w_in, rwkv_mu, rwkv_w0, rwkv_w_up, rwkv_a0, rwkv_a_up, rwkv_k_k, rwkv_k_a, rwkv_r_k, rwkv_g_up, rwkv_lnx_g, rwkv_lnx_b, att_sink, conv_dw, conv_dw_b, conv_ln_g, conv_ln_b, w_branch, w_out, w_mlp1, w_mlp2, final_g):
    batch, seq, _ = x.shape
    geom = _Geom(batch, ctx.shape[1], seq)
    depth = w_in.shape[0]
    assert batch + 1 <= 8
    cond = jnp.zeros((8, D_MODEL), F32).at[:batch].set(c).at[batch].set(c_ctx)
    xall = jnp.concatenate([ctx, x], axis=1).reshape(geom.m, D_MODEL)
    modtabs = []
    for l in range(depth):
        mod = _ada_mod(cond, ada_w, l, ada_b[l])
        mod_x = mod[:batch].reshape(batch, 1, 6, D_MODEL)
        mod_c = jnp.broadcast_to(mod[batch].reshape(1, 1, 6, D_MODEL), (batch, 1, 6, D_MODEL))
        modtabs.append(jnp.concatenate([mod_c, mod_x], axis=1).reshape(2 * batch, 6, D_MODEL))
    h = _norm_mod(geom, xall, norm1_g[0], modtabs[0], rows=(0, 1))
    for l in range(depth):
        modtab = modtabs[l]
        lp = {
            "rwkv_mu": rwkv_mu[l], "rwkv_w0": rwkv_w0[l], "rwkv_w_up": rwkv_w_up[l],
            "rwkv_a0": rwkv_a0[l], "rwkv_a_up": rwkv_a_up[l], "rwkv_k_k": rwkv_k_k[l],
            "rwkv_k_a": rwkv_k_a[l], "rwkv_r_k": rwkv_r_k[l], "rwkv_g_up": rwkv_g_up[l],
            "rwkv_lnx_g": rwkv_lnx_g[l], "rwkv_lnx_b": rwkv_lnx_b[l], "att_sink": att_sink[l],
            "conv_dw": conv_dw[l], "conv_dw_b": conv_dw_b[l], "conv_ln_g": conv_ln_g[l],
            "conv_ln_b": conv_ln_b[l], "w_branch": w_branch[l], "w_out": w_out[l],
        }
        xall, h2 = _mixer(geom, h, xall, modtab, lp, w_in, l, norm2_g[l])
        hid = _matmul_w32(h2, w_mlp1, l, BF16, epilogue="relu2", name="mlp_up")
        down = functools.partial(_matmul, hid, w_mlp2[l].astype(BF16), F32, epilogue="resid", geom=geom,
                                 res=xall, modtab=modtab, gate_row=5, name="mlp_down")
        if l + 1 < depth:
            xall, h = down(norm=(norm1_g[l + 1], modtabs[l + 1], (0, 1)), tm_cap=512)
        else:
            xall = down()
    return _final_norm(geom, xall, final_g).reshape(batch, seq, D_MODEL)
```

```python
import functools
import math

import numpy as np
import jax
import jax.numpy as jnp
from jax import lax
from jax.experimental import pallas as pl
from jax.experimental.pallas import tpu as pltpu

F32 = jnp.float32
BF16 = jnp.bfloat16

D_MODEL = 2048
GRID_W = 64
NORM_EPS = 1e-6
N_BRANCH = 4
BRANCH_W = D_MODEL // N_BRANCH
FNO_GROUPS = 4
FNO_GROUP_W = BRANCH_W // FNO_GROUPS
RWKV_HEAD = 64
RWKV_HEADS = BRANCH_W // RWKV_HEAD
N_DIR = 2
DECAY_LORA = 64
AICL_LORA = 64
GATE_LORA = 128
DIR_LORA_W = DECAY_LORA + AICL_LORA
SHIFT_W = 3 * BRANCH_W + DIR_LORA_W
GN_EPS = 64e-5
ATT_HEAD = 64
ATT_HEADS = BRANCH_W // ATT_HEAD
ATT_KV_HEADS = 2
ATT_REP = ATT_HEADS // ATT_KV_HEADS
ATT_KV_W = ATT_KV_HEADS * ATT_HEAD
WINDOW = 128
BLOCK = 128
ROPE_BASE = 10000.0
NEG_INF = -1e30
CONV_K = 31
CONV_PAD = (CONV_K - 1) // 2
LN_EPS = 1e-5
MLP_HIDDEN = 4 * D_MODEL

O_LORA = 3 * BRANCH_W
O_KV = O_LORA + N_DIR * DIR_LORA_W
CTX_STATE_COLS = O_KV + 2 * ATT_KV_W
O_G = CTX_STATE_COLS
O_Q = O_G + GATE_LORA
O_FNO = O_Q + BRANCH_W
O_CONV = O_FNO + BRANCH_W
O_GATE = O_CONV + 2 * BRANCH_W
IN_W = O_GATE + N_BRANCH * D_MODEL

LANES = 128
ROW_TILE = 256
SCAN_CHUNK = 64
PAIR_W = 2 * RWKV_HEAD
N_PAIRS = BRANCH_W // PAIR_W
HALO = 16
ATT_SLAB = 64
SUBLANES = 8
MXU_COLS = 256
CAST_ROWS = 256
CONV_ROWS = 32
VMEM_LIMIT = 56 * 1024 * 1024

NT_DIMS = (((1,), (1,)), ((), ()))
NN_DIMS = (((1,), (0,)), ((), ()))
TN_DIMS = (((0,), (0,)), ((), ()))


def _params(sem):
    return pltpu.CompilerParams(dimension_semantics=sem, vmem_limit_bytes=VMEM_LIMIT)


def _split2(a):
    hi = a.astype(BF16)
    lo = (a - hi.astype(F32)).astype(BF16)
    return hi, lo


def _dot3(a, b, dims=NN_DIMS):
    ah, al = _split2(a)
    bh, bl = _split2(b)
    dg = functools.partial(lax.dot_general, dimension_numbers=dims, preferred_element_type=F32)
    return dg(ah, bh) + (dg(ah, bl) + dg(al, bh))


def _dot_sel(a, sel_bf16):
    hi, lo = _split2(a)
    n = a.shape[0]
    both = jnp.dot(jnp.concatenate([hi, lo], axis=0), sel_bf16, preferred_element_type=F32)
    return both[:n] + both[n:]


def _sigmoid(x):
    return 1.0 / (1.0 + jnp.exp(-x))


def _ada_kernel(a_ref, w_ref, b_ref, o_ref):
    a = a_ref[...]
    s = a * _sigmoid(a)
    o_ref[...] = jnp.dot(s, w_ref[0], preferred_element_type=F32,
                         precision=lax.Precision.HIGHEST) + b_ref[...]


def _ada_mod(cond, w, layer, b):
    n = w.shape[2]
    tn = 1024
    return pl.pallas_call(
        _ada_kernel,
        grid=(n // tn,),
        in_specs=[pl.BlockSpec((8, D_MODEL), lambda j: (0, 0)),
                  pl.BlockSpec((1, D_MODEL, tn), lambda j: (layer, 0, j)),
                  pl.BlockSpec((1, tn), lambda j: (0, j))],
        out_specs=pl.BlockSpec((8, tn), lambda j: (0, j)),
        out_shape=jax.ShapeDtypeStruct((8, n), F32),
        compiler_params=_params(("parallel",)),
        name="ada_mod",
    )(cond, w, b.reshape(1, n))


class _Geom:
    def __init__(self, batch, ctx_len, seq):
        assert ctx_len % ROW_TILE == 0 and seq % ROW_TILE == 0
        assert seq % GRID_W == 0 and seq % BLOCK == 0 and ctx_len % BLOCK == 0
        self.batch = batch
        self.ctx_len = ctx_len
        self.seq = seq
        self.lt = ctx_len + seq
        self.m = batch * self.lt
        self.tiles_b = self.lt // ROW_TILE
        self.ctx_tiles = ctx_len // ROW_TILE
        self.tiles = batch * self.tiles_b

    def mod_row(self, i):
        return 2 * (i // self.tiles_b) + ((i % self.tiles_b) >= self.ctx_tiles).astype(jnp.int32)


def _norm_kernel(x_ref, g_ref, *rest, rows):
    x = x_ref[...]
    y = x * lax.rsqrt(jnp.mean(x * x, axis=-1, keepdims=True) + NORM_EPS) * g_ref[...]
    if rows is None:
        (o_ref,) = rest
    else:
        mod_ref, o_ref = rest
        mod = mod_ref[0]
        y = y * (1.0 + mod[rows[1]:rows[1] + 1]) + mod[rows[0]:rows[0] + 1]
    o_ref[...] = y.astype(o_ref.dtype)


def _norm_mod(geom, x, g, modtab, rows):
    return pl.pallas_call(
        functools.partial(_norm_kernel, rows=rows),
        grid=(geom.tiles,),
        in_specs=[pl.BlockSpec((ROW_TILE, D_MODEL), lambda i: (i, 0)),
                  pl.BlockSpec((1, D_MODEL), lambda i: (0, 0)),
                  pl.BlockSpec((1, 6, D_MODEL), lambda i: (geom.mod_row(i), 0, 0))],
        out_specs=pl.BlockSpec((ROW_TILE, D_MODEL), lambda i: (i, 0)),
        out_shape=jax.ShapeDtypeStruct((geom.m, D_MODEL), BF16),
        compiler_params=_params(("parallel",)),
        name="norm_mod",
    )(x, g.reshape(1, D_MODEL), modtab)


def _final_norm(geom, x, g):
    per_b = geom.seq // ROW_TILE

    def in_map(i):
        return ((i // per_b) * geom.tiles_b + geom.ctx_tiles + i % per_b, 0)

    return pl.pallas_call(
        functools.partial(_norm_kernel, rows=None),
        grid=(geom.batch * per_b,),
        in_specs=[pl.BlockSpec((ROW_TILE, D_MODEL), in_map),
                  pl.BlockSpec((1, D_MODEL), lambda i: (0, 0))],
        out_specs=pl.BlockSpec((ROW_TILE, D_MODEL), lambda i: (i, 0)),
        out_shape=jax.ShapeDtypeStruct((geom.batch * geom.seq, D_MODEL), F32),
        compiler_params=_params(("parallel",)),
        name="final_norm",
    )(x, g.reshape(1, D_MODEL))


def _mm_store(acc, res_ref, mod_refs, o_ref, epilogue, gate_row, norm=None, cs=slice(None)):
    if epilogue == "resid":
        for s, mod_ref in enumerate(mod_refs):
            rs = slice(s * ROW_TILE, (s + 1) * ROW_TILE)
            xn = res_ref[rs, cs] + mod_ref[0, gate_row:gate_row + 1, cs] * acc[rs]
            o_ref[rs, cs] = xn
            if norm is not None:
                g_ref, nmod_refs, h_ref, rows = norm
                nmod = nmod_refs[s][0]
                y = xn * lax.rsqrt(jnp.mean(xn * xn, axis=-1, keepdims=True) + NORM_EPS) * g_ref[...]
                h_ref[rs] = (y * (1.0 + nmod[rows[1]:rows[1] + 1]) + nmod[rows[0]:rows[0] + 1]).astype(h_ref.dtype)
        return
    if epilogue == "sigmoid":
        acc = _sigmoid(acc)
    elif epilogue == "relu2":
        acc = jnp.square(jnp.maximum(acc, 0.0))
    o_ref[:, cs] = acc.astype(o_ref.dtype)


def _col_chunks(tn, norm):
    if norm is not None or tn % MXU_COLS:
        return [slice(None)]
    return [slice(c, c + MXU_COLS) for c in range(0, tn, MXU_COLS)]


def _mm_kernel(a_ref, b_ref, *rest, epilogue, gate_row, nk, n_sub, norm_rows):
    rest = list(rest)
    res_ref, mod_refs, norm = None, (), None
    if epilogue == "resid":
        res_ref, mod_refs, rest = rest[0], rest[1:1 + n_sub], rest[1 + n_sub:]
        if norm_rows is not None:
            norm = (rest[0], rest[1:1 + n_sub], rest[2 + n_sub], norm_rows)
            rest = [rest[1 + n_sub]] + rest[3 + n_sub:]
    o_ref = rest[0]
    chunks = _col_chunks(o_ref.shape[1], norm)
    if nk == 1:
        for cs in chunks:
            acc = jnp.dot(a_ref[...], b_ref[:, cs], preferred_element_type=F32)
            _mm_store(acc, res_ref, mod_refs, o_ref, epilogue, gate_row, norm, cs)
        return
    acc_ref = rest[1]
    k = pl.program_id(2)

    @pl.when(k == 0)
    def _():
        acc_ref[...] = jnp.dot(a_ref[...], b_ref[...], preferred_element_type=F32)

    @pl.when((k > 0) & (k < nk - 1))
    def _():
        acc_ref[...] += jnp.dot(a_ref[...], b_ref[...], preferred_element_type=F32)

    @pl.when(k == nk - 1)
    def _():
        for cs in chunks:
            acc = acc_ref[:, cs] + jnp.dot(a_ref[...], b_ref[:, cs], preferred_element_type=F32)
            _mm_store(acc, res_ref, mod_refs, o_ref, epilogue, gate_row, norm, cs)


def _mm_w32_kernel(a_ref, w_ref, *rest, epilogue, gate_row):
    res_ref, mod_refs = (rest[0], rest[1:-2]) if epilogue == "resid" else (None, ())
    o_ref, wb_ref = rest[-2:]

    @pl.when(pl.program_id(1) == 0)
    def _():
        def cast_rows(r, carry):
            rows = pl.ds(pl.multiple_of(r * CAST_ROWS, CAST_ROWS), CAST_ROWS)
            wb_ref[rows, :] = w_ref[0, rows, :].astype(BF16)
            return carry

        lax.fori_loop(0, wb_ref.shape[0] // CAST_ROWS, cast_rows, 0)

    for cs in _col_chunks(o_ref.shape[1], None):
        acc = jnp.dot(a_ref[...], wb_ref[:, cs], preferred_element_type=F32)
        _mm_store(acc, res_ref, mod_refs, o_ref, epilogue, gate_row, cs=cs)


def _pick_tile(n, cap, unit=LANES):
    t = (min(cap, n) // unit) * unit
    while n % t:
        t -= unit
    return t


def _resid_operands(geom, res, modtab, tm, tn, row_col):
    n_sub = tm // ROW_TILE
    specs = [pl.BlockSpec((tm, tn), lambda *g: row_col(*g))]
    args = [res]
    for s in range(n_sub):
        specs.append(pl.BlockSpec(
            (1, 6, tn), lambda *g, s=s: (geom.mod_row(row_col(*g)[0] * n_sub + s), 0, row_col(*g)[1])))
        args.append(modtab)
    return specs, args


def _matmul(a, b, out_dtype, epilogue="none", geom=None, res=None, modtab=None, gate_row=0, norm=None,
            tm_cap=1024, tn_cap=1024, tk_cap=2048, name="matmul"):
    m, kdim = a.shape
    n = b.shape[1]
    tm = _pick_tile(m, tm_cap, ROW_TILE)
    tn = n if norm is not None else _pick_tile(n, tn_cap)
    tk = _pick_tile(kdim, tk_cap)
    nk = kdim // tk
    n_sub = tm // ROW_TILE
    in_specs = [pl.BlockSpec((tm, tk), lambda i, j, k: (i, k)),
                pl.BlockSpec((tk, tn), lambda i, j, k: (k, j))]
    args = [a, b]
    out_specs = pl.BlockSpec((tm, tn), lambda i, j, k: (i, j))
    out_shape = jax.ShapeDtypeStruct((m, n), out_dtype)
    if epilogue == "resid":
        specs, extra = _resid_operands(geom, res, modtab, tm, tn, lambda i, j, k: (i, j))
        in_specs += specs
        args += extra
        if norm is not None:
            gain, nmodtab, _ = norm
            specs, extra = _resid_operands(geom, gain.reshape(1, n), nmodtab, tm, tn, lambda i, j, k: (i, j))
            in_specs += [pl.BlockSpec((1, tn), lambda i, j, k: (0, 0))] + specs[1:]
            args += extra
            out_specs = [out_specs, pl.BlockSpec((tm, tn), lambda i, j, k: (i, j))]
            out_shape = [out_shape, jax.ShapeDtypeStruct((m, n), BF16)]
    return pl.pallas_call(
        functools.partial(_mm_kernel, epilogue=epilogue, gate_row=gate_row, nk=nk, n_sub=n_sub,
                          norm_rows=None if norm is None else norm[2]),
        grid=(m // tm, n // tn, nk),
        in_specs=in_specs,
        out_specs=out_specs,
        out_shape=out_shape,
        scratch_shapes=[pltpu.VMEM((tm, tn), F32)] if nk > 1 else [],
        compiler_params=_params(("parallel", "parallel", "arbitrary")),
        name=name,
    )(*args)


def _matmul_w32(a, w, layer, out_dtype, epilogue="none", geom=None, res=None, modtab=None, gate_row=0,
                cols=None, tm_cap=1024, tn_cap=1024, name="matmul_w32"):
    m, kdim = a.shape
    col0, n = cols if cols is not None else (0, w.shape[2])
    assert col0 % LANES == 0
    tm = _pick_tile(m, tm_cap, ROW_TILE)
    tn = _pick_tile(n, tn_cap)
    in_specs = [pl.BlockSpec((tm, kdim), lambda j, i: (i, 0)),
                pl.BlockSpec((pl.Element(1), pl.Element(kdim), pl.Element(tn)),
                             lambda j, i: (layer, 0, pl.multiple_of(col0 + j * tn, LANES)))]
    args = [a, w]
    if epilogue == "resid":
        specs, extra = _resid_operands(geom, res, modtab, tm, tn, lambda j, i: (i, j))
        in_specs += specs
        args += extra
    return pl.pallas_call(
        functools.partial(_mm_w32_kernel, epilogue=epilogue, gate_row=gate_row),
        grid=(n // tn, m // tm),
        in_specs=in_specs,
        out_specs=pl.BlockSpec((tm, tn), lambda j, i: (i, j)),
        out_shape=jax.ShapeDtypeStruct((m, n), out_dtype),
        scratch_shapes=[pltpu.VMEM((kdim, tn), BF16)],
        compiler_params=_params(("parallel", "arbitrary")),
        name=name,
    )(*args)


def _head_sum_matrix(width, head):
    idx = np.arange(width) // head
    return jnp.asarray((idx[:, None] == idx[None, :]).astype(np.float32), dtype=BF16)


def _prep_kernel(p_ref, lora_ref, halo_ref, hlora_ref, mu_ref, w0_ref, wup_ref, a0_ref, aup_ref,
                 kk_ref, ka_ref, hs_ref, r_o, lw_o, k_o, v_o, kkn_o, b_o, *, tiles_b, ctx_tiles):
    d = pl.program_id(0)
    j = pl.program_id(1) % tiles_b
    fwd = d == 0
    f = jnp.concatenate([p_ref[...], lora_ref[...]], axis=-1)
    t = f.shape[0]
    halo = jnp.concatenate([halo_ref[...], hlora_ref[...]], axis=-1)
    at_start = (j == 0) | (j == ctx_tiles)
    at_end = (j == ctx_tiles - 1) | (j == tiles_b - 1)
    edge = jnp.where(fwd, halo[7:8], halo[0:1])
    edge = jnp.where((fwd & at_start) | (jnp.logical_not(fwd) & at_end), 0.0, edge)
    row = lax.broadcasted_iota(jnp.int32, (t, 1), 0)
    prev = jnp.where(row == 0, edge, pltpu.roll(f, 1, 0))
    nxt = jnp.where(row == t - 1, edge, pltpu.roll(f, t - 1, 0))
    f = f + mu_ref[0] * (jnp.where(fwd, prev, nxt) - f)

    r = f[:, 0:BRANCH_W]
    k = f[:, BRANCH_W:2 * BRANCH_W]
    v = f[:, 2 * BRANCH_W:3 * BRANCH_W]
    wl = f[:, 3 * BRANCH_W:3 * BRANCH_W + DECAY_LORA]
    al = f[:, 3 * BRANCH_W + DECAY_LORA:SHIFT_W]
    w_raw = w0_ref[0] + _dot3(jnp.tanh(wl), wup_ref[0])
    lw = -math.exp(-0.5) * _sigmoid(w_raw)
    a = _sigmoid(a0_ref[0] + _dot3(al, aup_ref[0]))
    kk = k * kk_ref[...]
    norm = jnp.sqrt(_dot_sel(kk * kk, hs_ref[...]))
    kk = kk / jnp.maximum(norm, 1e-12)
    r_o[0] = r
    lw_o[0] = lw
    k_o[0] = k * (1.0 + (a - 1.0) * ka_ref[...])
    v_o[0] = v
    kkn_o[0] = kk
    b_o[0] = kk * a


def _rwkv_prep(geom, p1, lp):
    m = geom.m
    t = ROW_TILE
    tb = geom.tiles_b
    rkv_w = 3 * BRANCH_W
    lora_blk0 = O_LORA // DIR_LORA_W
    n_blk8 = m // 8

    def halo_idx(d, i):
        before = jnp.maximum(i * (t // 8) - 1, 0)
        after = jnp.minimum((i + 1) * (t // 8), n_blk8 - 1)
        return jnp.where(d == 0, before, after)

    out = jax.ShapeDtypeStruct((N_DIR, m, BRANCH_W), F32)
    ospec = pl.BlockSpec((1, t, BRANCH_W), lambda d, i: (d, i, 0))
    vec = lambda a: a.reshape(1, BRANCH_W)
    dvec = pl.BlockSpec((1, 1, BRANCH_W), lambda d, i: (d, 0, 0))
    return pl.pallas_call(
        functools.partial(_prep_kernel, tiles_b=tb, ctx_tiles=geom.ctx_tiles),
        grid=(N_DIR, geom.tiles),
        in_specs=[pl.BlockSpec((t, rkv_w), lambda d, i: (i, 0)),
                  pl.BlockSpec((t, DIR_LORA_W), lambda d, i: (i, lora_blk0 + d)),
                  pl.BlockSpec((8, rkv_w), lambda d, i: (halo_idx(d, i), 0)),
                  pl.BlockSpec((8, DIR_LORA_W), lambda d, i: (halo_idx(d, i), lora_blk0 + d)),
                  pl.BlockSpec((1, 1, SHIFT_W), lambda d, i: (d, 0, 0)),
                  dvec,
                  pl.BlockSpec((1, DECAY_LORA, BRANCH_W), lambda d, i: (d, 0, 0)),
                  dvec,
                  pl.BlockSpec((1, AICL_LORA, BRANCH_W), lambda d, i: (d, 0, 0)),
                  pl.BlockSpec((1, BRANCH_W), lambda d, i: (0, 0)),
                  pl.BlockSpec((1, BRANCH_W), lambda d, i: (0, 0)),
                  pl.BlockSpec((BRANCH_W, BRANCH_W), lambda d, i: (0, 0))],
        out_specs=[ospec] * 6,
        out_shape=[out] * 6,
        compiler_params=_params(("parallel", "parallel")),
        name="rwkv_prep",
    )(p1, p1, p1, p1, lp["rwkv_mu"].reshape(N_DIR, 1, SHIFT_W),
      lp["rwkv_w0"].reshape(N_DIR, 1, BRANCH_W), lp["rwkv_w_up"],
      lp["rwkv_a0"].reshape(N_DIR, 1, BRANCH_W), lp["rwkv_a_up"],
      vec(lp["rwkv_k_k"]), vec(lp["rwkv_k_a"]), _head_sum_matrix(BRANCH_W, RWKV_HEAD))


def _scan_kernel(*refs):
    (r_f, r_b, lw_f, lw_b, k_f, k_b, v_f, v_b, kk_f, kk_b, b_f, b_b, yf_ref, yb_ref, h_ref) = refs
    c = pl.program_id(1)
    C = SCAN_CHUNK
    W = 2 * C

    @pl.when(c == 0)
    def _():
        h_ref[...] = jnp.zeros_like(h_ref)

    rr = lax.broadcasted_iota(jnp.int32, (C, C), 0)
    cc = lax.broadcasted_iota(jnp.int32, (C, C), 1)
    lane = lax.broadcasted_iota(jnp.int32, (1, PAIR_W), 1)
    m_a = (lane < RWKV_HEAD).astype(F32)
    m_b = 1.0 - m_a
    r2 = lax.broadcasted_iota(jnp.int32, (W, W), 0)
    c2 = lax.broadcasted_iota(jnp.int32, (W, W), 1)
    same = (r2 // C) == (c2 // C)
    eye = (r2 == c2).astype(F32)

    def pairs(x):
        return [jnp.concatenate([x[:, p * PAIR_W:(p + 1) * PAIR_W] * m_a,
                                 x[:, p * PAIR_W:(p + 1) * PAIR_W] * m_b], axis=0) for p in range(N_PAIRS)]

    stacks = {name: [] for name in ("a", "b", "k", "r", "v", "bc", "kc", "pt")}
    strict, incl = [], []
    for sgn, (r_ref, lw_ref, k_ref, v_ref, kk_ref, b_ref) in (
            (1, (r_f, lw_f, k_f, v_f, kk_f, b_f)), (-1, (r_b, lw_b, k_b, v_b, kk_b, b_b))):
        lw = lw_ref[0]
        tri = jnp.where((rr - cc) * sgn >= 0, 1.0, 0.0).astype(BF16)
        lp_in = _dot_sel_lhs(tri, lw)
        tot = jnp.sum(lw, axis=0, keepdims=True)
        e_neg = jnp.exp(-lp_in)
        e_chk = jnp.exp(tot - lp_in)
        p_tot = jnp.exp(tot)
        stacks["a"] += pairs(-kk_ref[0] * jnp.exp(lp_in - lw))
        stacks["b"] += pairs(b_ref[0] * e_neg)
        stacks["k"] += pairs(k_ref[0] * e_neg)
        stacks["r"] += pairs(r_ref[0] * jnp.exp(lp_in))
        stacks["v"] += pairs(v_ref[0])
        stacks["bc"] += pairs(b_ref[0] * e_chk)
        stacks["kc"] += pairs(k_ref[0] * e_chk)
        stacks["pt"] += [p_tot[:, p * PAIR_W:(p + 1) * PAIR_W] for p in range(N_PAIRS)]
        dt = (r2 % C - c2 % C) * sgn
        strict += [(same & (dt > 0)).astype(F32)] * N_PAIRS
        incl += [(same & (dt >= 0)).astype(F32)] * N_PAIRS
    a_s, b_s, k_s, r_s, v_s, bc_s, kc_s, p_tot = (jnp.stack(stacks[n]) for n in
                                                  ("a", "b", "k", "r", "v", "bc", "kc", "pt"))
    strict = jnp.stack(strict)
    incl = jnp.stack(incl)

    a_b, b_b, k_b, r_b, v_b, bc_b, kc_b = (x.astype(BF16) for x in (a_s, b_s, k_s, r_s, v_s, bc_s, kc_s))
    big = _bdot(jnp.concatenate([a_b, r_b], axis=1), jnp.concatenate([b_b, k_b], axis=1), BNT_DIMS)
    l_ab = big[:, :W, :W] * strict
    l_ak = big[:, :W, W:] * strict
    m_rb = (big[:, W:, :W] * incl).astype(BF16)
    m_rk = big[:, W:, W:] * incl
    t_inv = eye + l_ab
    pw_b = l_ab.astype(BF16)
    pw_b = _bdot(pw_b, pw_b).astype(BF16)
    for _ in range(int(math.log2(C)) - 2):
        both = _bdot(jnp.concatenate([t_inv.astype(BF16), pw_b], axis=1), pw_b)
        t_inv = t_inv + both[:, :W]
        pw_b = both[:, W:].astype(BF16)
    t_inv = t_inv + _bdot(t_inv, pw_b)
    x1 = _bdot(t_inv, jnp.concatenate([a_b, _bdot(l_ak, v_b).astype(BF16)], axis=2)).astype(BF16)
    x2 = _bdot(m_rb, x1)
    r_hat = r_s + x2[:, :, :PAIR_W]
    y0 = x2[:, :, PAIR_W:] + _bdot(m_rk, v_b)
    x3 = _bdot(bc_b, x1, BTN_DIMS)
    g = eye * p_tot + x3[:, :, :PAIR_W]
    h_inc = x3[:, :, PAIR_W:] + _bdot(kc_b, v_b, BTN_DIMS)
    x4 = _bdot(jnp.concatenate([r_hat, g], axis=1), h_ref[...])
    ys = x4[:, :W] + y0
    h_ref[...] = x4[:, W:] + h_inc
    for d, y_ref in enumerate((yf_ref, yb_ref)):
        for p in range(N_PAIRS):
            y_ref[:, p * PAIR_W:(p + 1) * PAIR_W] = ys[d * N_PAIRS + p, :C] + ys[d * N_PAIRS + p, C:]


BNN_DIMS = (((2,), (1,)), ((0,), (0,)))
BNT_DIMS = (((2,), (2,)), ((0,), (0,)))
BTN_DIMS = (((1,), (1,)), ((0,), (0,)))


def _bdot(a, b, dims=BNN_DIMS):
    return lax.dot_general(a.astype(BF16), b.astype(BF16), dims, preferred_element_type=F32)


def _dot_sel_lhs(sel_bf16, a):
    hi = a.astype(BF16)
    r1 = a - hi.astype(F32)
    mid = r1.astype(BF16)
    lo = (r1 - mid.astype(F32)).astype(BF16)
    dg = functools.partial(jnp.dot, preferred_element_type=F32)
    return dg(sel_bf16, hi) + (dg(sel_bf16, mid) + dg(sel_bf16, lo))


def _rwkv_scan(geom, ins):
    C = SCAN_CHUNK
    nch = geom.lt // C
    nctx = geom.ctx_len // C

    def rev(c):
        return jnp.where(c < nctx, nctx - 1 - c, nch - 1 + nctx - c)

    fwd = pl.BlockSpec((1, C, BRANCH_W), lambda b, c: (0, b * nch + c, 0))
    bwd = pl.BlockSpec((1, C, BRANCH_W), lambda b, c: (1, b * nch + rev(c), 0))
    out = jax.ShapeDtypeStruct((geom.m, BRANCH_W), F32)
    return pl.pallas_call(
        _scan_kernel,
        grid=(geom.batch, nch),
        in_specs=[fwd, bwd] * 6,
        out_specs=[pl.BlockSpec((C, BRANCH_W), lambda b, c: (b * nch + c, 0)),
                   pl.BlockSpec((C, BRANCH_W), lambda b, c: (b * nch + rev(c), 0))],
        out_shape=[out, out],
        scratch_shapes=[pltpu.VMEM((N_DIR * N_PAIRS, PAIR_W, PAIR_W), F32)],
        compiler_params=_params(("parallel", "arbitrary")),
        name="rwkv_scan",
    )(*[a for a in ins for _ in range(N_DIR)])


def _readout_kernel(yf_ref, yb_ref, r_ref, k_ref, v_ref, p2_ref, gup_ref, rk_ref, lg_ref, lb_ref, hs_ref, o_ref):
    hs = hs_ref[...]
    y = yf_ref[...] + yb_ref[...]
    inv_n = 1.0 / RWKV_HEAD
    mean = _dot_sel(y, hs) * inv_n
    yc = y - mean
    var = _dot_sel(yc * yc, hs) * inv_n
    yn = yc * lax.rsqrt(var + GN_EPS) * lg_ref[...] + lb_ref[...]
    bonus = jnp.zeros_like(y)
    for d in range(N_DIR):
        bonus = bonus + _dot_sel(r_ref[d] * k_ref[d] * rk_ref[d:d + 1], hs) * v_ref[d]
    g = _dot3(_sigmoid(p2_ref[:, 0:GATE_LORA]), gup_ref[...])
    o_ref[...] = ((yn + bonus) * g).astype(o_ref.dtype)


def _rwkv_readout(geom, y, ins, p2, lp):
    t = ROW_TILE
    dspec = pl.BlockSpec((N_DIR, t, BRANCH_W), lambda i: (0, i, 0))
    vspec = pl.BlockSpec((1, BRANCH_W), lambda i: (0, 0))
    r, _, k, v, _, _ = ins
    return pl.pallas_call(
        _readout_kernel,
        grid=(geom.tiles,),
        in_specs=[pl.BlockSpec((t, BRANCH_W), lambda i: (i, 0)),
                  pl.BlockSpec((t, BRANCH_W), lambda i: (i, 0)),
                  dspec, dspec, dspec,
                  pl.BlockSpec((t, GATE_LORA + BRANCH_W), lambda i: (i, 0)),
                  pl.BlockSpec((GATE_LORA, BRANCH_W), lambda i: (0, 0)),
                  pl.BlockSpec((N_DIR, BRANCH_W), lambda i: (0, 0)),
                  vspec, vspec,
                  pl.BlockSpec((BRANCH_W, BRANCH_W), lambda i: (0, 0))],
        out_specs=pl.BlockSpec((t, BRANCH_W), lambda i: (i, 0)),
        out_shape=jax.ShapeDtypeStruct((geom.m, BRANCH_W), BF16),
        compiler_params=_params(("parallel",)),
        name="rwkv_readout",
    )(y[0], y[1], r, k, v, p2, lp["rwkv_g_up"], lp["rwkv_r_k"].reshape(N_DIR, BRANCH_W),
      lp["rwkv_lnx_g"].reshape(1, BRANCH_W), lp["rwkv_lnx_b"].reshape(1, BRANCH_W),
      _head_sum_matrix(BRANCH_W, RWKV_HEAD))


def _rope_tables(geom):
    half = ATT_HEAD // 2
    nf = half // 2
    inv = ROPE_BASE ** (-jnp.arange(nf, dtype=F32) / nf)
    pos = jnp.arange(geom.seq, dtype=jnp.int32)
    row_ang = (pos // GRID_W).astype(F32)[:, None] * inv[None, :]
    col_ang = (pos % GRID_W).astype(F32)[:, None] * inv[None, :]
    cos = jnp.concatenate([jnp.cos(row_ang)] * 2 + [jnp.cos(col_ang)] * 2, axis=-1)
    sin = jnp.concatenate([-jnp.sin(row_ang), jnp.sin(row_ang), -jnp.sin(col_ang), jnp.sin(col_ang)], axis=-1)
    cos = jnp.concatenate([jnp.ones((geom.ctx_len, ATT_HEAD), F32), cos], axis=0)
    sin = jnp.concatenate([jnp.zeros((geom.ctx_len, ATT_HEAD), F32), sin], axis=0)
    return cos, sin


def _rotate(t, cos, sin):
    w = t.shape[-1]
    nf = ATT_HEAD // 4
    lane = lax.broadcasted_iota(jnp.int32, (1, w), 1)
    partner = jnp.where((lane % (2 * nf)) < nf, pltpu.roll(t, w - nf, 1), pltpu.roll(t, nf, 1))
    return t * cos + partner * sin


def _rope_kernel(p2_ref, kv_ref, cq_ref, sq_ref, ck_ref, sk_ref, q_o, k_o, v_o):
    q = p2_ref[:, GATE_LORA:]
    q_o[...] = (_rotate(q, cq_ref[...], sq_ref[...]) * (ATT_HEAD ** -0.5)).astype(q_o.dtype)
    kv = kv_ref[...]
    k_o[...] = _rotate(kv[:, :ATT_KV_W], ck_ref[...], sk_ref[...]).astype(k_o.dtype)
    v_o[...] = kv[:, ATT_KV_W:].astype(v_o.dtype)


def _rope(geom, p1, p2):
    t = ROW_TILE
    tb = geom.tiles_b
    cos, sin = _rope_tables(geom)
    cq, sq = jnp.tile(cos, (1, ATT_HEADS)), jnp.tile(sin, (1, ATT_HEADS))
    ck, sk = jnp.tile(cos, (1, ATT_KV_HEADS)), jnp.tile(sin, (1, ATT_KV_HEADS))
    qspec = pl.BlockSpec((t, BRANCH_W), lambda i: (i % tb, 0))
    kspec = pl.BlockSpec((t, ATT_KV_W), lambda i: (i % tb, 0))
    m = geom.m
    return pl.pallas_call(
        _rope_kernel,
        grid=(geom.tiles,),
        in_specs=[pl.BlockSpec((t, GATE_LORA + BRANCH_W), lambda i: (i, 0)),
                  pl.BlockSpec((t, 2 * ATT_KV_W), lambda i: (i, O_KV // (2 * ATT_KV_W))),
                  qspec, qspec, kspec, kspec],
        out_specs=[pl.BlockSpec((t, BRANCH_W), lambda i: (i, 0)),
                   pl.BlockSpec((t, ATT_KV_W), lambda i: (i, 0)),
                   pl.BlockSpec((t, ATT_KV_W), lambda i: (i, 0))],
        out_shape=[jax.ShapeDtypeStruct((m, BRANCH_W), BF16),
                   jax.ShapeDtypeStruct((m, ATT_KV_W), BF16),
                   jax.ShapeDtypeStruct((m, ATT_KV_W), BF16)],
        compiler_params=_params(("parallel",)),
        name="rope",
    )(p2, p1, cq, sq, ck, sk)


def _attn_kernel(q_ref, kc_ref, vc_ref, kp_ref, ko_ref, kn_ref, vp_ref, vo_ref, vn_ref, sink_ref, o_ref,
                 s_ref, p_ref, *, ctx_blocks, blocks_b):
    j = pl.program_id(1)
    q = q_ref[...]
    k_all = jnp.concatenate([kp_ref[...], ko_ref[...], kn_ref[...], kc_ref[...]], axis=0)
    v_all = jnp.concatenate([vp_ref[...], vo_ref[...], vn_ref[...], vc_ref[...]], axis=0)
    nloc = 3 * BLOCK
    slab = ATT_SLAB
    qi0 = lax.broadcasted_iota(jnp.int32, (slab, nloc), 0)
    ki = lax.broadcasted_iota(jnp.int32, (slab, nloc), 1)
    never = 4 * BLOCK
    prev_off = jnp.where(j > ctx_blocks, 0, never)
    own_hi = jnp.where(j >= ctx_blocks, 2 * BLOCK, BLOCK)
    next_off = 2 * BLOCK - jnp.where((j >= ctx_blocks) & (j < blocks_b - 1), 0, never)
    outs = []
    for g in range(ATT_KV_HEADS):
        gs = slice(g * ATT_HEAD, (g + 1) * ATT_HEAD)
        qg = jnp.concatenate([q[:, (g * ATT_REP + h) * ATT_HEAD:(g * ATT_REP + h + 1) * ATT_HEAD]
                              for h in range(ATT_REP)], axis=0)
        s_ref[g] = lax.dot_general(qg, k_all[:, gs], NT_DIMS, preferred_element_type=F32)
        dens = []
        for blk in range(ATT_REP * BLOCK // slab):
            rs = slice(blk * slab, (blk + 1) * slab)
            head = g * ATT_REP + blk * slab // BLOCK
            qi = qi0 + (blk * slab) % BLOCK
            valid = (((ki < BLOCK) & (ki >= qi + prev_off)) | ((ki >= BLOCK) & (ki < own_hi))
                     | ((ki >= 2 * BLOCK) & (ki <= qi + next_off)))
            s_loc = jnp.where(valid, s_ref[g, rs, :nloc], NEG_INF)
            s_ctx = s_ref[g, rs, nloc:]
            sink = sink_ref[head:head + 1, 0:1]
            mx = jnp.maximum(jnp.maximum(jnp.max(s_loc, axis=-1, keepdims=True),
                                         jnp.max(s_ctx, axis=-1, keepdims=True)), sink)
            e_loc = jnp.exp(s_loc - mx)
            e_ctx = jnp.exp(s_ctx - mx)
            dens.append(jnp.sum(e_loc, axis=-1, keepdims=True) + jnp.sum(e_ctx, axis=-1, keepdims=True)
                        + jnp.exp(sink - mx))
            p_ref[g, rs, :nloc] = e_loc.astype(BF16)
            p_ref[g, rs, nloc:] = e_ctx.astype(BF16)
        o = jnp.dot(p_ref[g], v_all[:, gs], preferred_element_type=F32) / jnp.concatenate(dens, axis=0)
        outs += [o[h * BLOCK:(h + 1) * BLOCK] for h in range(ATT_REP)]
    o_ref[...] = jnp.concatenate(outs, axis=-1).astype(o_ref.dtype)


def _attention(geom, q, k, v, sink):
    nb = geom.lt // BLOCK
    cb = geom.ctx_len // BLOCK
    row = lambda b, j: (b * nb + j, 0)
    prev = lambda b, j: (b * nb + jnp.maximum(j - 1, 0), 0)
    nxt = lambda b, j: (b * nb + jnp.minimum(j + 1, nb - 1), 0)
    ctx = lambda b, j: (b * (geom.lt // geom.ctx_len), 0)
    assert geom.lt % geom.ctx_len == 0
    kvs = lambda f: pl.BlockSpec((BLOCK, ATT_KV_W), f)
    cspec = pl.BlockSpec((geom.ctx_len, ATT_KV_W), ctx)
    sink_tab = jnp.broadcast_to(sink.astype(F32)[:, None], (ATT_HEADS, LANES))
    return pl.pallas_call(
        functools.partial(_attn_kernel, ctx_blocks=cb, blocks_b=nb),
        grid=(geom.batch, nb),
        in_specs=[pl.BlockSpec((BLOCK, BRANCH_W), row), cspec, cspec,
                  kvs(prev), kvs(row), kvs(nxt), kvs(prev), kvs(row), kvs(nxt),
                  pl.BlockSpec((ATT_HEADS, LANES), lambda b, j: (0, 0))],
        out_specs=pl.BlockSpec((BLOCK, BRANCH_W), row),
        out_shape=jax.ShapeDtypeStruct((geom.m, BRANCH_W), BF16),
        scratch_shapes=[pltpu.VMEM((ATT_KV_HEADS, ATT_REP * BLOCK, 3 * BLOCK + geom.ctx_len), F32),
                        pltpu.VMEM((ATT_KV_HEADS, ATT_REP * BLOCK, 3 * BLOCK + geom.ctx_len), BF16)],
        compiler_params=_params(("parallel", "parallel")),
        name="attention",
    )(q, k, v, k, k, k, v, v, v, sink_tab)


def _conv_kernel(u_ref, up_ref, un_ref, dw_ref, db_ref, lg_ref, lb_ref, o_ref, hp_ref, sh_ref,
                 *, tiles_b, ctx_tiles):
    j = pl.program_id(0) % tiles_b
    t = u_ref.shape[0]

    def glu(u):
        return u[:, :BRANCH_W] * _sigmoid(u[:, BRANCH_W:])

    at_start = (j == 0) | (j == ctx_tiles)
    at_end = (j == ctx_tiles - 1) | (j == tiles_b - 1)
    hp_ref[0:HALO] = jnp.where(at_start, 0.0, glu(up_ref[...]))
    hp_ref[HALO:HALO + t] = glu(u_ref[...])
    hp_ref[HALO + t:] = jnp.where(at_end, 0.0, glu(un_ref[...]))
    span = t + 2 * HALO - SUBLANES
    for r in range(1, SUBLANES):
        sh_ref[r, 0:span] = hp_ref[pl.ds(r, span)]
    for chunk in range(t // CONV_ROWS):
        acc = jnp.zeros((CONV_ROWS, BRANCH_W), F32) + db_ref[...]
        for tap in range(CONV_K):
            q, r = divmod(HALO - CONV_PAD + tap, SUBLANES)
            rows = pl.ds(chunk * CONV_ROWS + q * SUBLANES, CONV_ROWS)
            src = hp_ref[rows] if r == 0 else sh_ref[r, rows]
            acc = acc + src * dw_ref[tap:tap + 1]
        mean = jnp.mean(acc, axis=-1, keepdims=True)
        cen = acc - mean
        var = jnp.mean(cen * cen, axis=-1, keepdims=True)
        h = cen * lax.rsqrt(var + LN_EPS) * lg_ref[...] + lb_ref[...]
        o_ref[chunk * CONV_ROWS:(chunk + 1) * CONV_ROWS] = (h * _sigmoid(h)).astype(o_ref.dtype)


def _conv(geom, p4, lp):
    t = ROW_TILE
    nh = geom.m // HALO
    vspec = pl.BlockSpec((1, BRANCH_W), lambda i: (0, 0))
    return pl.pallas_call(
        functools.partial(_conv_kernel, tiles_b=geom.tiles_b, ctx_tiles=geom.ctx_tiles),
        grid=(geom.tiles,),
        in_specs=[pl.BlockSpec((t, 2 * BRANCH_W), lambda i: (i, 0)),
                  pl.BlockSpec((HALO, 2 * BRANCH_W), lambda i: (jnp.maximum(i * (t // HALO) - 1, 0), 0)),
                  pl.BlockSpec((HALO, 2 * BRANCH_W), lambda i: (jnp.minimum((i + 1) * (t // HALO), nh - 1), 0)),
                  pl.BlockSpec((CONV_K, BRANCH_W), lambda i: (0, 0)),
                  vspec, vspec, vspec],
        out_specs=pl.BlockSpec((t, BRANCH_W), lambda i: (i, 0)),
        out_shape=jax.ShapeDtypeStruct((geom.m, BRANCH_W), BF16),
        scratch_shapes=[pltpu.VMEM((t + 2 * HALO, BRANCH_W), F32),
                        pltpu.VMEM((SUBLANES, t + 2 * HALO, BRANCH_W), F32)],
        compiler_params=_params(("parallel",)),
        name="conformer_conv",
    )(p4, p4, p4, lp["conv_dw"], lp["conv_dw_b"].reshape(1, BRANCH_W),
      lp["conv_ln_g"].reshape(1, BRANCH_W), lp["conv_ln_b"].reshape(1, BRANCH_W))


def _dft_cos_sin(n, scale):
    idx = np.arange(n, dtype=np.int64)
    ang = 2.0 * np.pi * ((idx[:, None] * idx[None, :]) % n).astype(np.float64) / n
    return np.cos(ang) * scale, np.sin(ang) * scale


def _channel_dft():
    c, s = _dft_cos_sin(FNO_GROUP_W, FNO_GROUP_W ** -0.5)
    eye = np.eye(FNO_GROUPS)
    return jnp.asarray(np.concatenate([np.kron(eye, c), np.kron(eye, s)], axis=1), dtype=F32).astype(BF16)


def _dft_pos_kernel(c_ref, s_ref, gc_ref, gs_ref, o_ref):
    o_ref[0] = (jnp.dot(c_ref[...], gc_ref[0], preferred_element_type=F32)
                + jnp.dot(s_ref[...], gs_ref[0], preferred_element_type=F32)).astype(o_ref.dtype)


def _fourier(u):
    bsz, length, _ = u.shape
    gcs = _matmul(u.reshape(bsz * length, BRANCH_W), _channel_dft(), BF16, name="dft_channels")
    gcs = gcs.reshape(bsz, length, 2 * BRANCH_W)
    c, s = _dft_cos_sin(length, length ** -0.5)
    tm = _pick_tile(length, 512, ROW_TILE)
    return pl.pallas_call(
        _dft_pos_kernel,
        grid=(length // tm, bsz),
        in_specs=[pl.BlockSpec((tm, length), lambda i, b: (i, 0)),
                  pl.BlockSpec((tm, length), lambda i, b: (i, 0)),
                  pl.BlockSpec((1, length, BRANCH_W), lambda i, b: (b, 0, 0)),
                  pl.BlockSpec((1, length, BRANCH_W), lambda i, b: (b, 0, 1))],
        out_specs=pl.BlockSpec((1, tm, BRANCH_W), lambda i, b: (b, i, 0)),
        out_shape=jax.ShapeDtypeStruct((bsz, length, BRANCH_W), BF16),
        compiler_params=_params(("parallel", "parallel")),
        name="dft_positions",
    )(jnp.asarray(c, dtype=F32).astype(BF16), jnp.asarray(-s, dtype=F32).astype(BF16), gcs, gcs)


def _merge_kernel(f0, f1, f2, f3, w_ref, g0, g1, g2, g3, o_ref):
    acc = None
    for i, (f, g) in enumerate(((f0, g0), (f1, g1), (f2, g2), (f3, g3))):
        term = jnp.dot(f[...], w_ref[i], preferred_element_type=F32) * g[...].astype(F32)
        acc = term if acc is None else acc + term
    o_ref[...] = acc.astype(o_ref.dtype)


def _merge(geom, feats, w_branch, gate):
    m = geom.m
    tm, tn = _pick_tile(m, 512, ROW_TILE), 1024
    nblk = D_MODEL // tn
    fspec = pl.BlockSpec((tm, BRANCH_W), lambda i, j: (i, 0))
    gspec = lambda br: pl.BlockSpec((tm, tn), lambda i, j: (i, br * nblk + j))
    return pl.pallas_call(
        _merge_kernel,
        grid=(m // tm, nblk),
        in_specs=[fspec] * 4 + [pl.BlockSpec((N_BRANCH, BRANCH_W, tn), lambda i, j: (0, 0, j))]
        + [gspec(br) for br in range(N_BRANCH)],
        out_specs=pl.BlockSpec((tm, tn), lambda i, j: (i, j)),
        out_shape=jax.ShapeDtypeStruct((m, D_MODEL), BF16),
        compiler_params=_params(("parallel", "parallel")),
        name="branch_merge",
    )(*feats, w_branch, gate, gate, gate, gate)


def _mixer(geom, h, xall, modtab, lp, w_in, layer, norm2_g):
    proj = lambda lo, hi, dt, name, ep="none": _matmul_w32(h, w_in, layer, dt, epilogue=ep,
                                                           cols=(lo, hi - lo), name=name)
    p1 = proj(0, CTX_STATE_COLS, F32, "in_proj_state")
    p2 = proj(O_G, O_FNO, F32, "in_proj_gq")
    p3 = proj(O_FNO, O_CONV, BF16, "in_proj_fno")
    p4 = proj(O_CONV, O_GATE, F32, "in_proj_conv")
    gate = proj(O_GATE, IN_W, BF16, "in_proj_gate", "sigmoid")

    ins = _rwkv_prep(geom, p1, lp)
    y = _rwkv_scan(geom, ins)
    rw = _rwkv_readout(geom, y, ins, p2, lp)

    q, k, v = _rope(geom, p1, p2)
    att = _attention(geom, q, k, v, lp["att_sink"])

    cv = _conv(geom, p4, lp)

    p3 = p3.reshape(geom.batch, geom.lt, BRANCH_W)
    fno = jnp.concatenate([_fourier(p3[:, :geom.ctx_len]), _fourier(p3[:, geom.ctx_len:])], axis=1)
    fno = fno.reshape(geom.m, BRANCH_W)

    mixed = _merge(geom, (fno, rw, att, cv), lp["w_branch"].astype(BF16), gate)
    return _matmul(mixed, lp["w_out"].astype(BF16), F32, epilogue="resid", geom=geom, res=xall,
                   modtab=modtab, gate_row=2, norm=(norm2_g, modtab, (3, 4)), tm_cap=512, name="out_proj")


def kernel(x, c, ctx, c_ctx, ada_w, ada_b, norm1_g, norm2_g, w_in, rwkv_mu, rwkv_w0, rwkv_w_up, rwkv_a0, rwkv_a_up, rwkv_k_k, rwkv_k_a, rwkv_r_k, rwkv_g_up, rwkv_lnx_g, rwkv_lnx_b, att_sink, conv_dw, conv_dw_b, conv_ln_g, conv_ln_b, w_branch, w_out, w_mlp1, w_mlp2, final_g):
    batch, seq, _ = x.shape
    geom = _Geom(batch, ctx.shape[1], seq)
    depth = w_in.shape[0]
    assert batch + 1 <= 8
    cond = jnp.zeros((8, D_MODEL), F32).at[:batch].set(c).at[batch].set(c_ctx)
    xall = jnp.concatenate([ctx, x], axis=1).reshape(geom.m, D_MODEL)
    modtabs = []
    for l in range(depth):
        mod = _ada_mod(cond, ada_w, l, ada_b[l])
        mod_x = mod[:batch].reshape(batch, 1, 6, D_MODEL)
        mod_c = jnp.broadcast_to(mod[batch].reshape(1, 1, 6, D_MODEL), (batch, 1, 6, D_MODEL))
        modtabs.append(jnp.concatenate([mod_c, mod_x], axis=1).reshape(2 * batch, 6, D_MODEL))
    h = _norm_mod(geom, xall, norm1_g[0], modtabs[0], rows=(0, 1))
    for l in range(depth):
        modtab = modtabs[l]
        lp = {
            "rwkv_mu": rwkv_mu[l], "rwkv_w0": rwkv_w0[l], "rwkv_w_up": rwkv_w_up[l],
            "rwkv_a0": rwkv_a0[l], "rwkv_a_up": rwkv_a_up[l], "rwkv_k_k": rwkv_k_k[l],
            "rwkv_k_a": rwkv_k_a[l], "rwkv_r_k": rwkv_r_k[l], "rwkv_g_up": rwkv_g_up[l],
            "rwkv_lnx_g": rwkv_lnx_g[l], "rwkv_lnx_b": rwkv_lnx_b[l], "att_sink": att_sink[l],
            "conv_dw": conv_dw[l], "conv_dw_b": conv_dw_b[l], "conv_ln_g": conv_ln_g[l],
            "conv_ln_b": conv_ln_b[l], "w_branch": w_branch[l], "w_out": w_out[l],
        }
        xall, h2 = _mixer(geom, h, xall, modtab, lp, w_in, l, norm2_g[l])
        hid = _matmul_w32(h2, w_mlp1, l, BF16, epilogue="relu2", name="mlp_up")
        down = functools.partial(_matmul, hid, w_mlp2[l].astype(BF16), F32, epilogue="resid", geom=geom,
                                 res=xall, modtab=modtab, gate_row=5, name="mlp_down")
        if l + 1 < depth:
            xall, h = down(norm=(norm1_g[l + 1], modtabs[l + 1], (0, 1)), tm_cap=512)
        else:
            xall = down()
    return _final_norm(geom, xall, final_g).reshape(batch, seq, D_MODEL)
```

```python
import functools
import math

import numpy as np
import jax
import jax.numpy as jnp
from jax import lax
from jax.experimental import pallas as pl
from jax.experimental.pallas import tpu as pltpu

F32 = jnp.float32
BF16 = jnp.bfloat16

D_MODEL = 2048
GRID_W = 64
NORM_EPS = 1e-6
N_BRANCH = 4
BRANCH_W = D_MODEL // N_BRANCH
FNO_GROUPS = 4
FNO_GROUP_W = BRANCH_W // FNO_GROUPS
RWKV_HEAD = 64
RWKV_HEADS = BRANCH_W // RWKV_HEAD
N_DIR = 2
DECAY_LORA = 64
AICL_LORA = 64
GATE_LORA = 128
DIR_LORA_W = DECAY_LORA + AICL_LORA
SHIFT_W = 3 * BRANCH_W + DIR_LORA_W
GN_EPS = 64e-5
ATT_HEAD = 64
ATT_HEADS = BRANCH_W // ATT_HEAD
ATT_KV_HEADS = 2
ATT_REP = ATT_HEADS // ATT_KV_HEADS
ATT_KV_W = ATT_KV_HEADS * ATT_HEAD
WINDOW = 128
BLOCK = 128
ROPE_BASE = 10000.0
NEG_INF = -1e30
CONV_K = 31
CONV_PAD = (CONV_K - 1) // 2
LN_EPS = 1e-5
MLP_HIDDEN = 4 * D_MODEL

O_LORA = 3 * BRANCH_W
O_KV = O_LORA + N_DIR * DIR_LORA_W
CTX_STATE_COLS = O_KV + 2 * ATT_KV_W
O_G = CTX_STATE_COLS
O_Q = O_G + GATE_LORA
O_FNO = O_Q + BRANCH_W
O_CONV = O_FNO + BRANCH_W
O_GATE = O_CONV + 2 * BRANCH_W
IN_W = O_GATE + N_BRANCH * D_MODEL

LANES = 128
ROW_TILE = 256
SCAN_CHUNK = 64
PAIR_W = 2 * RWKV_HEAD
N_PAIRS = BRANCH_W // PAIR_W
HALO = 16
ATT_SLAB = 64
SUBLANES = 8
MXU_COLS = 256
CAST_ROWS = 256
CONV_ROWS = 32
VMEM_LIMIT = 56 * 1024 * 1024

NT_DIMS = (((1,), (1,)), ((), ()))
NN_DIMS = (((1,), (0,)), ((), ()))
TN_DIMS = (((0,), (0,)), ((), ()))


def _params(sem):
    return pltpu.CompilerParams(dimension_semantics=sem, vmem_limit_bytes=VMEM_LIMIT)


def _split2(a):
    hi = a.astype(BF16)
    lo = (a - hi.astype(F32)).astype(BF16)
    return hi, lo


def _dot3(a, b, dims=NN_DIMS):
    ah, al = _split2(a)
    bh, bl = _split2(b)
    dg = functools.partial(lax.dot_general, dimension_numbers=dims, preferred_element_type=F32)
    return dg(ah, bh) + (dg(ah, bl) + dg(al, bh))


def _dot_sel(a, sel_bf16):
    hi, lo = _split2(a)
    n = a.shape[0]
    both = jnp.dot(jnp.concatenate([hi, lo], axis=0), sel_bf16, preferred_element_type=F32)
    return both[:n] + both[n:]


def _sigmoid(x):
    return 1.0 / (1.0 + jnp.exp(-x))


def _ada_kernel(a_ref, w_ref, b_ref, o_ref):
    a = a_ref[...]
    s = a * _sigmoid(a)
    o_ref[...] = jnp.dot(s, w_ref[0], preferred_element_type=F32,
                         precision=lax.Precision.HIGHEST) + b_ref[...]


def _ada_mod(cond, w, layer, b):
    n = w.shape[2]
    tn = 1024
    return pl.pallas_call(
        _ada_kernel,
        grid=(n // tn,),
        in_specs=[pl.BlockSpec((8, D_MODEL), lambda j: (0, 0)),
                  pl.BlockSpec((1, D_MODEL, tn), lambda j: (layer, 0, j)),
                  pl.BlockSpec((1, tn), lambda j: (0, j))],
        out_specs=pl.BlockSpec((8, tn), lambda j: (0, j)),
        out_shape=jax.ShapeDtypeStruct((8, n), F32),
        compiler_params=_params(("parallel",)),
        name="ada_mod",
    )(cond, w, b.reshape(1, n))


class _Geom:
    def __init__(self, batch, ctx_len, seq):
        assert ctx_len % ROW_TILE == 0 and seq % ROW_TILE == 0
        assert seq % GRID_W == 0 and seq % BLOCK == 0 and ctx_len % BLOCK == 0
        self.batch = batch
        self.ctx_len = ctx_len
        self.seq = seq
        self.lt = ctx_len + seq
        self.m = batch * self.lt
        self.tiles_b = self.lt // ROW_TILE
        self.ctx_tiles = ctx_len // ROW_TILE
        self.tiles = batch * self.tiles_b

    def mod_row(self, i):
        return 2 * (i // self.tiles_b) + ((i % self.tiles_b) >= self.ctx_tiles).astype(jnp.int32)


def _norm_kernel(x_ref, g_ref, *rest, rows):
    x = x_ref[...]
    y = x * lax.rsqrt(jnp.mean(x * x, axis=-1, keepdims=True) + NORM_EPS) * g_ref[...]
    if rows is None:
        (o_ref,) = rest
    else:
        mod_ref, o_ref = rest
        mod = mod_ref[0]
        y = y * (1.0 + mod[rows[1]:rows[1] + 1]) + mod[rows[0]:rows[0] + 1]
    o_ref[...] = y.astype(o_ref.dtype)


def _norm_mod(geom, x, g, modtab, rows):
    return pl.pallas_call(
        functools.partial(_norm_kernel, rows=rows),
        grid=(geom.tiles,),
        in_specs=[pl.BlockSpec((ROW_TILE, D_MODEL), lambda i: (i, 0)),
                  pl.BlockSpec((1, D_MODEL), lambda i: (0, 0)),
                  pl.BlockSpec((1, 6, D_MODEL), lambda i: (geom.mod_row(i), 0, 0))],
        out_specs=pl.BlockSpec((ROW_TILE, D_MODEL), lambda i: (i, 0)),
        out_shape=jax.ShapeDtypeStruct((geom.m, D_MODEL), BF16),
        compiler_params=_params(("parallel",)),
        name="norm_mod",
    )(x, g.reshape(1, D_MODEL), modtab)


def _final_norm(geom, x, g):
    per_b = geom.seq // ROW_TILE

    def in_map(i):
        return ((i // per_b) * geom.tiles_b + geom.ctx_tiles + i % per_b, 0)

    return pl.pallas_call(
        functools.partial(_norm_kernel, rows=None),
        grid=(geom.batch * per_b,),
        in_specs=[pl.BlockSpec((ROW_TILE, D_MODEL), in_map),
                  pl.BlockSpec((1, D_MODEL), lambda i: (0, 0))],
        out_specs=pl.BlockSpec((ROW_TILE, D_MODEL), lambda i: (i, 0)),
        out_shape=jax.ShapeDtypeStruct((geom.batch * geom.seq, D_MODEL), F32),
        compiler_params=_params(("parallel",)),
        name="final_norm",
    )(x, g.reshape(1, D_MODEL))


def _mm_store(acc, res_ref, mod_refs, o_ref, epilogue, gate_row, norm=None, cs=slice(None)):
    if epilogue == "resid":
        for s, mod_ref in enumerate(mod_refs):
            rs = slice(s * ROW_TILE, (s + 1) * ROW_TILE)
            xn = res_ref[rs, cs] + mod_ref[0, gate_row:gate_row + 1, cs] * acc[rs]
            o_ref[rs, cs] = xn
            if norm is not None:
                g_ref, nmod_refs, h_ref, rows = norm
                nmod = nmod_refs[s][0]
                y = xn * lax.rsqrt(jnp.mean(xn * xn, axis=-1, keepdims=True) + NORM_EPS) * g_ref[...]
                h_ref[rs] = (y * (1.0 + nmod[rows[1]:rows[1] + 1]) + nmod[rows[0]:rows[0] + 1]).astype(h_ref.dtype)
        return
    if epilogue == "sigmoid":
        acc = _sigmoid(acc)
    elif epilogue == "relu2":
        acc = jnp.square(jnp.maximum(acc, 0.0))
    o_ref[:, cs] = acc.astype(o_ref.dtype)


def _col_chunks(tn, norm):
    if norm is not None or tn % MXU_COLS:
        return [slice(None)]
    return [slice(c, c + MXU_COLS) for c in range(0, tn, MXU_COLS)]


def _mm_kernel(a_ref, b_ref, *rest, epilogue, gate_row, nk, n_sub, norm_rows):
    rest = list(rest)
    res_ref, mod_refs, norm = None, (), None
    if epilogue == "resid":
        res_ref, mod_refs, rest = rest[0], rest[1:1 + n_sub], rest[1 + n_sub:]
        if norm_rows is not None:
            norm = (rest[0], rest[1:1 + n_sub], rest[2 + n_sub], norm_rows)
            rest = [rest[1 + n_sub]] + rest[3 + n_sub:]
    o_ref = rest[0]
    chunks = _col_chunks(o_ref.shape[1], norm)
    if nk == 1:
        for cs in chunks:
            acc = jnp.dot(a_ref[...], b_ref[:, cs], preferred_element_type=F32)
            _mm_store(acc, res_ref, mod_refs, o_ref, epilogue, gate_row, norm, cs)
        return
    acc_ref = rest[1]
    k = pl.program_id(2)

    @pl.when(k == 0)
    def _():
        acc_ref[...] = jnp.dot(a_ref[...], b_ref[...], preferred_element_type=F32)

    @pl.when((k > 0) & (k < nk - 1))
    def _():
        acc_ref[...] += jnp.dot(a_ref[...], b_ref[...], preferred_element_type=F32)

    @pl.when(k == nk - 1)
    def _():
        for cs in chunks:
            acc = acc_ref[:, cs] + jnp.dot(a_ref[...], b_ref[:, cs], preferred_element_type=F32)
            _mm_store(acc, res_ref, mod_refs, o_ref, epilogue, gate_row, norm, cs)


def _mm_w32_kernel(a_ref, w_ref, *rest, epilogue, gate_row):
    res_ref, mod_refs = (rest[0], rest[1:-2]) if epilogue == "resid" else (None, ())
    o_ref, wb_ref = rest[-2:]

    @pl.when(pl.program_id(1) == 0)
    def _():
        def cast_rows(r, carry):
            rows = pl.ds(pl.multiple_of(r * CAST_ROWS, CAST_ROWS), CAST_ROWS)
            wb_ref[rows, :] = w_ref[0, rows, :].astype(BF16)
            return carry

        lax.fori_loop(0, wb_ref.shape[0] // CAST_ROWS, cast_rows, 0)

    for cs in _col_chunks(o_ref.shape[1], None):
        acc = jnp.dot(a_ref[...], wb_ref[:, cs], preferred_element_type=F32)
        _mm_store(acc, res_ref, mod_refs, o_ref, epilogue, gate_row, cs=cs)


def _pick_tile(n, cap, unit=LANES):
    t = (min(cap, n) // unit) * unit
    while n % t:
        t -= unit
    return t


class _RowPlan:
    def __init__(self, geom, tm_cap, latent_only=False, rows=None):
        self.geom = geom
        if rows is not None:
            self.tm = _pick_tile(rows, tm_cap, ROW_TILE)
            self.n = rows // self.tm
            self.off = lambda i: i * self.tm
        elif latent_only:
            self.tm = _pick_tile(geom.seq, tm_cap, ROW_TILE)
            per_b = geom.seq // self.tm
            self.n = geom.batch * per_b
            self.off = lambda i: (i // per_b) * geom.lt + geom.ctx_len + (i % per_b) * self.tm
        else:
            self.tm = _pick_tile(geom.m, tm_cap, ROW_TILE)
            self.n = geom.m // self.tm
            self.off = lambda i: i * self.tm
        self.n_sub = self.tm // ROW_TILE

    def spec(self, width, at):
        def index(*g):
            i, col = at(*g)
            col = col if isinstance(col, int) else pl.multiple_of(col, LANES)
            return pl.multiple_of(self.off(i), ROW_TILE), col
        return pl.BlockSpec((pl.Element(self.tm), pl.Element(width)), index)

    def mod_specs(self, tn, at):
        return [pl.BlockSpec((1, 6, tn), lambda *g, s=s: (
            self.geom.mod_row(self.off(at(*g)[0]) // ROW_TILE + s), 0, at(*g)[1])) for s in range(self.n_sub)]


def _matmul(plan, a, b, out_dtype, epilogue="none", res=None, modtab=None, gate_row=0, norm=None,
            tn_cap=1024, tk_cap=2048, name="matmul"):
    m, kdim = a.shape
    n = b.shape[1]
    tm = plan.tm
    tn = n if norm is not None else _pick_tile(n, tn_cap)
    tk = _pick_tile(kdim, tk_cap)
    nk = kdim // tk
    row_j = lambda i, j, k: (i, j * tn)
    tile_ij = lambda i, j, k: (i, j)
    in_specs = [plan.spec(tk, lambda i, j, k: (i, k * tk)),
                pl.BlockSpec((tk, tn), lambda i, j, k: (k, j))]
    args = [a, b]
    out_specs = plan.spec(tn, row_j)
    out_shape = jax.ShapeDtypeStruct((m, n), out_dtype)
    if epilogue == "resid":
        in_specs += [plan.spec(tn, row_j)] + plan.mod_specs(tn, tile_ij)
        args += [res] + [modtab] * plan.n_sub
        if norm is not None:
            gain, nmodtab, _ = norm
            in_specs += [pl.BlockSpec((1, tn), lambda i, j, k: (0, 0))] + plan.mod_specs(tn, tile_ij)
            args += [gain.reshape(1, n)] + [nmodtab] * plan.n_sub
            out_specs = [out_specs, plan.spec(tn, row_j)]
            out_shape = [out_shape, jax.ShapeDtypeStruct((m, n), BF16)]
    return pl.pallas_call(
        functools.partial(_mm_kernel, epilogue=epilogue, gate_row=gate_row, nk=nk, n_sub=plan.n_sub,
                          norm_rows=None if norm is None else norm[2]),
        grid=(plan.n, n // tn, nk),
        in_specs=in_specs,
        out_specs=out_specs,
        out_shape=out_shape,
        scratch_shapes=[pltpu.VMEM((tm, tn), F32)] if nk > 1 else [],
        compiler_params=_params(("parallel", "parallel", "arbitrary")),
        name=name,
    )(*args)


def _matmul_w32(plan, a, w, layer, out_dtype, epilogue="none", res=None, modtab=None, gate_row=0,
                cols=None, tn_cap=1024, name="matmul_w32"):
    m, kdim = a.shape
    col0, n = cols if cols is not None else (0, w.shape[2])
    assert col0 % LANES == 0
    tm = plan.tm
    tn = _pick_tile(n, tn_cap)
    row_j = lambda j, i: (i, j * tn)
    in_specs = [plan.spec(kdim, lambda j, i: (i, 0)),
                pl.BlockSpec((pl.Element(1), pl.Element(kdim), pl.Element(tn)),
                             lambda j, i: (layer, 0, pl.multiple_of(col0 + j * tn, LANES)))]
    args = [a, w]
    if epilogue == "resid":
        in_specs += [plan.spec(tn, row_j)] + plan.mod_specs(tn, lambda j, i: (i, j))
        args += [res] + [modtab] * plan.n_sub
    return pl.pallas_call(
        functools.partial(_mm_w32_kernel, epilogue=epilogue, gate_row=gate_row),
        grid=(n // tn, plan.n),
        in_specs=in_specs,
        out_specs=plan.spec(tn, row_j),
        out_shape=jax.ShapeDtypeStruct((m, n), out_dtype),
        scratch_shapes=[pltpu.VMEM((kdim, tn), BF16)],
        compiler_params=_params(("parallel", "arbitrary")),
        name=name,
    )(*args)


def _head_sum_matrix(width, head):
    idx = np.arange(width) // head
    return jnp.asarray((idx[:, None] == idx[None, :]).astype(np.float32), dtype=BF16)


def _prep_kernel(p_ref, lora_ref, halo_ref, hlora_ref, mu_ref, w0_ref, wup_ref, a0_ref, aup_ref,
                 kk_ref, ka_ref, hs_ref, r_o, lw_o, k_o, v_o, kkn_o, b_o, *, tiles_b, ctx_tiles):
    d = pl.program_id(0)
    j = pl.program_id(1) % tiles_b
    fwd = d == 0
    f = jnp.concatenate([p_ref[...], lora_ref[...]], axis=-1)
    t = f.shape[0]
    halo = jnp.concatenate([halo_ref[...], hlora_ref[...]], axis=-1)
    at_start = (j == 0) | (j == ctx_tiles)
    at_end = (j == ctx_tiles - 1) | (j == tiles_b - 1)
    edge = jnp.where(fwd, halo[7:8], halo[0:1])
    edge = jnp.where((fwd & at_start) | (jnp.logical_not(fwd) & at_end), 0.0, edge)
    row = lax.broadcasted_iota(jnp.int32, (t, 1), 0)
    prev = jnp.where(row == 0, edge, pltpu.roll(f, 1, 0))
    nxt = jnp.where(row == t - 1, edge, pltpu.roll(f, t - 1, 0))
    f = f + mu_ref[0] * (jnp.where(fwd, prev, nxt) - f)

    r = f[:, 0:BRANCH_W]
    k = f[:, BRANCH_W:2 * BRANCH_W]
    v = f[:, 2 * BRANCH_W:3 * BRANCH_W]
    wl = f[:, 3 * BRANCH_W:3 * BRANCH_W + DECAY_LORA]
    al = f[:, 3 * BRANCH_W + DECAY_LORA:SHIFT_W]
    w_raw = w0_ref[0] + _dot3(jnp.tanh(wl), wup_ref[0])
    lw = -math.exp(-0.5) * _sigmoid(w_raw)
    a = _sigmoid(a0_ref[0] + _dot3(al, aup_ref[0]))
    kk = k * kk_ref[...]
    norm = jnp.sqrt(_dot_sel(kk * kk, hs_ref[...]))
    kk = kk / jnp.maximum(norm, 1e-12)
    r_o[0] = r
    lw_o[0] = lw
    k_o[0] = k * (1.0 + (a - 1.0) * ka_ref[...])
    v_o[0] = v
    kkn_o[0] = kk
    b_o[0] = kk * a


def _rwkv_prep(geom, p1, lp):
    m = geom.m
    t = ROW_TILE
    tb = geom.tiles_b
    rkv_w = 3 * BRANCH_W
    lora_blk0 = O_LORA // DIR_LORA_W
    n_blk8 = m // 8

    def halo_idx(d, i):
        before = jnp.maximum(i * (t // 8) - 1, 0)
        after = jnp.minimum((i + 1) * (t // 8), n_blk8 - 1)
        return jnp.where(d == 0, before, after)

    out = jax.ShapeDtypeStruct((N_DIR, m, BRANCH_W), F32)
    ospec = pl.BlockSpec((1, t, BRANCH_W), lambda d, i: (d, i, 0))
    vec = lambda a: a.reshape(1, BRANCH_W)
    dvec = pl.BlockSpec((1, 1, BRANCH_W), lambda d, i: (d, 0, 0))
    return pl.pallas_call(
        functools.partial(_prep_kernel, tiles_b=tb, ctx_tiles=geom.ctx_tiles),
        grid=(N_DIR, geom.tiles),
        in_specs=[pl.BlockSpec((t, rkv_w), lambda d, i: (i, 0)),
                  pl.BlockSpec((t, DIR_LORA_W), lambda d, i: (i, lora_blk0 + d)),
                  pl.BlockSpec((8, rkv_w), lambda d, i: (halo_idx(d, i), 0)),
                  pl.BlockSpec((8, DIR_LORA_W), lambda d, i: (halo_idx(d, i), lora_blk0 + d)),
                  pl.BlockSpec((1, 1, SHIFT_W), lambda d, i: (d, 0, 0)),
                  dvec,
                  pl.BlockSpec((1, DECAY_LORA, BRANCH_W), lambda d, i: (d, 0, 0)),
                  dvec,
                  pl.BlockSpec((1, AICL_LORA, BRANCH_W), lambda d, i: (d, 0, 0)),
                  pl.BlockSpec((1, BRANCH_W), lambda d, i: (0, 0)),
                  pl.BlockSpec((1, BRANCH_W), lambda d, i: (0, 0)),
                  pl.BlockSpec((BRANCH_W, BRANCH_W), lambda d, i: (0, 0))],
        out_specs=[ospec] * 6,
        out_shape=[out] * 6,
        compiler_params=_params(("parallel", "parallel")),
        name="rwkv_prep",
    )(p1, p1, p1, p1, lp["rwkv_mu"].reshape(N_DIR, 1, SHIFT_W),
      lp["rwkv_w0"].reshape(N_DIR, 1, BRANCH_W), lp["rwkv_w_up"],
      lp["rwkv_a0"].reshape(N_DIR, 1, BRANCH_W), lp["rwkv_a_up"],
      vec(lp["rwkv_k_k"]), vec(lp["rwkv_k_a"]), _head_sum_matrix(BRANCH_W, RWKV_HEAD))


def _scan_kernel(*refs):
    (r_f, r_b, lw_f, lw_b, k_f, k_b, v_f, v_b, kk_f, kk_b, b_f, b_b, yf_ref, yb_ref, h_ref) = refs
    c = pl.program_id(1)
    C = SCAN_CHUNK
    W = 2 * C

    @pl.when(c == 0)
    def _():
        h_ref[...] = jnp.zeros_like(h_ref)

    rr = lax.broadcasted_iota(jnp.int32, (C, C), 0)
    cc = lax.broadcasted_iota(jnp.int32, (C, C), 1)
    lane = lax.broadcasted_iota(jnp.int32, (1, PAIR_W), 1)
    m_a = (lane < RWKV_HEAD).astype(F32)
    m_b = 1.0 - m_a
    r2 = lax.broadcasted_iota(jnp.int32, (W, W), 0)
    c2 = lax.broadcasted_iota(jnp.int32, (W, W), 1)
    same = (r2 // C) == (c2 // C)
    eye = (r2 == c2).astype(F32)

    def pairs(x):
        return [jnp.concatenate([x[:, p * PAIR_W:(p + 1) * PAIR_W] * m_a,
                                 x[:, p * PAIR_W:(p + 1) * PAIR_W] * m_b], axis=0) for p in range(N_PAIRS)]

    stacks = {name: [] for name in ("a", "b", "k", "r", "v", "bc", "kc", "pt")}
    strict, incl = [], []
    for sgn, (r_ref, lw_ref, k_ref, v_ref, kk_ref, b_ref) in (
            (1, (r_f, lw_f, k_f, v_f, kk_f, b_f)), (-1, (r_b, lw_b, k_b, v_b, kk_b, b_b))):
        lw = lw_ref[0]
        tri = jnp.where((rr - cc) * sgn >= 0, 1.0, 0.0).astype(BF16)
        lp_in = _dot_sel_lhs(tri, lw)
        tot = jnp.sum(lw, axis=0, keepdims=True)
        e_neg = jnp.exp(-lp_in)
        e_chk = jnp.exp(tot - lp_in)
        p_tot = jnp.exp(tot)
        stacks["a"] += pairs(-kk_ref[0] * jnp.exp(lp_in - lw))
        stacks["b"] += pairs(b_ref[0] * e_neg)
        stacks["k"] += pairs(k_ref[0] * e_neg)
        stacks["r"] += pairs(r_ref[0] * jnp.exp(lp_in))
        stacks["v"] += pairs(v_ref[0])
        stacks["bc"] += pairs(b_ref[0] * e_chk)
        stacks["kc"] += pairs(k_ref[0] * e_chk)
        stacks["pt"] += [p_tot[:, p * PAIR_W:(p + 1) * PAIR_W] for p in range(N_PAIRS)]
        dt = (r2 % C - c2 % C) * sgn
        strict += [(same & (dt > 0)).astype(F32)] * N_PAIRS
        incl += [(same & (dt >= 0)).astype(F32)] * N_PAIRS
    a_s, b_s, k_s, r_s, v_s, bc_s, kc_s, p_tot = (jnp.stack(stacks[n]) for n in
                                                  ("a", "b", "k", "r", "v", "bc", "kc", "pt"))
    strict = jnp.stack(strict)
    incl = jnp.stack(incl)

    a_b, b_b, k_b, r_b, v_b, bc_b, kc_b = (x.astype(BF16) for x in (a_s, b_s, k_s, r_s, v_s, bc_s, kc_s))
    big = _bdot(jnp.concatenate([a_b, r_b], axis=1), jnp.concatenate([b_b, k_b], axis=1), BNT_DIMS)
    l_ab = big[:, :W, :W] * strict
    l_ak = big[:, :W, W:] * strict
    m_rb = (big[:, W:, :W] * incl).astype(BF16)
    m_rk = big[:, W:, W:] * incl
    t_inv = eye + l_ab
    pw_b = l_ab.astype(BF16)
    pw_b = _bdot(pw_b, pw_b).astype(BF16)
    for _ in range(int(math.log2(C)) - 2):
        both = _bdot(jnp.concatenate([t_inv.astype(BF16), pw_b], axis=1), pw_b)
        t_inv = t_inv + both[:, :W]
        pw_b = both[:, W:].astype(BF16)
    t_inv = t_inv + _bdot(t_inv, pw_b)
    x1 = _bdot(t_inv, jnp.concatenate([a_b, _bdot(l_ak, v_b).astype(BF16)], axis=2)).astype(BF16)
    x2 = _bdot(m_rb, x1)
    r_hat = r_s + x2[:, :, :PAIR_W]
    y0 = x2[:, :, PAIR_W:] + _bdot(m_rk, v_b)
    x3 = _bdot(bc_b, x1, BTN_DIMS)
    g = eye * p_tot + x3[:, :, :PAIR_W]
    h_inc = x3[:, :, PAIR_W:] + _bdot(kc_b, v_b, BTN_DIMS)
    x4 = _bdot(jnp.concatenate([r_hat, g], axis=1), h_ref[...])
    ys = x4[:, :W] + y0
    h_ref[...] = x4[:, W:] + h_inc
    for d, y_ref in enumerate((yf_ref, yb_ref)):
        for p in range(N_PAIRS):
            y_ref[:, p * PAIR_W:(p + 1) * PAIR_W] = ys[d * N_PAIRS + p, :C] + ys[d * N_PAIRS + p, C:]


BNN_DIMS = (((2,), (1,)), ((0,), (0,)))
BNT_DIMS = (((2,), (2,)), ((0,), (0,)))
BTN_DIMS = (((1,), (1,)), ((0,), (0,)))


def _bdot(a, b, dims=BNN_DIMS):
    return lax.dot_general(a.astype(BF16), b.astype(BF16), dims, preferred_element_type=F32)


def _dot_sel_lhs(sel_bf16, a):
    hi = a.astype(BF16)
    r1 = a - hi.astype(F32)
    mid = r1.astype(BF16)
    lo = (r1 - mid.astype(F32)).astype(BF16)
    dg = functools.partial(jnp.dot, preferred_element_type=F32)
    return dg(sel_bf16, hi) + (dg(sel_bf16, mid) + dg(sel_bf16, lo))


def _rwkv_scan(geom, ins):
    C = SCAN_CHUNK
    nch = geom.lt // C
    nctx = geom.ctx_len // C

    def rev(c):
        return jnp.where(c < nctx, nctx - 1 - c, nch - 1 + nctx - c)

    fwd = pl.BlockSpec((1, C, BRANCH_W), lambda b, c: (0, b * nch + c, 0))
    bwd = pl.BlockSpec((1, C, BRANCH_W), lambda b, c: (1, b * nch + rev(c), 0))
    out = jax.ShapeDtypeStruct((geom.m, BRANCH_W), F32)
    return pl.pallas_call(
        _scan_kernel,
        grid=(geom.batch, nch),
        in_specs=[fwd, bwd] * 6,
        out_specs=[pl.BlockSpec((C, BRANCH_W), lambda b, c: (b * nch + c, 0)),
                   pl.BlockSpec((C, BRANCH_W), lambda b, c: (b * nch + rev(c), 0))],
        out_shape=[out, out],
        scratch_shapes=[pltpu.VMEM((N_DIR * N_PAIRS, PAIR_W, PAIR_W), F32)],
        compiler_params=_params(("parallel", "arbitrary")),
        name="rwkv_scan",
    )(*[a for a in ins for _ in range(N_DIR)])


def _readout_kernel(yf_ref, yb_ref, r_ref, k_ref, v_ref, p2_ref, gup_ref, rk_ref, lg_ref, lb_ref, hs_ref, o_ref):
    hs = hs_ref[...]
    y = yf_ref[...] + yb_ref[...]
    inv_n = 1.0 / RWKV_HEAD
    mean = _dot_sel(y, hs) * inv_n
    yc = y - mean
    var = _dot_sel(yc * yc, hs) * inv_n
    yn = yc * lax.rsqrt(var + GN_EPS) * lg_ref[...] + lb_ref[...]
    bonus = jnp.zeros_like(y)
    for d in range(N_DIR):
        bonus = bonus + _dot_sel(r_ref[d] * k_ref[d] * rk_ref[d:d + 1], hs) * v_ref[d]
    g = _dot3(_sigmoid(p2_ref[:, 0:GATE_LORA]), gup_ref[...])
    o_ref[...] = ((yn + bonus) * g).astype(o_ref.dtype)


def _rwkv_readout(geom, y, ins, p2, lp):
    t = ROW_TILE
    dspec = pl.BlockSpec((N_DIR, t, BRANCH_W), lambda i: (0, i, 0))
    vspec = pl.BlockSpec((1, BRANCH_W), lambda i: (0, 0))
    r, _, k, v, _, _ = ins
    return pl.pallas_call(
        _readout_kernel,
        grid=(geom.tiles,),
        in_specs=[pl.BlockSpec((t, BRANCH_W), lambda i: (i, 0)),
                  pl.BlockSpec((t, BRANCH_W), lambda i: (i, 0)),
                  dspec, dspec, dspec,
                  pl.BlockSpec((t, GATE_LORA + BRANCH_W), lambda i: (i, 0)),
                  pl.BlockSpec((GATE_LORA, BRANCH_W), lambda i: (0, 0)),
                  pl.BlockSpec((N_DIR, BRANCH_W), lambda i: (0, 0)),
                  vspec, vspec,
                  pl.BlockSpec((BRANCH_W, BRANCH_W), lambda i: (0, 0))],
        out_specs=pl.BlockSpec((t, BRANCH_W), lambda i: (i, 0)),
        out_shape=jax.ShapeDtypeStruct((geom.m, BRANCH_W), BF16),
        compiler_params=_params(("parallel",)),
        name="rwkv_readout",
    )(y[0], y[1], r, k, v, p2, lp["rwkv_g_up"], lp["rwkv_r_k"].reshape(N_DIR, BRANCH_W),
      lp["rwkv_lnx_g"].reshape(1, BRANCH_W), lp["rwkv_lnx_b"].reshape(1, BRANCH_W),
      _head_sum_matrix(BRANCH_W, RWKV_HEAD))


def _rope_tables(geom):
    half = ATT_HEAD // 2
    nf = half // 2
    inv = ROPE_BASE ** (-jnp.arange(nf, dtype=F32) / nf)
    pos = jnp.arange(geom.seq, dtype=jnp.int32)
    row_ang = (pos // GRID_W).astype(F32)[:, None] * inv[None, :]
    col_ang = (pos % GRID_W).astype(F32)[:, None] * inv[None, :]
    cos = jnp.concatenate([jnp.cos(row_ang)] * 2 + [jnp.cos(col_ang)] * 2, axis=-1)
    sin = jnp.concatenate([-jnp.sin(row_ang), jnp.sin(row_ang), -jnp.sin(col_ang), jnp.sin(col_ang)], axis=-1)
    cos = jnp.concatenate([jnp.ones((geom.ctx_len, ATT_HEAD), F32), cos], axis=0)
    sin = jnp.concatenate([jnp.zeros((geom.ctx_len, ATT_HEAD), F32), sin], axis=0)
    return cos, sin


def _rotate(t, cos, sin):
    w = t.shape[-1]
    nf = ATT_HEAD // 4
    lane = lax.broadcasted_iota(jnp.int32, (1, w), 1)
    partner = jnp.where((lane % (2 * nf)) < nf, pltpu.roll(t, w - nf, 1), pltpu.roll(t, nf, 1))
    return t * cos + partner * sin


def _rope_kernel(p2_ref, kv_ref, cq_ref, sq_ref, ck_ref, sk_ref, q_o, k_o, v_o):
    q = p2_ref[:, GATE_LORA:]
    q_o[...] = (_rotate(q, cq_ref[...], sq_ref[...]) * (ATT_HEAD ** -0.5)).astype(q_o.dtype)
    kv = kv_ref[...]
    k_o[...] = _rotate(kv[:, :ATT_KV_W], ck_ref[...], sk_ref[...]).astype(k_o.dtype)
    v_o[...] = kv[:, ATT_KV_W:].astype(v_o.dtype)


def _rope(geom, p1, p2):
    t = ROW_TILE
    tb = geom.tiles_b
    cos, sin = _rope_tables(geom)
    cq, sq = jnp.tile(cos, (1, ATT_HEADS)), jnp.tile(sin, (1, ATT_HEADS))
    ck, sk = jnp.tile(cos, (1, ATT_KV_HEADS)), jnp.tile(sin, (1, ATT_KV_HEADS))
    qspec = pl.BlockSpec((t, BRANCH_W), lambda i: (i % tb, 0))
    kspec = pl.BlockSpec((t, ATT_KV_W), lambda i: (i % tb, 0))
    m = geom.m
    return pl.pallas_call(
        _rope_kernel,
        grid=(geom.tiles,),
        in_specs=[pl.BlockSpec((t, GATE_LORA + BRANCH_W), lambda i: (i, 0)),
                  pl.BlockSpec((t, 2 * ATT_KV_W), lambda i: (i, O_KV // (2 * ATT_KV_W))),
                  qspec, qspec, kspec, kspec],
        out_specs=[pl.BlockSpec((t, BRANCH_W), lambda i: (i, 0)),
                   pl.BlockSpec((t, ATT_KV_W), lambda i: (i, 0)),
                   pl.BlockSpec((t, ATT_KV_W), lambda i: (i, 0))],
        out_shape=[jax.ShapeDtypeStruct((m, BRANCH_W), BF16),
                   jax.ShapeDtypeStruct((m, ATT_KV_W), BF16),
                   jax.ShapeDtypeStruct((m, ATT_KV_W), BF16)],
        compiler_params=_params(("parallel",)),
        name="rope",
    )(p2, p1, cq, sq, ck, sk)


def _attn_kernel(q_ref, kc_ref, vc_ref, kp_ref, ko_ref, kn_ref, vp_ref, vo_ref, vn_ref, sink_ref, o_ref,
                 s_ref, p_ref, *, ctx_blocks, blocks_b):
    j = pl.program_id(1)
    q = q_ref[...]
    k_all = jnp.concatenate([kp_ref[...], ko_ref[...], kn_ref[...], kc_ref[...]], axis=0)
    v_all = jnp.concatenate([vp_ref[...], vo_ref[...], vn_ref[...], vc_ref[...]], axis=0)
    nloc = 3 * BLOCK
    slab = ATT_SLAB
    qi0 = lax.broadcasted_iota(jnp.int32, (slab, nloc), 0)
    ki = lax.broadcasted_iota(jnp.int32, (slab, nloc), 1)
    never = 4 * BLOCK
    prev_off = jnp.where(j > ctx_blocks, 0, never)
    own_hi = jnp.where(j >= ctx_blocks, 2 * BLOCK, BLOCK)
    next_off = 2 * BLOCK - jnp.where((j >= ctx_blocks) & (j < blocks_b - 1), 0, never)
    outs = []
    for g in range(ATT_KV_HEADS):
        gs = slice(g * ATT_HEAD, (g + 1) * ATT_HEAD)
        qg = jnp.concatenate([q[:, (g * ATT_REP + h) * ATT_HEAD:(g * ATT_REP + h + 1) * ATT_HEAD]
                              for h in range(ATT_REP)], axis=0)
        s_ref[g] = lax.dot_general(qg, k_all[:, gs], NT_DIMS, preferred_element_type=F32)
        dens = []
        for blk in range(ATT_REP * BLOCK // slab):
            rs = slice(blk * slab, (blk + 1) * slab)
            head = g * ATT_REP + blk * slab // BLOCK
            qi = qi0 + (blk * slab) % BLOCK
            valid = (((ki < BLOCK) & (ki >= qi + prev_off)) | ((ki >= BLOCK) & (ki < own_hi))
                     | ((ki >= 2 * BLOCK) & (ki <= qi + next_off)))
            s_loc = jnp.where(valid, s_ref[g, rs, :nloc], NEG_INF)
            s_ctx = s_ref[g, rs, nloc:]
            sink = sink_ref[head:head + 1, 0:1]
            mx = jnp.maximum(jnp.maximum(jnp.max(s_loc, axis=-1, keepdims=True),
                                         jnp.max(s_ctx, axis=-1, keepdims=True)), sink)
            e_loc = jnp.exp(s_loc - mx)
            e_ctx = jnp.exp(s_ctx - mx)
            dens.append(jnp.sum(e_loc, axis=-1, keepdims=True) + jnp.sum(e_ctx, axis=-1, keepdims=True)
                        + jnp.exp(sink - mx))
            p_ref[g, rs, :nloc] = e_loc.astype(BF16)
            p_ref[g, rs, nloc:] = e_ctx.astype(BF16)
        o = jnp.dot(p_ref[g], v_all[:, gs], preferred_element_type=F32) / jnp.concatenate(dens, axis=0)
        outs += [o[h * BLOCK:(h + 1) * BLOCK] for h in range(ATT_REP)]
    o_ref[...] = jnp.concatenate(outs, axis=-1).astype(o_ref.dtype)


def _attention(geom, q, k, v, sink):
    nb = geom.lt // BLOCK
    cb = geom.ctx_len // BLOCK
    row = lambda b, j: (b * nb + j, 0)
    prev = lambda b, j: (b * nb + jnp.maximum(j - 1, 0), 0)
    nxt = lambda b, j: (b * nb + jnp.minimum(j + 1, nb - 1), 0)
    ctx = lambda b, j: (b * (geom.lt // geom.ctx_len), 0)
    assert geom.lt % geom.ctx_len == 0
    kvs = lambda f: pl.BlockSpec((BLOCK, ATT_KV_W), f)
    cspec = pl.BlockSpec((geom.ctx_len, ATT_KV_W), ctx)
    sink_tab = jnp.broadcast_to(sink.astype(F32)[:, None], (ATT_HEADS, LANES))
    return pl.pallas_call(
        functools.partial(_attn_kernel, ctx_blocks=cb, blocks_b=nb),
        grid=(geom.batch, nb),
        in_specs=[pl.BlockSpec((BLOCK, BRANCH_W), row), cspec, cspec,
                  kvs(prev), kvs(row), kvs(nxt), kvs(prev), kvs(row), kvs(nxt),
                  pl.BlockSpec((ATT_HEADS, LANES), lambda b, j: (0, 0))],
        out_specs=pl.BlockSpec((BLOCK, BRANCH_W), row),
        out_shape=jax.ShapeDtypeStruct((geom.m, BRANCH_W), BF16),
        scratch_shapes=[pltpu.VMEM((ATT_KV_HEADS, ATT_REP * BLOCK, 3 * BLOCK + geom.ctx_len), F32),
                        pltpu.VMEM((ATT_KV_HEADS, ATT_REP * BLOCK, 3 * BLOCK + geom.ctx_len), BF16)],
        compiler_params=_params(("parallel", "parallel")),
        name="attention",
    )(q, k, v, k, k, k, v, v, v, sink_tab)


def _conv_kernel(u_ref, up_ref, un_ref, dw_ref, db_ref, lg_ref, lb_ref, o_ref, hp_ref, sh_ref,
                 *, tiles_b, ctx_tiles):
    j = pl.program_id(0) % tiles_b
    t = u_ref.shape[0]

    def glu(u):
        return u[:, :BRANCH_W] * _sigmoid(u[:, BRANCH_W:])

    at_start = (j == 0) | (j == ctx_tiles)
    at_end = (j == ctx_tiles - 1) | (j == tiles_b - 1)
    hp_ref[0:HALO] = jnp.where(at_start, 0.0, glu(up_ref[...]))
    hp_ref[HALO:HALO + t] = glu(u_ref[...])
    hp_ref[HALO + t:] = jnp.where(at_end, 0.0, glu(un_ref[...]))
    span = t + 2 * HALO - SUBLANES
    for r in range(1, SUBLANES):
        sh_ref[r, 0:span] = hp_ref[pl.ds(r, span)]
    for chunk in range(t // CONV_ROWS):
        acc = jnp.zeros((CONV_ROWS, BRANCH_W), F32) + db_ref[...]
        for tap in range(CONV_K):
            q, r = divmod(HALO - CONV_PAD + tap, SUBLANES)
            rows = pl.ds(chunk * CONV_ROWS + q * SUBLANES, CONV_ROWS)
            src = hp_ref[rows] if r == 0 else sh_ref[r, rows]
            acc = acc + src * dw_ref[tap:tap + 1]
        mean = jnp.mean(acc, axis=-1, keepdims=True)
        cen = acc - mean
        var = jnp.mean(cen * cen, axis=-1, keepdims=True)
        h = cen * lax.rsqrt(var + LN_EPS) * lg_ref[...] + lb_ref[...]
        o_ref[chunk * CONV_ROWS:(chunk + 1) * CONV_ROWS] = (h * _sigmoid(h)).astype(o_ref.dtype)


def _conv(geom, p4, lp):
    t = ROW_TILE
    nh = geom.m // HALO
    vspec = pl.BlockSpec((1, BRANCH_W), lambda i: (0, 0))
    return pl.pallas_call(
        functools.partial(_conv_kernel, tiles_b=geom.tiles_b, ctx_tiles=geom.ctx_tiles),
        grid=(geom.tiles,),
        in_specs=[pl.BlockSpec((t, 2 * BRANCH_W), lambda i: (i, 0)),
                  pl.BlockSpec((HALO, 2 * BRANCH_W), lambda i: (jnp.maximum(i * (t // HALO) - 1, 0), 0)),
                  pl.BlockSpec((HALO, 2 * BRANCH_W), lambda i: (jnp.minimum((i + 1) * (t // HALO), nh - 1), 0)),
                  pl.BlockSpec((CONV_K, BRANCH_W), lambda i: (0, 0)),
                  vspec, vspec, vspec],
        out_specs=pl.BlockSpec((t, BRANCH_W), lambda i: (i, 0)),
        out_shape=jax.ShapeDtypeStruct((geom.m, BRANCH_W), BF16),
        scratch_shapes=[pltpu.VMEM((t + 2 * HALO, BRANCH_W), F32),
                        pltpu.VMEM((SUBLANES, t + 2 * HALO, BRANCH_W), F32)],
        compiler_params=_params(("parallel",)),
        name="conformer_conv",
    )(p4, p4, p4, lp["conv_dw"], lp["conv_dw_b"].reshape(1, BRANCH_W),
      lp["conv_ln_g"].reshape(1, BRANCH_W), lp["conv_ln_b"].reshape(1, BRANCH_W))


def _dft_cos_sin(n, scale):
    idx = np.arange(n, dtype=np.int64)
    ang = 2.0 * np.pi * ((idx[:, None] * idx[None, :]) % n).astype(np.float64) / n
    return np.cos(ang) * scale, np.sin(ang) * scale


def _channel_dft():
    c, s = _dft_cos_sin(FNO_GROUP_W, FNO_GROUP_W ** -0.5)
    eye = np.eye(FNO_GROUPS)
    return jnp.asarray(np.concatenate([np.kron(eye, c), np.kron(eye, s)], axis=1), dtype=F32).astype(BF16)


def _dft_pos_kernel(c_ref, s_ref, gc_ref, gs_ref, o_ref):
    o_ref[0] = (jnp.dot(c_ref[...], gc_ref[0], preferred_element_type=F32)
                + jnp.dot(s_ref[...], gs_ref[0], preferred_element_type=F32)).astype(o_ref.dtype)


def _fourier(u):
    bsz, length, _ = u.shape
    gcs = _matmul(_RowPlan(None, 1024, rows=bsz * length), u.reshape(bsz * length, BRANCH_W), _channel_dft(),
                  BF16, name="dft_channels")
    gcs = gcs.reshape(bsz, length, 2 * BRANCH_W)
    c, s = _dft_cos_sin(length, length ** -0.5)
    tm = _pick_tile(length, 512, ROW_TILE)
    return pl.pallas_call(
        _dft_pos_kernel,
        grid=(length // tm, bsz),
        in_specs=[pl.BlockSpec((tm, length), lambda i, b: (i, 0)),
                  pl.BlockSpec((tm, length), lambda i, b: (i, 0)),
                  pl.BlockSpec((1, length, BRANCH_W), lambda i, b: (b, 0, 0)),
                  pl.BlockSpec((1, length, BRANCH_W), lambda i, b: (b, 0, 1))],
        out_specs=pl.BlockSpec((1, tm, BRANCH_W), lambda i, b: (b, i, 0)),
        out_shape=jax.ShapeDtypeStruct((bsz, length, BRANCH_W), BF16),
        compiler_params=_params(("parallel", "parallel")),
        name="dft_positions",
    )(jnp.asarray(c, dtype=F32).astype(BF16), jnp.asarray(-s, dtype=F32).astype(BF16), gcs, gcs)


def _merge_kernel(f0, f1, f2, f3, w_ref, g0, g1, g2, g3, o_ref):
    acc = None
    for i, (f, g) in enumerate(((f0, g0), (f1, g1), (f2, g2), (f3, g3))):
        term = jnp.dot(f[...], w_ref[i], preferred_element_type=F32) * g[...].astype(F32)
        acc = term if acc is None else acc + term
    o_ref[...] = acc.astype(o_ref.dtype)


def _merge(plan, feats, w_branch, gate):
    m = gate.shape[0]
    tn = 1024
    fspec = plan.spec(BRANCH_W, lambda i, j: (i, 0))
    gspec = lambda br: plan.spec(tn, lambda i, j: (i, br * D_MODEL + j * tn))
    return pl.pallas_call(
        _merge_kernel,
        grid=(plan.n, D_MODEL // tn),
        in_specs=[fspec] * 4 + [pl.BlockSpec((N_BRANCH, BRANCH_W, tn), lambda i, j: (0, 0, j))]
        + [gspec(br) for br in range(N_BRANCH)],
        out_specs=plan.spec(tn, lambda i, j: (i, j * tn)),
        out_shape=jax.ShapeDtypeStruct((m, D_MODEL), BF16),
        compiler_params=_params(("parallel", "parallel")),
        name="branch_merge",
    )(*feats, w_branch, gate, gate, gate, gate)


def _mixer(geom, h, xall, modtab, lp, w_in, layer, norm2_g, latent_only):
    rows_all = _RowPlan(geom, 1024)
    rows_out = _RowPlan(geom, 1024, latent_only)
    proj = lambda plan, lo, hi, dt, name, ep="none": _matmul_w32(plan, h, w_in, layer, dt, epilogue=ep,
                                                                 cols=(lo, hi - lo), name=name)
    p1 = proj(rows_all, 0, CTX_STATE_COLS, F32, "in_proj_state")
    p2 = proj(rows_all, O_G, O_FNO, F32, "in_proj_gq")
    p3 = proj(rows_all, O_FNO, O_CONV, BF16, "in_proj_fno")
    p4 = proj(rows_all, O_CONV, O_GATE, F32, "in_proj_conv")
    gate = proj(rows_out, O_GATE, IN_W, BF16, "in_proj_gate", "sigmoid")

    ins = _rwkv_prep(geom, p1, lp)
    y = _rwkv_scan(geom, ins)
    rw = _rwkv_readout(geom, y, ins, p2, lp)

    q, k, v = _rope(geom, p1, p2)
    att = _attention(geom, q, k, v, lp["att_sink"])

    cv = _conv(geom, p4, lp)

    p3 = p3.reshape(geom.batch, geom.lt, BRANCH_W)
    fno = jnp.concatenate([_fourier(p3[:, :geom.ctx_len]), _fourier(p3[:, geom.ctx_len:])], axis=1)
    fno = fno.reshape(geom.m, BRANCH_W)

    rows_half = _RowPlan(geom, 512, latent_only)
    mixed = _merge(rows_half, (fno, rw, att, cv), lp["w_branch"].astype(BF16), gate)
    return _matmul(rows_half, mixed, lp["w_out"].astype(BF16), F32, epilogue="resid", res=xall,
                   modtab=modtab, gate_row=2, norm=(norm2_g, modtab, (3, 4)), name="out_proj")


def kernel(x, c, ctx, c_ctx, ada_w, ada_b, norm1_g, norm2_g, w_in, rwkv_mu, rwkv_w0, rwkv_w_up, rwkv_a0, rwkv_a_up, rwkv_k_k, rwkv_k_a, rwkv_r_k, rwkv_g_up, rwkv_lnx_g, rwkv_lnx_b, att_sink, conv_dw, conv_dw_b, conv_ln_g, conv_ln_b, w_branch, w_out, w_mlp1, w_mlp2, final_g):
    batch, seq, _ = x.shape
    geom = _Geom(batch, ctx.shape[1], seq)
    depth = w_in.shape[0]
    assert batch + 1 <= 8
    cond = jnp.zeros((8, D_MODEL), F32).at[:batch].set(c).at[batch].set(c_ctx)
    xall = jnp.concatenate([ctx, x], axis=1).reshape(geom.m, D_MODEL)
    modtabs = []
    for l in range(depth):
        mod = _ada_mod(cond, ada_w, l, ada_b[l])
        mod_x = mod[:batch].reshape(batch, 1, 6, D_MODEL)
        mod_c = jnp.broadcast_to(mod[batch].reshape(1, 1, 6, D_MODEL), (batch, 1, 6, D_MODEL))
        modtabs.append(jnp.concatenate([mod_c, mod_x], axis=1).reshape(2 * batch, 6, D_MODEL))
    h = _norm_mod(geom, xall, norm1_g[0], modtabs[0], rows=(0, 1))
    for l in range(depth):
        modtab = modtabs[l]
        lp = {
            "rwkv_mu": rwkv_mu[l], "rwkv_w0": rwkv_w0[l], "rwkv_w_up": rwkv_w_up[l],
            "rwkv_a0": rwkv_a0[l], "rwkv_a_up": rwkv_a_up[l], "rwkv_k_k": rwkv_k_k[l],
            "rwkv_k_a": rwkv_k_a[l], "rwkv_r_k": rwkv_r_k[l], "rwkv_g_up": rwkv_g_up[l],
            "rwkv_lnx_g": rwkv_lnx_g[l], "rwkv_lnx_b": rwkv_lnx_b[l], "att_sink": att_sink[l],
            "conv_dw": conv_dw[l], "conv_dw_b": conv_dw_b[l], "conv_ln_g": conv_ln_g[l],
            "conv_ln_b": conv_ln_b[l], "w_branch": w_branch[l], "w_out": w_out[l],
        }
        last = l + 1 == depth
        xall, h2 = _mixer(geom, h, xall, modtab, lp, w_in, l, norm2_g[l], latent_only=last)
        hid = _matmul_w32(_RowPlan(geom, 1024, last), h2, w_mlp1, l, BF16, epilogue="relu2", name="mlp_up")
        down = functools.partial(_matmul, a=hid, b=w_mlp2[l].astype(BF16), out_dtype=F32, epilogue="resid",
                                 res=xall, modtab=modtab, gate_row=5, name="mlp_down")
        if last:
            xall = down(_RowPlan(geom, 1024, True))
        else:
            xall, h = down(_RowPlan(geom, 512), norm=(norm1_g[l + 1], modtabs[l + 1], (0, 1)))
    return _final_norm(geom, xall, final_g).reshape(batch, seq, D_MODEL)
```

```python
import functools
import math

import numpy as np
import jax
import jax.numpy as jnp
from jax import lax
from jax.experimental import pallas as pl
from jax.experimental.pallas import tpu as pltpu

F32 = jnp.float32
BF16 = jnp.bfloat16

D_MODEL = 2048
GRID_W = 64
NORM_EPS = 1e-6
N_BRANCH = 4
BRANCH_W = D_MODEL // N_BRANCH
FNO_GROUPS = 4
FNO_GROUP_W = BRANCH_W // FNO_GROUPS
RWKV_HEAD = 64
RWKV_HEADS = BRANCH_W // RWKV_HEAD
N_DIR = 2
DECAY_LORA = 64
AICL_LORA = 64
GATE_LORA = 128
DIR_LORA_W = DECAY_LORA + AICL_LORA
SHIFT_W = 3 * BRANCH_W + DIR_LORA_W
GN_EPS = 64e-5
ATT_HEAD = 64
ATT_HEADS = BRANCH_W // ATT_HEAD
ATT_KV_HEADS = 2
ATT_REP = ATT_HEADS // ATT_KV_HEADS
ATT_KV_W = ATT_KV_HEADS * ATT_HEAD
WINDOW = 128
BLOCK = 128
ROPE_BASE = 10000.0
NEG_INF = -1e30
CONV_K = 31
CONV_PAD = (CONV_K - 1) // 2
LN_EPS = 1e-5
MLP_HIDDEN = 4 * D_MODEL

O_LORA = 3 * BRANCH_W
O_KV = O_LORA + N_DIR * DIR_LORA_W
CTX_STATE_COLS = O_KV + 2 * ATT_KV_W
O_G = CTX_STATE_COLS
O_Q = O_G + GATE_LORA
O_FNO = O_Q + BRANCH_W
O_CONV = O_FNO + BRANCH_W
O_GATE = O_CONV + 2 * BRANCH_W
IN_W = O_GATE + N_BRANCH * D_MODEL

LANES = 128
ROW_TILE = 256
SCAN_CHUNK = 64
PAIR_W = 2 * RWKV_HEAD
N_PAIRS = BRANCH_W // PAIR_W
HALO = 16
ATT_SLAB = 64
SUBLANES = 8
MXU_COLS = 256
CAST_ROWS = 256
CONV_ROWS = 32
VMEM_LIMIT = 56 * 1024 * 1024

NT_DIMS = (((1,), (1,)), ((), ()))
NN_DIMS = (((1,), (0,)), ((), ()))
TN_DIMS = (((0,), (0,)), ((), ()))


def _params(sem):
    return pltpu.CompilerParams(dimension_semantics=sem, vmem_limit_bytes=VMEM_LIMIT)


def _split2(a):
    hi = a.astype(BF16)
    lo = (a - hi.astype(F32)).astype(BF16)
    return hi, lo


def _dot3(a, b, dims=NN_DIMS):
    ah, al = _split2(a)
    bh, bl = _split2(b)
    dg = functools.partial(lax.dot_general, dimension_numbers=dims, preferred_element_type=F32)
    return dg(ah, bh) + (dg(ah, bl) + dg(al, bh))


def _dot_sel(a, sel_bf16):
    hi, lo = _split2(a)
    n = a.shape[0]
    both = jnp.dot(jnp.concatenate([hi, lo], axis=0), sel_bf16, preferred_element_type=F32)
    return both[:n] + both[n:]


def _sigmoid(x):
    return 1.0 / (1.0 + jnp.exp(-x))


def _ada_kernel(a_ref, w_ref, b_ref, o_ref):
    a = a_ref[...]
    s = a * _sigmoid(a)
    o_ref[...] = jnp.dot(s, w_ref[0], preferred_element_type=F32,
                         precision=lax.Precision.HIGHEST) + b_ref[...]


def _ada_mod(cond, w, layer, b):
    n = w.shape[2]
    tn = 1024
    return pl.pallas_call(
        _ada_kernel,
        grid=(n // tn,),
        in_specs=[pl.BlockSpec((8, D_MODEL), lambda j: (0, 0)),
                  pl.BlockSpec((1, D_MODEL, tn), lambda j: (layer, 0, j)),
                  pl.BlockSpec((1, tn), lambda j: (0, j))],
        out_specs=pl.BlockSpec((8, tn), lambda j: (0, j)),
        out_shape=jax.ShapeDtypeStruct((8, n), F32),
        compiler_params=_params(("parallel",)),
        name="ada_mod",
    )(cond, w, b.reshape(1, n))


class _Geom:
    def __init__(self, batch, ctx_len, seq):
        assert ctx_len % ROW_TILE == 0 and seq % ROW_TILE == 0
        assert seq % GRID_W == 0 and seq % BLOCK == 0 and ctx_len % BLOCK == 0
        self.batch = batch
        self.ctx_len = ctx_len
        self.seq = seq
        self.lt = ctx_len + seq
        self.m = batch * self.lt
        self.tiles_b = self.lt // ROW_TILE
        self.ctx_tiles = ctx_len // ROW_TILE
        self.tiles = batch * self.tiles_b

    def mod_row(self, i):
        return 2 * (i // self.tiles_b) + ((i % self.tiles_b) >= self.ctx_tiles).astype(jnp.int32)


def _norm_kernel(x_ref, g_ref, *rest, rows):
    x = x_ref[...]
    y = x * lax.rsqrt(jnp.mean(x * x, axis=-1, keepdims=True) + NORM_EPS) * g_ref[...]
    if rows is None:
        (o_ref,) = rest
    else:
        mod_ref, o_ref = rest
        mod = mod_ref[0]
        y = y * (1.0 + mod[rows[1]:rows[1] + 1]) + mod[rows[0]:rows[0] + 1]
    o_ref[...] = y.astype(o_ref.dtype)


def _norm_mod(geom, x, g, modtab, rows):
    return pl.pallas_call(
        functools.partial(_norm_kernel, rows=rows),
        grid=(geom.tiles,),
        in_specs=[pl.BlockSpec((ROW_TILE, D_MODEL), lambda i: (i, 0)),
                  pl.BlockSpec((1, D_MODEL), lambda i: (0, 0)),
                  pl.BlockSpec((1, 6, D_MODEL), lambda i: (geom.mod_row(i), 0, 0))],
        out_specs=pl.BlockSpec((ROW_TILE, D_MODEL), lambda i: (i, 0)),
        out_shape=jax.ShapeDtypeStruct((geom.m, D_MODEL), BF16),
        compiler_params=_params(("parallel",)),
        name="norm_mod",
    )(x, g.reshape(1, D_MODEL), modtab)


def _final_norm(geom, x, g):
    return pl.pallas_call(
        functools.partial(_norm_kernel, rows=None),
        grid=(geom.batch * geom.seq // ROW_TILE,),
        in_specs=[pl.BlockSpec((ROW_TILE, D_MODEL), lambda i: (i, 0)),
                  pl.BlockSpec((1, D_MODEL), lambda i: (0, 0))],
        out_specs=pl.BlockSpec((ROW_TILE, D_MODEL), lambda i: (i, 0)),
        out_shape=jax.ShapeDtypeStruct((geom.batch * geom.seq, D_MODEL), F32),
        compiler_params=_params(("parallel",)),
        name="final_norm",
    )(x, g.reshape(1, D_MODEL))


def _mm_store(acc, res_ref, mod_refs, o_ref, epilogue, gate_row, norm=None, cs=slice(None)):
    if epilogue == "resid":
        for s, mod_ref in enumerate(mod_refs):
            rs = slice(s * ROW_TILE, (s + 1) * ROW_TILE)
            xn = res_ref[rs, cs] + mod_ref[0, gate_row:gate_row + 1, cs] * acc[rs]
            o_ref[rs, cs] = xn
            if norm is not None:
                g_ref, nmod_refs, h_ref, rows = norm
                nmod = nmod_refs[s][0]
                y = xn * lax.rsqrt(jnp.mean(xn * xn, axis=-1, keepdims=True) + NORM_EPS) * g_ref[...]
                h_ref[rs] = (y * (1.0 + nmod[rows[1]:rows[1] + 1]) + nmod[rows[0]:rows[0] + 1]).astype(h_ref.dtype)
        return
    if epilogue == "sigmoid":
        acc = _sigmoid(acc)
    elif epilogue == "relu2":
        acc = jnp.square(jnp.maximum(acc, 0.0))
    o_ref[:, cs] = acc.astype(o_ref.dtype)


def _col_chunks(tn, norm):
    if norm is not None or tn % MXU_COLS:
        return [slice(None)]
    return [slice(c, c + MXU_COLS) for c in range(0, tn, MXU_COLS)]


def _mm_kernel(a_ref, b_ref, *rest, epilogue, gate_row, nk, n_sub, norm_rows):
    rest = list(rest)
    res_ref, mod_refs, norm = None, (), None
    if epilogue == "resid":
        res_ref, mod_refs, rest = rest[0], rest[1:1 + n_sub], rest[1 + n_sub:]
        if norm_rows is not None:
            norm = (rest[0], rest[1:1 + n_sub], rest[2 + n_sub], norm_rows)
            rest = [rest[1 + n_sub]] + rest[3 + n_sub:]
    o_ref = rest[0]
    chunks = _col_chunks(o_ref.shape[1], norm)
    if nk == 1:
        for cs in chunks:
            acc = jnp.dot(a_ref[...], b_ref[:, cs], preferred_element_type=F32)
            _mm_store(acc, res_ref, mod_refs, o_ref, epilogue, gate_row, norm, cs)
        return
    acc_ref = rest[1]
    k = pl.program_id(2)

    @pl.when(k == 0)
    def _():
        acc_ref[...] = jnp.dot(a_ref[...], b_ref[...], preferred_element_type=F32)

    @pl.when((k > 0) & (k < nk - 1))
    def _():
        acc_ref[...] += jnp.dot(a_ref[...], b_ref[...], preferred_element_type=F32)

    @pl.when(k == nk - 1)
    def _():
        for cs in chunks:
            acc = acc_ref[:, cs] + jnp.dot(a_ref[...], b_ref[:, cs], preferred_element_type=F32)
            _mm_store(acc, res_ref, mod_refs, o_ref, epilogue, gate_row, norm, cs)


def _mm_w32_kernel(a_ref, w_ref, *rest, epilogue, gate_row):
    res_ref, mod_refs = (rest[0], rest[1:-2]) if epilogue == "resid" else (None, ())
    o_ref, wb_ref = rest[-2:]

    @pl.when(pl.program_id(1) == 0)
    def _():
        def cast_rows(r, carry):
            rows = pl.ds(pl.multiple_of(r * CAST_ROWS, CAST_ROWS), CAST_ROWS)
            wb_ref[rows, :] = w_ref[0, rows, :].astype(BF16)
            return carry

        lax.fori_loop(0, wb_ref.shape[0] // CAST_ROWS, cast_rows, 0)

    for cs in _col_chunks(o_ref.shape[1], None):
        acc = jnp.dot(a_ref[...], wb_ref[:, cs], preferred_element_type=F32)
        _mm_store(acc, res_ref, mod_refs, o_ref, epilogue, gate_row, cs=cs)


def _pick_tile(n, cap, unit=LANES):
    t = (min(cap, n) // unit) * unit
    while n % t:
        t -= unit
    return t


class _RowPlan:
    def __init__(self, geom, tm_cap, kind="all", rows=None):
        if kind in ("latent", "packed"):
            self.tm = _pick_tile(geom.seq, tm_cap, ROW_TILE)
            per_b = geom.seq // self.tm
            self.n = geom.batch * per_b
            self.rows = geom.m if kind == "latent" else geom.batch * geom.seq
            if kind == "latent":
                self.off = lambda i: (i // per_b) * geom.lt + geom.ctx_len + (i % per_b) * self.tm
                self.mod_row = geom.mod_row
            else:
                self.off = lambda i: i * self.tm
                self.mod_row = lambda t: 2 * (t // (geom.seq // ROW_TILE)) + 1
        else:
            self.rows = geom.m if kind == "all" else rows
            self.tm = _pick_tile(self.rows, tm_cap, ROW_TILE)
            self.n = self.rows // self.tm
            self.off = lambda i: i * self.tm
            self.mod_row = geom.mod_row if kind == "all" else None
        self.n_sub = self.tm // ROW_TILE

    def spec(self, width, at):
        def index(*g):
            i, col = at(*g)
            col = col if isinstance(col, int) else pl.multiple_of(col, LANES)
            return pl.multiple_of(self.off(i), ROW_TILE), col
        return pl.BlockSpec((pl.Element(self.tm), pl.Element(width)), index)

    def mod_specs(self, tn, at):
        return [pl.BlockSpec((1, 6, tn), lambda *g, s=s: (
            self.mod_row(self.off(at(*g)[0]) // ROW_TILE + s), 0, at(*g)[1])) for s in range(self.n_sub)]


def _matmul(plan, a, b, out_dtype, epilogue="none", res=None, modtab=None, gate_row=0, norm=None,
            res_plan=None, out_plan=None, tn_cap=1024, tk_cap=2048, name="matmul"):
    kdim = a.shape[1]
    n = b.shape[1]
    out_plan = out_plan or plan
    res_plan = res_plan or out_plan
    tm, m = plan.tm, out_plan.rows
    assert (out_plan.tm, out_plan.n, res_plan.tm, res_plan.n) == (tm, plan.n, tm, plan.n)
    tn = n if norm is not None else _pick_tile(n, tn_cap)
    tk = _pick_tile(kdim, tk_cap)
    nk = kdim // tk
    row_j = lambda i, j, k: (i, j * tn)
    tile_ij = lambda i, j, k: (i, j)
    in_specs = [plan.spec(tk, lambda i, j, k: (i, k * tk)),
                pl.BlockSpec((tk, tn), lambda i, j, k: (k, j))]
    args = [a, b]
    out_specs = out_plan.spec(tn, row_j)
    out_shape = jax.ShapeDtypeStruct((m, n), out_dtype)
    if epilogue == "resid":
        in_specs += [res_plan.spec(tn, row_j)] + out_plan.mod_specs(tn, tile_ij)
        args += [res] + [modtab] * plan.n_sub
        if norm is not None:
            gain, nmodtab, _ = norm
            in_specs += [pl.BlockSpec((1, tn), lambda i, j, k: (0, 0))] + out_plan.mod_specs(tn, tile_ij)
            args += [gain.reshape(1, n)] + [nmodtab] * plan.n_sub
            out_specs = [out_specs, out_plan.spec(tn, row_j)]
            out_shape = [out_shape, jax.ShapeDtypeStruct((m, n), BF16)]
    return pl.pallas_call(
        functools.partial(_mm_kernel, epilogue=epilogue, gate_row=gate_row, nk=nk, n_sub=plan.n_sub,
                          norm_rows=None if norm is None else norm[2]),
        grid=(plan.n, n // tn, nk),
        in_specs=in_specs,
        out_specs=out_specs,
        out_shape=out_shape,
        scratch_shapes=[pltpu.VMEM((tm, tn), F32)] if nk > 1 else [],
        compiler_params=_params(("parallel", "parallel", "arbitrary")),
        name=name,
    )(*args)


def _matmul_w32(plan, a, w, layer, out_dtype, epilogue="none", cols=None, out_plan=None, tn_cap=1024,
                name="matmul_w32"):
    kdim = a.shape[1]
    col0, n = cols if cols is not None else (0, w.shape[2])
    assert col0 % LANES == 0 and epilogue != "resid"
    out_plan = out_plan or plan
    assert (out_plan.tm, out_plan.n) == (plan.tm, plan.n)
    tn = _pick_tile(n, tn_cap)
    in_specs = [plan.spec(kdim, lambda j, i: (i, 0)),
                pl.BlockSpec((pl.Element(1), pl.Element(kdim), pl.Element(tn)),
                             lambda j, i: (layer, 0, pl.multiple_of(col0 + j * tn, LANES)))]
    args = [a, w]
    return pl.pallas_call(
        functools.partial(_mm_w32_kernel, epilogue=epilogue, gate_row=0),
        grid=(n // tn, plan.n),
        in_specs=in_specs,
        out_specs=out_plan.spec(tn, lambda j, i: (i, j * tn)),
        out_shape=jax.ShapeDtypeStruct((out_plan.rows, n), out_dtype),
        scratch_shapes=[pltpu.VMEM((kdim, tn), BF16)],
        compiler_params=_params(("parallel", "arbitrary")),
        name=name,
    )(*args)


def _head_sum_matrix(width, head):
    idx = np.arange(width) // head
    return jnp.asarray((idx[:, None] == idx[None, :]).astype(np.float32), dtype=BF16)


def _prep_kernel(p_ref, lora_ref, halo_ref, hlora_ref, mu_ref, w0_ref, wup_ref, a0_ref, aup_ref,
                 kk_ref, ka_ref, hs_ref, r_o, lw_o, k_o, v_o, kkn_o, b_o, *, tiles_b, ctx_tiles):
    d = pl.program_id(0)
    j = pl.program_id(1) % tiles_b
    fwd = d == 0
    f = jnp.concatenate([p_ref[...], lora_ref[...]], axis=-1)
    t = f.shape[0]
    halo = jnp.concatenate([halo_ref[...], hlora_ref[...]], axis=-1)
    at_start = (j == 0) | (j == ctx_tiles)
    at_end = (j == ctx_tiles - 1) | (j == tiles_b - 1)
    edge = jnp.where(fwd, halo[7:8], halo[0:1])
    edge = jnp.where((fwd & at_start) | (jnp.logical_not(fwd) & at_end), 0.0, edge)
    row = lax.broadcasted_iota(jnp.int32, (t, 1), 0)
    prev = jnp.where(row == 0, edge, pltpu.roll(f, 1, 0))
    nxt = jnp.where(row == t - 1, edge, pltpu.roll(f, t - 1, 0))
    f = f + mu_ref[0] * (jnp.where(fwd, prev, nxt) - f)

    r = f[:, 0:BRANCH_W]
    k = f[:, BRANCH_W:2 * BRANCH_W]
    v = f[:, 2 * BRANCH_W:3 * BRANCH_W]
    wl = f[:, 3 * BRANCH_W:3 * BRANCH_W + DECAY_LORA]
    al = f[:, 3 * BRANCH_W + DECAY_LORA:SHIFT_W]
    w_raw = w0_ref[0] + _dot3(jnp.tanh(wl), wup_ref[0])
    lw = -math.exp(-0.5) * _sigmoid(w_raw)
    a = _sigmoid(a0_ref[0] + _dot3(al, aup_ref[0]))
    kk = k * kk_ref[...]
    norm = jnp.sqrt(_dot_sel(kk * kk, hs_ref[...]))
    kk = kk / jnp.maximum(norm, 1e-12)
    r_o[0] = r
    lw_o[0] = lw
    k_o[0] = k * (1.0 + (a - 1.0) * ka_ref[...])
    v_o[0] = v
    kkn_o[0] = kk
    b_o[0] = kk * a


def _rwkv_prep(geom, p1, lp):
    m = geom.m
    t = ROW_TILE
    tb = geom.tiles_b
    rkv_w = 3 * BRANCH_W
    lora_blk0 = O_LORA // DIR_LORA_W
    n_blk8 = m // 8

    def halo_idx(d, i):
        before = jnp.maximum(i * (t // 8) - 1, 0)
        after = jnp.minimum((i + 1) * (t // 8), n_blk8 - 1)
        return jnp.where(d == 0, before, after)

    out = jax.ShapeDtypeStruct((N_DIR, m, BRANCH_W), F32)
    ospec = pl.BlockSpec((1, t, BRANCH_W), lambda d, i: (d, i, 0))
    vec = lambda a: a.reshape(1, BRANCH_W)
    dvec = pl.BlockSpec((1, 1, BRANCH_W), lambda d, i: (d, 0, 0))
    return pl.pallas_call(
        functools.partial(_prep_kernel, tiles_b=tb, ctx_tiles=geom.ctx_tiles),
        grid=(N_DIR, geom.tiles),
        in_specs=[pl.BlockSpec((t, rkv_w), lambda d, i: (i, 0)),
                  pl.BlockSpec((t, DIR_LORA_W), lambda d, i: (i, lora_blk0 + d)),
                  pl.BlockSpec((8, rkv_w), lambda d, i: (halo_idx(d, i), 0)),
                  pl.BlockSpec((8, DIR_LORA_W), lambda d, i: (halo_idx(d, i), lora_blk0 + d)),
                  pl.BlockSpec((1, 1, SHIFT_W), lambda d, i: (d, 0, 0)),
                  dvec,
                  pl.BlockSpec((1, DECAY_LORA, BRANCH_W), lambda d, i: (d, 0, 0)),
                  dvec,
                  pl.BlockSpec((1, AICL_LORA, BRANCH_W), lambda d, i: (d, 0, 0)),
                  pl.BlockSpec((1, BRANCH_W), lambda d, i: (0, 0)),
                  pl.BlockSpec((1, BRANCH_W), lambda d, i: (0, 0)),
                  pl.BlockSpec((BRANCH_W, BRANCH_W), lambda d, i: (0, 0))],
        out_specs=[ospec] * 6,
        out_shape=[out] * 6,
        compiler_params=_params(("parallel", "parallel")),
        name="rwkv_prep",
    )(p1, p1, p1, p1, lp["rwkv_mu"].reshape(N_DIR, 1, SHIFT_W),
      lp["rwkv_w0"].reshape(N_DIR, 1, BRANCH_W), lp["rwkv_w_up"],
      lp["rwkv_a0"].reshape(N_DIR, 1, BRANCH_W), lp["rwkv_a_up"],
      vec(lp["rwkv_k_k"]), vec(lp["rwkv_k_a"]), _head_sum_matrix(BRANCH_W, RWKV_HEAD))


def _scan_kernel(*refs):
    (r_f, r_b, lw_f, lw_b, k_f, k_b, v_f, v_b, kk_f, kk_b, b_f, b_b, yf_ref, yb_ref, h_ref) = refs
    c = pl.program_id(1)
    C = SCAN_CHUNK
    W = 2 * C

    @pl.when(c == 0)
    def _():
        h_ref[...] = jnp.zeros_like(h_ref)

    rr = lax.broadcasted_iota(jnp.int32, (C, C), 0)
    cc = lax.broadcasted_iota(jnp.int32, (C, C), 1)
    lane = lax.broadcasted_iota(jnp.int32, (1, PAIR_W), 1)
    m_a = (lane < RWKV_HEAD).astype(F32)
    m_b = 1.0 - m_a
    r2 = lax.broadcasted_iota(jnp.int32, (W, W), 0)
    c2 = lax.broadcasted_iota(jnp.int32, (W, W), 1)
    same = (r2 // C) == (c2 // C)
    eye = (r2 == c2).astype(F32)

    def pairs(x):
        return [jnp.concatenate([x[:, p * PAIR_W:(p + 1) * PAIR_W] * m_a,
                                 x[:, p * PAIR_W:(p + 1) * PAIR_W] * m_b], axis=0) for p in range(N_PAIRS)]

    stacks = {name: [] for name in ("a", "b", "k", "r", "v", "bc", "kc", "pt")}
    strict, incl = [], []
    for sgn, (r_ref, lw_ref, k_ref, v_ref, kk_ref, b_ref) in (
            (1, (r_f, lw_f, k_f, v_f, kk_f, b_f)), (-1, (r_b, lw_b, k_b, v_b, kk_b, b_b))):
        lw = lw_ref[0]
        tri = jnp.where((rr - cc) * sgn >= 0, 1.0, 0.0).astype(BF16)
        lp_in = _dot_sel_lhs(tri, lw)
        tot = jnp.sum(lw, axis=0, keepdims=True)
        e_neg = jnp.exp(-lp_in)
        e_chk = jnp.exp(tot - lp_in)
        p_tot = jnp.exp(tot)
        stacks["a"] += pairs(-kk_ref[0] * jnp.exp(lp_in - lw))
        stacks["b"] += pairs(b_ref[0] * e_neg)
        stacks["k"] += pairs(k_ref[0] * e_neg)
        stacks["r"] += pairs(r_ref[0] * jnp.exp(lp_in))
        stacks["v"] += pairs(v_ref[0])
        stacks["bc"] += pairs(b_ref[0] * e_chk)
        stacks["kc"] += pairs(k_ref[0] * e_chk)
        stacks["pt"] += [p_tot[:, p * PAIR_W:(p + 1) * PAIR_W] for p in range(N_PAIRS)]
        dt = (r2 % C - c2 % C) * sgn
        strict += [(same & (dt > 0)).astype(F32)] * N_PAIRS
        incl += [(same & (dt >= 0)).astype(F32)] * N_PAIRS
    a_s, b_s, k_s, r_s, v_s, bc_s, kc_s, p_tot = (jnp.stack(stacks[n]) for n in
                                                  ("a", "b", "k", "r", "v", "bc", "kc", "pt"))
    strict = jnp.stack(strict)
    incl = jnp.stack(incl)

    a_b, b_b, k_b, r_b, v_b, bc_b, kc_b = (x.astype(BF16) for x in (a_s, b_s, k_s, r_s, v_s, bc_s, kc_s))
    big = _bdot(jnp.concatenate([a_b, r_b], axis=1), jnp.concatenate([b_b, k_b], axis=1), BNT_DIMS)
    l_ab = big[:, :W, :W] * strict
    l_ak = big[:, :W, W:] * strict
    m_rb = (big[:, W:, :W] * incl).astype(BF16)
    m_rk = big[:, W:, W:] * incl
    t_inv = eye + l_ab
    pw_b = l_ab.astype(BF16)
    pw_b = _bdot(pw_b, pw_b).astype(BF16)
    for _ in range(int(math.log2(C)) - 2):
        both = _bdot(jnp.concatenate([t_inv.astype(BF16), pw_b], axis=1), pw_b)
        t_inv = t_inv + both[:, :W]
        pw_b = both[:, W:].astype(BF16)
    t_inv = t_inv + _bdot(t_inv, pw_b)
    x1 = _bdot(t_inv, jnp.concatenate([a_b, _bdot(l_ak, v_b).astype(BF16)], axis=2)).astype(BF16)
    x2 = _bdot(m_rb, x1)
    r_hat = r_s + x2[:, :, :PAIR_W]
    y0 = x2[:, :, PAIR_W:] + _bdot(m_rk, v_b)
    x3 = _bdot(bc_b, x1, BTN_DIMS)
    g = eye * p_tot + x3[:, :, :PAIR_W]
    h_inc = x3[:, :, PAIR_W:] + _bdot(kc_b, v_b, BTN_DIMS)
    x4 = _bdot(jnp.concatenate([r_hat, g], axis=1), h_ref[...])
    ys = x4[:, :W] + y0
    h_ref[...] = x4[:, W:] + h_inc
    for d, y_ref in enumerate((yf_ref, yb_ref)):
        for p in range(N_PAIRS):
            y_ref[:, p * PAIR_W:(p + 1) * PAIR_W] = ys[d * N_PAIRS + p, :C] + ys[d * N_PAIRS + p, C:]


BNN_DIMS = (((2,), (1,)), ((0,), (0,)))
BNT_DIMS = (((2,), (2,)), ((0,), (0,)))
BTN_DIMS = (((1,), (1,)), ((0,), (0,)))


def _bdot(a, b, dims=BNN_DIMS):
    return lax.dot_general(a.astype(BF16), b.astype(BF16), dims, preferred_element_type=F32)


def _dot_sel_lhs(sel_bf16, a):
    hi = a.astype(BF16)
    r1 = a - hi.astype(F32)
    mid = r1.astype(BF16)
    lo = (r1 - mid.astype(F32)).astype(BF16)
    dg = functools.partial(jnp.dot, preferred_element_type=F32)
    return dg(sel_bf16, hi) + (dg(sel_bf16, mid) + dg(sel_bf16, lo))


def _rwkv_scan(geom, ins):
    C = SCAN_CHUNK
    nch = geom.lt // C
    nctx = geom.ctx_len // C

    def rev(c):
        return jnp.where(c < nctx, nctx - 1 - c, nch - 1 + nctx - c)

    fwd = pl.BlockSpec((1, C, BRANCH_W), lambda b, c: (0, b * nch + c, 0))
    bwd = pl.BlockSpec((1, C, BRANCH_W), lambda b, c: (1, b * nch + rev(c), 0))
    out = jax.ShapeDtypeStruct((geom.m, BRANCH_W), F32)
    return pl.pallas_call(
        _scan_kernel,
        grid=(geom.batch, nch),
        in_specs=[fwd, bwd] * 6,
        out_specs=[pl.BlockSpec((C, BRANCH_W), lambda b, c: (b * nch + c, 0)),
                   pl.BlockSpec((C, BRANCH_W), lambda b, c: (b * nch + rev(c), 0))],
        out_shape=[out, out],
        scratch_shapes=[pltpu.VMEM((N_DIR * N_PAIRS, PAIR_W, PAIR_W), F32)],
        compiler_params=_params(("parallel", "arbitrary")),
        name="rwkv_scan",
    )(*[a for a in ins for _ in range(N_DIR)])


def _readout_kernel(yf_ref, yb_ref, r_ref, k_ref, v_ref, p2_ref, gup_ref, rk_ref, lg_ref, lb_ref, hs_ref, o_ref):
    hs = hs_ref[...]
    y = yf_ref[...] + yb_ref[...]
    inv_n = 1.0 / RWKV_HEAD
    mean = _dot_sel(y, hs) * inv_n
    yc = y - mean
    var = _dot_sel(yc * yc, hs) * inv_n
    yn = yc * lax.rsqrt(var + GN_EPS) * lg_ref[...] + lb_ref[...]
    bonus = jnp.zeros_like(y)
    for d in range(N_DIR):
        bonus = bonus + _dot_sel(r_ref[d] * k_ref[d] * rk_ref[d:d + 1], hs) * v_ref[d]
    g = _dot3(_sigmoid(p2_ref[:, 0:GATE_LORA]), gup_ref[...])
    o_ref[...] = ((yn + bonus) * g).astype(o_ref.dtype)


def _rwkv_readout(geom, y, ins, p2, lp):
    t = ROW_TILE
    dspec = pl.BlockSpec((N_DIR, t, BRANCH_W), lambda i: (0, i, 0))
    vspec = pl.BlockSpec((1, BRANCH_W), lambda i: (0, 0))
    r, _, k, v, _, _ = ins
    return pl.pallas_call(
        _readout_kernel,
        grid=(geom.tiles,),
        in_specs=[pl.BlockSpec((t, BRANCH_W), lambda i: (i, 0)),
                  pl.BlockSpec((t, BRANCH_W), lambda i: (i, 0)),
                  dspec, dspec, dspec,
                  pl.BlockSpec((t, GATE_LORA + BRANCH_W), lambda i: (i, 0)),
                  pl.BlockSpec((GATE_LORA, BRANCH_W), lambda i: (0, 0)),
                  pl.BlockSpec((N_DIR, BRANCH_W), lambda i: (0, 0)),
                  vspec, vspec,
                  pl.BlockSpec((BRANCH_W, BRANCH_W), lambda i: (0, 0))],
        out_specs=pl.BlockSpec((t, BRANCH_W), lambda i: (i, 0)),
        out_shape=jax.ShapeDtypeStruct((geom.m, BRANCH_W), BF16),
        compiler_params=_params(("parallel",)),
        name="rwkv_readout",
    )(y[0], y[1], r, k, v, p2, lp["rwkv_g_up"], lp["rwkv_r_k"].reshape(N_DIR, BRANCH_W),
      lp["rwkv_lnx_g"].reshape(1, BRANCH_W), lp["rwkv_lnx_b"].reshape(1, BRANCH_W),
      _head_sum_matrix(BRANCH_W, RWKV_HEAD))


def _rope_tables(geom):
    half = ATT_HEAD // 2
    nf = half // 2
    inv = ROPE_BASE ** (-jnp.arange(nf, dtype=F32) / nf)
    pos = jnp.arange(geom.seq, dtype=jnp.int32)
    row_ang = (pos // GRID_W).astype(F32)[:, None] * inv[None, :]
    col_ang = (pos % GRID_W).astype(F32)[:, None] * inv[None, :]
    cos = jnp.concatenate([jnp.cos(row_ang)] * 2 + [jnp.cos(col_ang)] * 2, axis=-1)
    sin = jnp.concatenate([-jnp.sin(row_ang), jnp.sin(row_ang), -jnp.sin(col_ang), jnp.sin(col_ang)], axis=-1)
    cos = jnp.concatenate([jnp.ones((geom.ctx_len, ATT_HEAD), F32), cos], axis=0)
    sin = jnp.concatenate([jnp.zeros((geom.ctx_len, ATT_HEAD), F32), sin], axis=0)
    return cos, sin


def _rotate(t, cos, sin):
    w = t.shape[-1]
    nf = ATT_HEAD // 4
    lane = lax.broadcasted_iota(jnp.int32, (1, w), 1)
    partner = jnp.where((lane % (2 * nf)) < nf, pltpu.roll(t, w - nf, 1), pltpu.roll(t, nf, 1))
    return t * cos + partner * sin


def _rope_kernel(p2_ref, kv_ref, cq_ref, sq_ref, ck_ref, sk_ref, q_o, k_o, v_o):
    q = p2_ref[:, GATE_LORA:]
    q_o[...] = (_rotate(q, cq_ref[...], sq_ref[...]) * (ATT_HEAD ** -0.5)).astype(q_o.dtype)
    kv = kv_ref[...]
    k_o[...] = _rotate(kv[:, :ATT_KV_W], ck_ref[...], sk_ref[...]).astype(k_o.dtype)
    v_o[...] = kv[:, ATT_KV_W:].astype(v_o.dtype)


def _rope(geom, p1, p2):
    t = ROW_TILE
    tb = geom.tiles_b
    cos, sin = _rope_tables(geom)
    cq, sq = jnp.tile(cos, (1, ATT_HEADS)), jnp.tile(sin, (1, ATT_HEADS))
    ck, sk = jnp.tile(cos, (1, ATT_KV_HEADS)), jnp.tile(sin, (1, ATT_KV_HEADS))
    qspec = pl.BlockSpec((t, BRANCH_W), lambda i: (i % tb, 0))
    kspec = pl.BlockSpec((t, ATT_KV_W), lambda i: (i % tb, 0))
    m = geom.m
    return pl.pallas_call(
        _rope_kernel,
        grid=(geom.tiles,),
        in_specs=[pl.BlockSpec((t, GATE_LORA + BRANCH_W), lambda i: (i, 0)),
                  pl.BlockSpec((t, 2 * ATT_KV_W), lambda i: (i, O_KV // (2 * ATT_KV_W))),
                  qspec, qspec, kspec, kspec],
        out_specs=[pl.BlockSpec((t, BRANCH_W), lambda i: (i, 0)),
                   pl.BlockSpec((t, ATT_KV_W), lambda i: (i, 0)),
                   pl.BlockSpec((t, ATT_KV_W), lambda i: (i, 0))],
        out_shape=[jax.ShapeDtypeStruct((m, BRANCH_W), BF16),
                   jax.ShapeDtypeStruct((m, ATT_KV_W), BF16),
                   jax.ShapeDtypeStruct((m, ATT_KV_W), BF16)],
        compiler_params=_params(("parallel",)),
        name="rope",
    )(p2, p1, cq, sq, ck, sk)


def _attn_kernel(q_ref, kc_ref, vc_ref, kp_ref, ko_ref, kn_ref, vp_ref, vo_ref, vn_ref, sink_ref, o_ref,
                 s_ref, p_ref, *, ctx_blocks, blocks_b):
    j = pl.program_id(1)
    q = q_ref[...]
    k_all = jnp.concatenate([kp_ref[...], ko_ref[...], kn_ref[...], kc_ref[...]], axis=0)
    v_all = jnp.concatenate([vp_ref[...], vo_ref[...], vn_ref[...], vc_ref[...]], axis=0)
    nloc = 3 * BLOCK
    slab = ATT_SLAB
    qi0 = lax.broadcasted_iota(jnp.int32, (slab, nloc), 0)
    ki = lax.broadcasted_iota(jnp.int32, (slab, nloc), 1)
    never = 4 * BLOCK
    prev_off = jnp.where(j > ctx_blocks, 0, never)
    own_hi = jnp.where(j >= ctx_blocks, 2 * BLOCK, BLOCK)
    next_off = 2 * BLOCK - jnp.where((j >= ctx_blocks) & (j < blocks_b - 1), 0, never)
    outs = []
    for g in range(ATT_KV_HEADS):
        gs = slice(g * ATT_HEAD, (g + 1) * ATT_HEAD)
        qg = jnp.concatenate([q[:, (g * ATT_REP + h) * ATT_HEAD:(g * ATT_REP + h + 1) * ATT_HEAD]
                              for h in range(ATT_REP)], axis=0)
        s_ref[g] = lax.dot_general(qg, k_all[:, gs], NT_DIMS, preferred_element_type=F32)
        dens = []
        for blk in range(ATT_REP * BLOCK // slab):
            rs = slice(blk * slab, (blk + 1) * slab)
            head = g * ATT_REP + blk * slab // BLOCK
            qi = qi0 + (blk * slab) % BLOCK
            valid = (((ki < BLOCK) & (ki >= qi + prev_off)) | ((ki >= BLOCK) & (ki < own_hi))
                     | ((ki >= 2 * BLOCK) & (ki <= qi + next_off)))
            s_loc = jnp.where(valid, s_ref[g, rs, :nloc], NEG_INF)
            s_ctx = s_ref[g, rs, nloc:]
            sink = sink_ref[head:head + 1, 0:1]
            mx = jnp.maximum(jnp.maximum(jnp.max(s_loc, axis=-1, keepdims=True),
                                         jnp.max(s_ctx, axis=-1, keepdims=True)), sink)
            e_loc = jnp.exp(s_loc - mx)
            e_ctx = jnp.exp(s_ctx - mx)
            dens.append(jnp.sum(e_loc, axis=-1, keepdims=True) + jnp.sum(e_ctx, axis=-1, keepdims=True)
                        + jnp.exp(sink - mx))
            p_ref[g, rs, :nloc] = e_loc.astype(BF16)
            p_ref[g, rs, nloc:] = e_ctx.astype(BF16)
        o = jnp.dot(p_ref[g], v_all[:, gs], preferred_element_type=F32) / jnp.concatenate(dens, axis=0)
        outs += [o[h * BLOCK:(h + 1) * BLOCK] for h in range(ATT_REP)]
    o_ref[...] = jnp.concatenate(outs, axis=-1).astype(o_ref.dtype)


def _attention(geom, q, k, v, sink):
    nb = geom.lt // BLOCK
    cb = geom.ctx_len // BLOCK
    row = lambda b, j: (b * nb + j, 0)
    prev = lambda b, j: (b * nb + jnp.maximum(j - 1, 0), 0)
    nxt = lambda b, j: (b * nb + jnp.minimum(j + 1, nb - 1), 0)
    ctx = lambda b, j: (b * (geom.lt // geom.ctx_len), 0)
    assert geom.lt % geom.ctx_len == 0
    kvs = lambda f: pl.BlockSpec((BLOCK, ATT_KV_W), f)
    cspec = pl.BlockSpec((geom.ctx_len, ATT_KV_W), ctx)
    sink_tab = jnp.broadcast_to(sink.astype(F32)[:, None], (ATT_HEADS, LANES))
    return pl.pallas_call(
        functools.partial(_attn_kernel, ctx_blocks=cb, blocks_b=nb),
        grid=(geom.batch, nb),
        in_specs=[pl.BlockSpec((BLOCK, BRANCH_W), row), cspec, cspec,
                  kvs(prev), kvs(row), kvs(nxt), kvs(prev), kvs(row), kvs(nxt),
                  pl.BlockSpec((ATT_HEADS, LANES), lambda b, j: (0, 0))],
        out_specs=pl.BlockSpec((BLOCK, BRANCH_W), row),
        out_shape=jax.ShapeDtypeStruct((geom.m, BRANCH_W), BF16),
        scratch_shapes=[pltpu.VMEM((ATT_KV_HEADS, ATT_REP * BLOCK, 3 * BLOCK + geom.ctx_len), F32),
                        pltpu.VMEM((ATT_KV_HEADS, ATT_REP * BLOCK, 3 * BLOCK + geom.ctx_len), BF16)],
        compiler_params=_params(("parallel", "parallel")),
        name="attention",
    )(q, k, v, k, k, k, v, v, v, sink_tab)


def _conv_kernel(u_ref, up_ref, un_ref, dw_ref, db_ref, lg_ref, lb_ref, o_ref, hp_ref, sh_ref,
                 *, tiles_b, ctx_tiles):
    j = pl.program_id(0) % tiles_b
    t = u_ref.shape[0]

    def glu(u):
        return u[:, :BRANCH_W] * _sigmoid(u[:, BRANCH_W:])

    at_start = (j == 0) | (j == ctx_tiles)
    at_end = (j == ctx_tiles - 1) | (j == tiles_b - 1)
    hp_ref[0:HALO] = jnp.where(at_start, 0.0, glu(up_ref[...]))
    hp_ref[HALO:HALO + t] = glu(u_ref[...])
    hp_ref[HALO + t:] = jnp.where(at_end, 0.0, glu(un_ref[...]))
    span = t + 2 * HALO - SUBLANES
    for r in range(1, SUBLANES):
        sh_ref[r, 0:span] = hp_ref[pl.ds(r, span)]
    for chunk in range(t // CONV_ROWS):
        acc = jnp.zeros((CONV_ROWS, BRANCH_W), F32) + db_ref[...]
        for tap in range(CONV_K):
            q, r = divmod(HALO - CONV_PAD + tap, SUBLANES)
            rows = pl.ds(chunk * CONV_ROWS + q * SUBLANES, CONV_ROWS)
            src = hp_ref[rows] if r == 0 else sh_ref[r, rows]
            acc = acc + src * dw_ref[tap:tap + 1]
        mean = jnp.mean(acc, axis=-1, keepdims=True)
        cen = acc - mean
        var = jnp.mean(cen * cen, axis=-1, keepdims=True)
        h = cen * lax.rsqrt(var + LN_EPS) * lg_ref[...] + lb_ref[...]
        o_ref[chunk * CONV_ROWS:(chunk + 1) * CONV_ROWS] = (h * _sigmoid(h)).astype(o_ref.dtype)


def _conv(geom, p4, lp):
    t = ROW_TILE
    nh = geom.m // HALO
    vspec = pl.BlockSpec((1, BRANCH_W), lambda i: (0, 0))
    return pl.pallas_call(
        functools.partial(_conv_kernel, tiles_b=geom.tiles_b, ctx_tiles=geom.ctx_tiles),
        grid=(geom.tiles,),
        in_specs=[pl.BlockSpec((t, 2 * BRANCH_W), lambda i: (i, 0)),
                  pl.BlockSpec((HALO, 2 * BRANCH_W), lambda i: (jnp.maximum(i * (t // HALO) - 1, 0), 0)),
                  pl.BlockSpec((HALO, 2 * BRANCH_W), lambda i: (jnp.minimum((i + 1) * (t // HALO), nh - 1), 0)),
                  pl.BlockSpec((CONV_K, BRANCH_W), lambda i: (0, 0)),
                  vspec, vspec, vspec],
        out_specs=pl.BlockSpec((t, BRANCH_W), lambda i: (i, 0)),
        out_shape=jax.ShapeDtypeStruct((geom.m, BRANCH_W), BF16),
        scratch_shapes=[pltpu.VMEM((t + 2 * HALO, BRANCH_W), F32),
                        pltpu.VMEM((SUBLANES, t + 2 * HALO, BRANCH_W), F32)],
        compiler_params=_params(("parallel",)),
        name="conformer_conv",
    )(p4, p4, p4, lp["conv_dw"], lp["conv_dw_b"].reshape(1, BRANCH_W),
      lp["conv_ln_g"].reshape(1, BRANCH_W), lp["conv_ln_b"].reshape(1, BRANCH_W))


def _dft_cos_sin(n, scale):
    idx = np.arange(n, dtype=np.int64)
    ang = 2.0 * np.pi * ((idx[:, None] * idx[None, :]) % n).astype(np.float64) / n
    return np.cos(ang) * scale, np.sin(ang) * scale


def _channel_dft():
    c, s = _dft_cos_sin(FNO_GROUP_W, FNO_GROUP_W ** -0.5)
    eye = np.eye(FNO_GROUPS)
    return jnp.asarray(np.concatenate([np.kron(eye, c), np.kron(eye, s)], axis=1), dtype=F32).astype(BF16)


def _dft_pos_kernel(c_ref, s_ref, gc_ref, gs_ref, o_ref):
    o_ref[0] = (jnp.dot(c_ref[...], gc_ref[0], preferred_element_type=F32)
                + jnp.dot(s_ref[...], gs_ref[0], preferred_element_type=F32)).astype(o_ref.dtype)


def _fourier(u):
    bsz, length, _ = u.shape
    gcs = _matmul(_RowPlan(None, 1024, "plain", bsz * length), u.reshape(bsz * length, BRANCH_W), _channel_dft(),
                  BF16, name="dft_channels")
    gcs = gcs.reshape(bsz, length, 2 * BRANCH_W)
    c, s = _dft_cos_sin(length, length ** -0.5)
    tm = _pick_tile(length, 512, ROW_TILE)
    return pl.pallas_call(
        _dft_pos_kernel,
        grid=(length // tm, bsz),
        in_specs=[pl.BlockSpec((tm, length), lambda i, b: (i, 0)),
                  pl.BlockSpec((tm, length), lambda i, b: (i, 0)),
                  pl.BlockSpec((1, length, BRANCH_W), lambda i, b: (b, 0, 0)),
                  pl.BlockSpec((1, length, BRANCH_W), lambda i, b: (b, 0, 1))],
        out_specs=pl.BlockSpec((1, tm, BRANCH_W), lambda i, b: (b, i, 0)),
        out_shape=jax.ShapeDtypeStruct((bsz, length, BRANCH_W), BF16),
        compiler_params=_params(("parallel", "parallel")),
        name="dft_positions",
    )(jnp.asarray(c, dtype=F32).astype(BF16), jnp.asarray(-s, dtype=F32).astype(BF16), gcs, gcs)


def _merge_kernel(f0, f1, f2, f3, w_ref, g0, g1, g2, g3, o_ref):
    acc = None
    for i, (f, g) in enumerate(((f0, g0), (f1, g1), (f2, g2), (f3, g3))):
        term = jnp.dot(f[...], w_ref[i], preferred_element_type=F32) * g[...].astype(F32)
        acc = term if acc is None else acc + term
    o_ref[...] = acc.astype(o_ref.dtype)


def _merge(feat_plan, plan, feats, w_branch, gate):
    assert (feat_plan.tm, feat_plan.n) == (plan.tm, plan.n)
    tn = 1024
    fspec = feat_plan.spec(BRANCH_W, lambda i, j: (i, 0))
    gspec = lambda br: plan.spec(tn, lambda i, j: (i, br * D_MODEL + j * tn))
    return pl.pallas_call(
        _merge_kernel,
        grid=(plan.n, D_MODEL // tn),
        in_specs=[fspec] * 4 + [pl.BlockSpec((N_BRANCH, BRANCH_W, tn), lambda i, j: (0, 0, j))]
        + [gspec(br) for br in range(N_BRANCH)],
        out_specs=plan.spec(tn, lambda i, j: (i, j * tn)),
        out_shape=jax.ShapeDtypeStruct((plan.rows, D_MODEL), BF16),
        compiler_params=_params(("parallel", "parallel")),
        name="branch_merge",
    )(*feats, w_branch, gate, gate, gate, gate)


def _mixer(geom, h, xall, modtab, lp, w_in, layer, norm2_g, latent_only):
    src, dst = ("latent", "packed") if latent_only else ("all", "all")
    rows_all = _RowPlan(geom, 1024)
    proj = lambda lo, hi, dt, name, **kw: _matmul_w32(kw.pop("plan", rows_all), h, w_in, layer, dt,
                                                      cols=(lo, hi - lo), name=name, **kw)
    p1 = proj(0, CTX_STATE_COLS, F32, "in_proj_state")
    p2 = proj(O_G, O_FNO, F32, "in_proj_gq")
    p3 = proj(O_FNO, O_CONV, BF16, "in_proj_fno")
    p4 = proj(O_CONV, O_GATE, F32, "in_proj_conv")
    gate = proj(O_GATE, IN_W, BF16, "in_proj_gate", epilogue="sigmoid", plan=_RowPlan(geom, 1024, src),
                out_plan=_RowPlan(geom, 1024, dst))

    ins = _rwkv_prep(geom, p1, lp)
    y = _rwkv_scan(geom, ins)
    rw = _rwkv_readout(geom, y, ins, p2, lp)

    q, k, v = _rope(geom, p1, p2)
    att = _attention(geom, q, k, v, lp["att_sink"])

    cv = _conv(geom, p4, lp)

    p3 = p3.reshape(geom.batch, geom.lt, BRANCH_W)
    fno = jnp.concatenate([_fourier(p3[:, :geom.ctx_len]), _fourier(p3[:, geom.ctx_len:])], axis=1)
    fno = fno.reshape(geom.m, BRANCH_W)

    half_src, half_dst = _RowPlan(geom, 512, src), _RowPlan(geom, 512, dst)
    mixed = _merge(half_src, half_dst, (fno, rw, att, cv), lp["w_branch"].astype(BF16), gate)
    return _matmul(half_dst, mixed, lp["w_out"].astype(BF16), F32, epilogue="resid", res=xall, res_plan=half_src,
                   modtab=modtab, gate_row=2, norm=(norm2_g, modtab, (3, 4)), name="out_proj")


def kernel(x, c, ctx, c_ctx, ada_w, ada_b, norm1_g, norm2_g, w_in, rwkv_mu, rwkv_w0, rwkv_w_up, rwkv_a0, rwkv_a_up, rwkv_k_k, rwkv_k_a, rwkv_r_k, rwkv_g_up, rwkv_lnx_g, rwkv_lnx_b, att_sink, conv_dw, conv_dw_b, conv_ln_g, conv_ln_b, w_branch, w_out, w_mlp1, w_mlp2, final_g):
    batch, seq, _ = x.shape
    geom = _Geom(batch, ctx.shape[1], seq)
    depth = w_in.shape[0]
    assert batch + 1 <= 8
    cond = jnp.zeros((8, D_MODEL), F32).at[:batch].set(c).at[batch].set(c_ctx)
    xall = jnp.concatenate([ctx, x], axis=1).reshape(geom.m, D_MODEL)
    modtabs = []
    for l in range(depth):
        mod = _ada_mod(cond, ada_w, l, ada_b[l])
        mod_x = mod[:batch].reshape(batch, 1, 6, D_MODEL)
        mod_c = jnp.broadcast_to(mod[batch].reshape(1, 1, 6, D_MODEL), (batch, 1, 6, D_MODEL))
        modtabs.append(jnp.concatenate([mod_c, mod_x], axis=1).reshape(2 * batch, 6, D_MODEL))
    h = _norm_mod(geom, xall, norm1_g[0], modtabs[0], rows=(0, 1))
    for l in range(depth):
        modtab = modtabs[l]
        lp = {
            "rwkv_mu": rwkv_mu[l], "rwkv_w0": rwkv_w0[l], "rwkv_w_up": rwkv_w_up[l],
            "rwkv_a0": rwkv_a0[l], "rwkv_a_up": rwkv_a_up[l], "rwkv_k_k": rwkv_k_k[l],
            "rwkv_k_a": rwkv_k_a[l], "rwkv_r_k": rwkv_r_k[l], "rwkv_g_up": rwkv_g_up[l],
            "rwkv_lnx_g": rwkv_lnx_g[l], "rwkv_lnx_b": rwkv_lnx_b[l], "att_sink": att_sink[l],
            "conv_dw": conv_dw[l], "conv_dw_b": conv_dw_b[l], "conv_ln_g": conv_ln_g[l],
            "conv_ln_b": conv_ln_b[l], "w_branch": w_branch[l], "w_out": w_out[l],
        }
        last = l + 1 == depth
        xall, h2 = _mixer(geom, h, xall, modtab, lp, w_in, l, norm2_g[l], latent_only=last)
        kind = "packed" if last else "all"
        hid = _matmul_w32(_RowPlan(geom, 1024, kind), h2, w_mlp1, l, BF16, epilogue="relu2", name="mlp_up")
        down = functools.partial(_matmul, a=hid, b=w_mlp2[l].astype(BF16), out_dtype=F32, epilogue="resid",
                                 res=xall, modtab=modtab, gate_row=5, name="mlp_down")
        if last:
            xall = down(_RowPlan(geom, 1024, kind))
        else:
            xall, h = down(_RowPlan(geom, 512), norm=(norm1_g[l + 1], modtabs[l + 1], (0, 1)))
    return _final_norm(geom, xall, final_g).reshape(batch, seq, D_MODEL)
```

```python
import functools
import math

import numpy as np
import jax
import jax.numpy as jnp
from jax import lax
from jax.experimental import pallas as pl
from jax.experimental.pallas import tpu as pltpu

F32 = jnp.float32
BF16 = jnp.bfloat16

D_MODEL = 2048
GRID_W = 64
NORM_EPS = 1e-6
N_BRANCH = 4
BRANCH_W = D_MODEL // N_BRANCH
FNO_GROUPS = 4
FNO_GROUP_W = BRANCH_W // FNO_GROUPS
RWKV_HEAD = 64
RWKV_HEADS = BRANCH_W // RWKV_HEAD
N_DIR = 2
DECAY_LORA = 64
AICL_LORA = 64
GATE_LORA = 128
DIR_LORA_W = DECAY_LORA + AICL_LORA
SHIFT_W = 3 * BRANCH_W + DIR_LORA_W
GN_EPS = 64e-5
ATT_HEAD = 64
ATT_HEADS = BRANCH_W // ATT_HEAD
ATT_KV_HEADS = 2
ATT_REP = ATT_HEADS // ATT_KV_HEADS
ATT_KV_W = ATT_KV_HEADS * ATT_HEAD
WINDOW = 128
BLOCK = 128
ROPE_BASE = 10000.0
NEG_INF = -1e30
CONV_K = 31
CONV_PAD = (CONV_K - 1) // 2
LN_EPS = 1e-5
MLP_HIDDEN = 4 * D_MODEL

O_LORA = 3 * BRANCH_W
O_KV = O_LORA + N_DIR * DIR_LORA_W
CTX_STATE_COLS = O_KV + 2 * ATT_KV_W
O_G = CTX_STATE_COLS
O_Q = O_G + GATE_LORA
O_FNO = O_Q + BRANCH_W
O_CONV = O_FNO + BRANCH_W
O_GATE = O_CONV + 2 * BRANCH_W
IN_W = O_GATE + N_BRANCH * D_MODEL

LANES = 128
ROW_TILE = 256
SCAN_CHUNK = 64
PAIR_W = 2 * RWKV_HEAD
N_PAIRS = BRANCH_W // PAIR_W
HALO = 16
ATT_SLAB = 64
SUBLANES = 8
MXU_COLS = 256
CAST_ROWS = 256
CONV_ROWS = 32
PREP_ROWS = 64
SCAN_STEPS = 2
VMEM_LIMIT = 56 * 1024 * 1024

NT_DIMS = (((1,), (1,)), ((), ()))
NN_DIMS = (((1,), (0,)), ((), ()))
TN_DIMS = (((0,), (0,)), ((), ()))


def _params(sem):
    return pltpu.CompilerParams(dimension_semantics=sem, vmem_limit_bytes=VMEM_LIMIT)


def _split2(a):
    hi = a.astype(BF16)
    lo = (a - hi.astype(F32)).astype(BF16)
    return hi, lo


def _dot3(a, b, dims=NN_DIMS):
    ah, al = _split2(a)
    bh, bl = _split2(b)
    dg = functools.partial(lax.dot_general, dimension_numbers=dims, preferred_element_type=F32)
    return dg(ah, bh) + (dg(ah, bl) + dg(al, bh))


def _dot_sel(a, sel_bf16):
    hi, lo = _split2(a)
    n = a.shape[0]
    both = jnp.dot(jnp.concatenate([hi, lo], axis=0), sel_bf16, preferred_element_type=F32)
    return both[:n] + both[n:]


def _sigmoid(x):
    return 1.0 / (1.0 + jnp.exp(-x))


def _ada_kernel(a_ref, w_ref, b_ref, o_ref):
    a = a_ref[...]
    s = a * _sigmoid(a)
    o_ref[...] = jnp.dot(s, w_ref[0], preferred_element_type=F32,
                         precision=lax.Precision.HIGHEST) + b_ref[...]


def _ada_mod(cond, w, layer, b):
    n = w.shape[2]
    tn = 1024
    return pl.pallas_call(
        _ada_kernel,
        grid=(n // tn,),
        in_specs=[pl.BlockSpec((8, D_MODEL), lambda j: (0, 0)),
                  pl.BlockSpec((1, D_MODEL, tn), lambda j: (layer, 0, j)),
                  pl.BlockSpec((1, tn), lambda j: (0, j))],
        out_specs=pl.BlockSpec((8, tn), lambda j: (0, j)),
        out_shape=jax.ShapeDtypeStruct((8, n), F32),
        compiler_params=_params(("parallel",)),
        name="ada_mod",
    )(cond, w, b.reshape(1, n))


class _Geom:
    def __init__(self, batch, ctx_len, seq):
        assert ctx_len % ROW_TILE == 0 and seq % ROW_TILE == 0
        assert seq % GRID_W == 0 and seq % BLOCK == 0 and ctx_len % BLOCK == 0
        self.batch = batch
        self.ctx_len = ctx_len
        self.seq = seq
        self.lt = ctx_len + seq
        self.m = batch * self.lt
        self.tiles_b = self.lt // ROW_TILE
        self.ctx_tiles = ctx_len // ROW_TILE
        self.tiles = batch * self.tiles_b

    def mod_row(self, i):
        return 2 * (i // self.tiles_b) + ((i % self.tiles_b) >= self.ctx_tiles).astype(jnp.int32)


def _norm_kernel(x_ref, g_ref, *rest, rows):
    x = x_ref[...]
    y = x * lax.rsqrt(jnp.mean(x * x, axis=-1, keepdims=True) + NORM_EPS) * g_ref[...]
    if rows is None:
        (o_ref,) = rest
    else:
        mod_ref, o_ref = rest
        mod = mod_ref[0]
        y = y * (1.0 + mod[rows[1]:rows[1] + 1]) + mod[rows[0]:rows[0] + 1]
    o_ref[...] = y.astype(o_ref.dtype)


def _norm_mod(geom, x, g, modtab, rows):
    return pl.pallas_call(
        functools.partial(_norm_kernel, rows=rows),
        grid=(geom.tiles,),
        in_specs=[pl.BlockSpec((ROW_TILE, D_MODEL), lambda i: (i, 0)),
                  pl.BlockSpec((1, D_MODEL), lambda i: (0, 0)),
                  pl.BlockSpec((1, 6, D_MODEL), lambda i: (geom.mod_row(i), 0, 0))],
        out_specs=pl.BlockSpec((ROW_TILE, D_MODEL), lambda i: (i, 0)),
        out_shape=jax.ShapeDtypeStruct((geom.m, D_MODEL), BF16),
        compiler_params=_params(("parallel",)),
        name="norm_mod",
    )(x, g.reshape(1, D_MODEL), modtab)


def _final_norm(geom, x, g):
    return pl.pallas_call(
        functools.partial(_norm_kernel, rows=None),
        grid=(geom.batch * geom.seq // ROW_TILE,),
        in_specs=[pl.BlockSpec((ROW_TILE, D_MODEL), lambda i: (i, 0)),
                  pl.BlockSpec((1, D_MODEL), lambda i: (0, 0))],
        out_specs=pl.BlockSpec((ROW_TILE, D_MODEL), lambda i: (i, 0)),
        out_shape=jax.ShapeDtypeStruct((geom.batch * geom.seq, D_MODEL), F32),
        compiler_params=_params(("parallel",)),
        name="final_norm",
    )(x, g.reshape(1, D_MODEL))


def _mm_store(acc, res_ref, mod_refs, o_ref, epilogue, gate_row, norm=None, cs=slice(None)):
    if epilogue == "resid":
        for s, mod_ref in enumerate(mod_refs):
            rs = slice(s * ROW_TILE, (s + 1) * ROW_TILE)
            xn = res_ref[rs, cs] + mod_ref[0, gate_row:gate_row + 1, cs] * acc[rs]
            o_ref[rs, cs] = xn
            if norm is not None:
                g_ref, nmod_refs, h_ref, rows = norm
                nmod = nmod_refs[s][0]
                y = xn * lax.rsqrt(jnp.mean(xn * xn, axis=-1, keepdims=True) + NORM_EPS) * g_ref[...]
                h_ref[rs] = (y * (1.0 + nmod[rows[1]:rows[1] + 1]) + nmod[rows[0]:rows[0] + 1]).astype(h_ref.dtype)
        return
    if epilogue == "sigmoid":
        acc = _sigmoid(acc)
    elif epilogue == "relu2":
        acc = jnp.square(jnp.maximum(acc, 0.0))
    o_ref[:, cs] = acc.astype(o_ref.dtype)


def _col_chunks(tn, norm):
    if norm is not None or tn % MXU_COLS:
        return [slice(None)]
    return [slice(c, c + MXU_COLS) for c in range(0, tn, MXU_COLS)]


def _mm_kernel(a_ref, b_ref, *rest, epilogue, gate_row, nk, n_sub, norm_rows):
    rest = list(rest)
    res_ref, mod_refs, norm = None, (), None
    if epilogue == "resid":
        res_ref, mod_refs, rest = rest[0], rest[1:1 + n_sub], rest[1 + n_sub:]
        if norm_rows is not None:
            norm = (rest[0], rest[1:1 + n_sub], rest[2 + n_sub], norm_rows)
            rest = [rest[1 + n_sub]] + rest[3 + n_sub:]
    o_ref = rest[0]
    chunks = _col_chunks(o_ref.shape[1], norm)
    if nk == 1:
        for cs in chunks:
            acc = jnp.dot(a_ref[...], b_ref[:, cs], preferred_element_type=F32)
            _mm_store(acc, res_ref, mod_refs, o_ref, epilogue, gate_row, norm, cs)
        return
    acc_ref = rest[1]
    k = pl.program_id(2)

    @pl.when(k == 0)
    def _():
        acc_ref[...] = jnp.dot(a_ref[...], b_ref[...], preferred_element_type=F32)

    @pl.when((k > 0) & (k < nk - 1))
    def _():
        acc_ref[...] += jnp.dot(a_ref[...], b_ref[...], preferred_element_type=F32)

    @pl.when(k == nk - 1)
    def _():
        for cs in chunks:
            acc = acc_ref[:, cs] + jnp.dot(a_ref[...], b_ref[:, cs], preferred_element_type=F32)
            _mm_store(acc, res_ref, mod_refs, o_ref, epilogue, gate_row, norm, cs)


def _mm_w32_kernel(a_ref, w_ref, *rest, epilogue, gate_row):
    res_ref, mod_refs = (rest[0], rest[1:-2]) if epilogue == "resid" else (None, ())
    o_ref, wb_ref = rest[-2:]

    @pl.when(pl.program_id(1) == 0)
    def _():
        def cast_rows(r, carry):
            rows = pl.ds(pl.multiple_of(r * CAST_ROWS, CAST_ROWS), CAST_ROWS)
            wb_ref[rows, :] = w_ref[0, rows, :].astype(BF16)
            return carry

        lax.fori_loop(0, wb_ref.shape[0] // CAST_ROWS, cast_rows, 0)

    for cs in _col_chunks(o_ref.shape[1], None):
        acc = jnp.dot(a_ref[...], wb_ref[:, cs], preferred_element_type=F32)
        _mm_store(acc, res_ref, mod_refs, o_ref, epilogue, gate_row, cs=cs)


def _pick_tile(n, cap, unit=LANES):
    t = (min(cap, n) // unit) * unit
    while n % t:
        t -= unit
    return t


class _RowPlan:
    def __init__(self, geom, tm_cap, kind="all", rows=None):
        if kind in ("latent", "packed"):
            self.tm = _pick_tile(geom.seq, tm_cap, ROW_TILE)
            per_b = geom.seq // self.tm
            self.n = geom.batch * per_b
            self.rows = geom.m if kind == "latent" else geom.batch * geom.seq
            if kind == "latent":
                self.off = lambda i: (i // per_b) * geom.lt + geom.ctx_len + (i % per_b) * self.tm
                self.mod_row = geom.mod_row
            else:
                self.off = lambda i: i * self.tm
                self.mod_row = lambda t: 2 * (t // (geom.seq // ROW_TILE)) + 1
        else:
            self.rows = geom.m if kind == "all" else rows
            self.tm = _pick_tile(self.rows, tm_cap, ROW_TILE)
            self.n = self.rows // self.tm
            self.off = lambda i: i * self.tm
            self.mod_row = geom.mod_row if kind == "all" else None
        self.n_sub = self.tm // ROW_TILE

    def spec(self, width, at):
        def index(*g):
            i, col = at(*g)
            col = col if isinstance(col, int) else pl.multiple_of(col, LANES)
            return pl.multiple_of(self.off(i), ROW_TILE), col
        return pl.BlockSpec((pl.Element(self.tm), pl.Element(width)), index)

    def mod_specs(self, tn, at):
        return [pl.BlockSpec((1, 6, tn), lambda *g, s=s: (
            self.mod_row(self.off(at(*g)[0]) // ROW_TILE + s), 0, at(*g)[1])) for s in range(self.n_sub)]


def _matmul(plan, a, b, out_dtype, epilogue="none", res=None, modtab=None, gate_row=0, norm=None,
            res_plan=None, out_plan=None, tn_cap=1024, tk_cap=2048, name="matmul"):
    kdim = a.shape[1]
    n = b.shape[1]
    out_plan = out_plan or plan
    res_plan = res_plan or out_plan
    tm, m = plan.tm, out_plan.rows
    assert (out_plan.tm, out_plan.n, res_plan.tm, res_plan.n) == (tm, plan.n, tm, plan.n)
    tn = n if norm is not None else _pick_tile(n, tn_cap)
    tk = _pick_tile(kdim, tk_cap)
    nk = kdim // tk
    row_j = lambda i, j, k: (i, j * tn)
    tile_ij = lambda i, j, k: (i, j)
    in_specs = [plan.spec(tk, lambda i, j, k: (i, k * tk)),
                pl.BlockSpec((tk, tn), lambda i, j, k: (k, j))]
    args = [a, b]
    out_specs = out_plan.spec(tn, row_j)
    out_shape = jax.ShapeDtypeStruct((m, n), out_dtype)
    if epilogue == "resid":
        in_specs += [res_plan.spec(tn, row_j)] + out_plan.mod_specs(tn, tile_ij)
        args += [res] + [modtab] * plan.n_sub
        if norm is not None:
            gain, nmodtab, _ = norm
            in_specs += [pl.BlockSpec((1, tn), lambda i, j, k: (0, 0))] + out_plan.mod_specs(tn, tile_ij)
            args += [gain.reshape(1, n)] + [nmodtab] * plan.n_sub
            out_specs = [out_specs, out_plan.spec(tn, row_j)]
            out_shape = [out_shape, jax.ShapeDtypeStruct((m, n), BF16)]
    return pl.pallas_call(
        functools.partial(_mm_kernel, epilogue=epilogue, gate_row=gate_row, nk=nk, n_sub=plan.n_sub,
                          norm_rows=None if norm is None else norm[2]),
        grid=(plan.n, n // tn, nk),
        in_specs=in_specs,
        out_specs=out_specs,
        out_shape=out_shape,
        scratch_shapes=[pltpu.VMEM((tm, tn), F32)] if nk > 1 else [],
        compiler_params=_params(("parallel", "parallel", "arbitrary")),
        name=name,
    )(*args)


def _matmul_w32(plan, a, w, layer, out_dtype, epilogue="none", cols=None, out_plan=None, tn_cap=1024,
                name="matmul_w32"):
    kdim = a.shape[1]
    col0, n = cols if cols is not None else (0, w.shape[2])
    assert col0 % LANES == 0 and epilogue != "resid"
    out_plan = out_plan or plan
    assert (out_plan.tm, out_plan.n) == (plan.tm, plan.n)
    tn = _pick_tile(n, tn_cap)
    in_specs = [plan.spec(kdim, lambda j, i: (i, 0)),
                pl.BlockSpec((pl.Element(1), pl.Element(kdim), pl.Element(tn)),
                             lambda j, i: (layer, 0, pl.multiple_of(col0 + j * tn, LANES)))]
    args = [a, w]
    return pl.pallas_call(
        functools.partial(_mm_w32_kernel, epilogue=epilogue, gate_row=0),
        grid=(n // tn, plan.n),
        in_specs=in_specs,
        out_specs=out_plan.spec(tn, lambda j, i: (i, j * tn)),
        out_shape=jax.ShapeDtypeStruct((out_plan.rows, n), out_dtype),
        scratch_shapes=[pltpu.VMEM((kdim, tn), BF16)],
        compiler_params=_params(("parallel", "arbitrary")),
        name=name,
    )(*args)


def _head_sum_matrix(width, head):
    idx = np.arange(width) // head
    return jnp.asarray((idx[:, None] == idx[None, :]).astype(np.float32), dtype=BF16)


def _prep_kernel(p_ref, lora_ref, halo_ref, hlora_ref, mu_ref, w0_ref, wup_ref, a0_ref, aup_ref,
                 kk_ref, ka_ref, hs_ref, r_o, lw_o, k_o, v_o, kkn_o, b_o, *, tiles_b, ctx_tiles):
    d = pl.program_id(0)
    j = pl.program_id(1) % tiles_b
    fwd = d == 0
    t = p_ref.shape[0]
    rkv_w = 3 * BRANCH_W
    at_start = (j == 0) | (j == ctx_tiles)
    at_end = (j == ctx_tiles - 1) | (j == tiles_b - 1)
    no_edge = (fwd & at_start) | (jnp.logical_not(fwd) & at_end)
    row = lax.broadcasted_iota(jnp.int32, (t, 1), 0)
    shift = jnp.where(fwd, 1, t - 1)
    edge_row = jnp.where(fwd, 0, t - 1)

    def neighbour(x_ref, h_ref):
        edge = jnp.where(no_edge, 0.0, jnp.where(fwd, h_ref[7:8], h_ref[0:1]))
        return jnp.where(row == edge_row, edge, pltpu.roll(x_ref[...], shift, 0))

    nb_rkv = neighbour(p_ref, halo_ref)
    nb_lora = neighbour(lora_ref, hlora_ref)
    mu = mu_ref[0]
    for c in range(t // PREP_ROWS):
        rs = slice(c * PREP_ROWS, (c + 1) * PREP_ROWS)
        f = p_ref[rs]
        f = f + mu[:, :rkv_w] * (nb_rkv[rs] - f)
        lo = lora_ref[rs]
        lo = lo + mu[:, rkv_w:] * (nb_lora[rs] - lo)
        r = f[:, 0:BRANCH_W]
        k = f[:, BRANCH_W:2 * BRANCH_W]
        v = f[:, 2 * BRANCH_W:rkv_w]
        w_raw = w0_ref[0] + _dot3(jnp.tanh(lo[:, :DECAY_LORA]), wup_ref[0])
        lw = -math.exp(-0.5) * _sigmoid(w_raw)
        a = _sigmoid(a0_ref[0] + _dot3(lo[:, DECAY_LORA:], aup_ref[0]))
        kk = k * kk_ref[...]
        norm = jnp.sqrt(_dot_sel(kk * kk, hs_ref[...]))
        kk = kk / jnp.maximum(norm, 1e-12)
        r_o[0, rs] = r
        lw_o[0, rs] = lw
        k_o[0, rs] = k * (1.0 + (a - 1.0) * ka_ref[...])
        v_o[0, rs] = v
        kkn_o[0, rs] = kk
        b_o[0, rs] = kk * a


def _rwkv_prep(geom, p1, lp):
    m = geom.m
    t = ROW_TILE
    tb = geom.tiles_b
    rkv_w = 3 * BRANCH_W
    lora_blk0 = O_LORA // DIR_LORA_W
    n_blk8 = m // 8

    def halo_idx(d, i):
        before = jnp.maximum(i * (t // 8) - 1, 0)
        after = jnp.minimum((i + 1) * (t // 8), n_blk8 - 1)
        return jnp.where(d == 0, before, after)

    out = jax.ShapeDtypeStruct((N_DIR, m, BRANCH_W), F32)
    ospec = pl.BlockSpec((1, t, BRANCH_W), lambda d, i: (d, i, 0))
    vec = lambda a: a.reshape(1, BRANCH_W)
    dvec = pl.BlockSpec((1, 1, BRANCH_W), lambda d, i: (d, 0, 0))
    return pl.pallas_call(
        functools.partial(_prep_kernel, tiles_b=tb, ctx_tiles=geom.ctx_tiles),
        grid=(N_DIR, geom.tiles),
        in_specs=[pl.BlockSpec((t, rkv_w), lambda d, i: (i, 0)),
                  pl.BlockSpec((t, DIR_LORA_W), lambda d, i: (i, lora_blk0 + d)),
                  pl.BlockSpec((8, rkv_w), lambda d, i: (halo_idx(d, i), 0)),
                  pl.BlockSpec((8, DIR_LORA_W), lambda d, i: (halo_idx(d, i), lora_blk0 + d)),
                  pl.BlockSpec((1, 1, SHIFT_W), lambda d, i: (d, 0, 0)),
                  dvec,
                  pl.BlockSpec((1, DECAY_LORA, BRANCH_W), lambda d, i: (d, 0, 0)),
                  dvec,
                  pl.BlockSpec((1, AICL_LORA, BRANCH_W), lambda d, i: (d, 0, 0)),
                  pl.BlockSpec((1, BRANCH_W), lambda d, i: (0, 0)),
                  pl.BlockSpec((1, BRANCH_W), lambda d, i: (0, 0)),
                  pl.BlockSpec((BRANCH_W, BRANCH_W), lambda d, i: (0, 0))],
        out_specs=[ospec] * 6,
        out_shape=[out] * 6,
        compiler_params=_params(("parallel", "parallel")),
        name="rwkv_prep",
    )(p1, p1, p1, p1, lp["rwkv_mu"].reshape(N_DIR, 1, SHIFT_W),
      lp["rwkv_w0"].reshape(N_DIR, 1, BRANCH_W), lp["rwkv_w_up"],
      lp["rwkv_a0"].reshape(N_DIR, 1, BRANCH_W), lp["rwkv_a_up"],
      vec(lp["rwkv_k_k"]), vec(lp["rwkv_k_a"]), _head_sum_matrix(BRANCH_W, RWKV_HEAD))


def _scan_kernel(*refs):
    yf_ref, yb_ref, h_ref = refs[12:]
    c = pl.program_id(1)
    C = SCAN_CHUNK
    W = 2 * C

    @pl.when(c == 0)
    def _():
        h_ref[...] = jnp.zeros_like(h_ref)

    rr = lax.broadcasted_iota(jnp.int32, (C, C), 0)
    cc = lax.broadcasted_iota(jnp.int32, (C, C), 1)
    lane = lax.broadcasted_iota(jnp.int32, (1, PAIR_W), 1)
    m_a = (lane < RWKV_HEAD).astype(F32)
    m_b = 1.0 - m_a
    r2 = lax.broadcasted_iota(jnp.int32, (W, W), 0)
    c2 = lax.broadcasted_iota(jnp.int32, (W, W), 1)
    same = (r2 // C) == (c2 // C)
    eye = (r2 == c2).astype(F32)

    def pairs(x):
        return [jnp.concatenate([x[:, p * PAIR_W:(p + 1) * PAIR_W] * m_a,
                                 x[:, p * PAIR_W:(p + 1) * PAIR_W] * m_b], axis=0) for p in range(N_PAIRS)]

    strict, incl, tri = [], [], []
    for sgn in (1, -1):
        dt = (r2 % C - c2 % C) * sgn
        strict += [(same & (dt > 0)).astype(F32)] * N_PAIRS
        incl += [(same & (dt >= 0)).astype(F32)] * N_PAIRS
        tri.append(jnp.where((rr - cc) * sgn >= 0, 1.0, 0.0).astype(BF16))
    strict = jnp.stack(strict)
    incl = jnp.stack(incl)

    h = h_ref[...]
    for step in range(SCAN_STEPS):
        rows = (pl.ds(step * C, C), pl.ds((SCAN_STEPS - 1 - step) * C, C))
        h = _scan_chunk(refs[:12], rows, tri, strict, incl, eye, pairs, h, (yf_ref, yb_ref))
    h_ref[...] = h


def _scan_chunk(in_refs, rows, tri, strict, incl, eye, pairs, h0, y_refs):
    C = SCAN_CHUNK
    W = 2 * C
    stacks = {name: [] for name in ("a", "b", "k", "r", "v", "bc", "kc", "pt")}
    for d in range(N_DIR):
        r_ref, lw_ref, k_ref, v_ref, kk_ref, b_ref = in_refs[d::N_DIR]
        rs = rows[d]
        lw = lw_ref[0, rs]
        lp_in = _dot_sel_lhs(tri[d], lw)
        tot = jnp.sum(lw, axis=0, keepdims=True)
        e_neg = jnp.exp(-lp_in)
        e_chk = jnp.exp(tot - lp_in)
        p_tot = jnp.exp(tot)
        stacks["a"] += pairs(-kk_ref[0, rs] * jnp.exp(lp_in - lw))
        stacks["b"] += pairs(b_ref[0, rs] * e_neg)
        stacks["k"] += pairs(k_ref[0, rs] * e_neg)
        stacks["r"] += pairs(r_ref[0, rs] * jnp.exp(lp_in))
        stacks["v"] += pairs(v_ref[0, rs])
        stacks["bc"] += pairs(b_ref[0, rs] * e_chk)
        stacks["kc"] += pairs(k_ref[0, rs] * e_chk)
        stacks["pt"] += [p_tot[:, p * PAIR_W:(p + 1) * PAIR_W] for p in range(N_PAIRS)]
    a_s, b_s, k_s, r_s, v_s, bc_s, kc_s, p_tot = (jnp.stack(stacks[n]) for n in
                                                  ("a", "b", "k", "r", "v", "bc", "kc", "pt"))

    a_b, b_b, k_b, r_b, v_b, bc_b, kc_b = (x.astype(BF16) for x in (a_s, b_s, k_s, r_s, v_s, bc_s, kc_s))
    big = _bdot(jnp.concatenate([a_b, r_b], axis=1), jnp.concatenate([b_b, k_b], axis=1), BNT_DIMS)
    l_ab = big[:, :W, :W] * strict
    l_ak = big[:, :W, W:] * strict
    m_rb = (big[:, W:, :W] * incl).astype(BF16)
    m_rk = big[:, W:, W:] * incl
    t_inv = eye + l_ab
    pw_b = l_ab.astype(BF16)
    pw_b = _bdot(pw_b, pw_b).astype(BF16)
    for _ in range(int(math.log2(C)) - 2):
        both = _bdot(jnp.concatenate([t_inv.astype(BF16), pw_b], axis=1), pw_b)
        t_inv = t_inv + both[:, :W]
        pw_b = both[:, W:].astype(BF16)
    t_inv = t_inv + _bdot(t_inv, pw_b)
    x1 = _bdot(t_inv, jnp.concatenate([a_b, _bdot(l_ak, v_b).astype(BF16)], axis=2)).astype(BF16)
    x2 = _bdot(m_rb, x1)
    r_hat = r_s + x2[:, :, :PAIR_W]
    y0 = x2[:, :, PAIR_W:] + _bdot(m_rk, v_b)
    x3 = _bdot(bc_b, x1, BTN_DIMS)
    g = eye * p_tot + x3[:, :, :PAIR_W]
    h_inc = x3[:, :, PAIR_W:] + _bdot(kc_b, v_b, BTN_DIMS)
    x4 = _bdot(jnp.concatenate([r_hat, g], axis=1), h0)
    ys = x4[:, :W] + y0
    for d, y_ref in enumerate(y_refs):
        for p in range(N_PAIRS):
            y_ref[rows[d], p * PAIR_W:(p + 1) * PAIR_W] = ys[d * N_PAIRS + p, :C] + ys[d * N_PAIRS + p, C:]
    return x4[:, W:] + h_inc


BNN_DIMS = (((2,), (1,)), ((0,), (0,)))
BNT_DIMS = (((2,), (2,)), ((0,), (0,)))
BTN_DIMS = (((1,), (1,)), ((0,), (0,)))


def _bdot(a, b, dims=BNN_DIMS):
    return lax.dot_general(a.astype(BF16), b.astype(BF16), dims, preferred_element_type=F32)


def _dot_sel_lhs(sel_bf16, a):
    hi = a.astype(BF16)
    r1 = a - hi.astype(F32)
    mid = r1.astype(BF16)
    lo = (r1 - mid.astype(F32)).astype(BF16)
    dg = functools.partial(jnp.dot, preferred_element_type=F32)
    return dg(sel_bf16, hi) + (dg(sel_bf16, mid) + dg(sel_bf16, lo))


def _rwkv_scan(geom, ins):
    C = SCAN_CHUNK * SCAN_STEPS
    assert geom.ctx_len % C == 0 and geom.seq % C == 0
    nch = geom.lt // C
    nctx = geom.ctx_len // C

    def rev(c):
        return jnp.where(c < nctx, nctx - 1 - c, nch - 1 + nctx - c)

    fwd = pl.BlockSpec((1, C, BRANCH_W), lambda b, c: (0, b * nch + c, 0))
    bwd = pl.BlockSpec((1, C, BRANCH_W), lambda b, c: (1, b * nch + rev(c), 0))
    out = jax.ShapeDtypeStruct((geom.m, BRANCH_W), F32)
    return pl.pallas_call(
        _scan_kernel,
        grid=(geom.batch, nch),
        in_specs=[fwd, bwd] * 6,
        out_specs=[pl.BlockSpec((C, BRANCH_W), lambda b, c: (b * nch + c, 0)),
                   pl.BlockSpec((C, BRANCH_W), lambda b, c: (b * nch + rev(c), 0))],
        out_shape=[out, out],
        scratch_shapes=[pltpu.VMEM((N_DIR * N_PAIRS, PAIR_W, PAIR_W), F32)],
        compiler_params=_params(("parallel", "arbitrary")),
        name="rwkv_scan",
    )(*[a for a in ins for _ in range(N_DIR)])


def _readout_kernel(yf_ref, yb_ref, r_ref, k_ref, v_ref, p2_ref, gup_ref, rk_ref, lg_ref, lb_ref, hs_ref, o_ref):
    hs = hs_ref[...]
    y = yf_ref[...] + yb_ref[...]
    inv_n = 1.0 / RWKV_HEAD
    mean = _dot_sel(y, hs) * inv_n
    yc = y - mean
    var = _dot_sel(yc * yc, hs) * inv_n
    yn = yc * lax.rsqrt(var + GN_EPS) * lg_ref[...] + lb_ref[...]
    bonus = jnp.zeros_like(y)
    for d in range(N_DIR):
        bonus = bonus + _dot_sel(r_ref[d] * k_ref[d] * rk_ref[d:d + 1], hs) * v_ref[d]
    g = _dot3(_sigmoid(p2_ref[:, 0:GATE_LORA]), gup_ref[...])
    o_ref[...] = ((yn + bonus) * g).astype(o_ref.dtype)


def _rwkv_readout(geom, y, ins, p2, lp):
    t = ROW_TILE
    dspec = pl.BlockSpec((N_DIR, t, BRANCH_W), lambda i: (0, i, 0))
    vspec = pl.BlockSpec((1, BRANCH_W), lambda i: (0, 0))
    r, _, k, v, _, _ = ins
    return pl.pallas_call(
        _readout_kernel,
        grid=(geom.tiles,),
        in_specs=[pl.BlockSpec((t, BRANCH_W), lambda i: (i, 0)),
                  pl.BlockSpec((t, BRANCH_W), lambda i: (i, 0)),
                  dspec, dspec, dspec,
                  pl.BlockSpec((t, GATE_LORA + BRANCH_W), lambda i: (i, 0)),
                  pl.BlockSpec((GATE_LORA, BRANCH_W), lambda i: (0, 0)),
                  pl.BlockSpec((N_DIR, BRANCH_W), lambda i: (0, 0)),
                  vspec, vspec,
                  pl.BlockSpec((BRANCH_W, BRANCH_W), lambda i: (0, 0))],
        out_specs=pl.BlockSpec((t, BRANCH_W), lambda i: (i, 0)),
        out_shape=jax.ShapeDtypeStruct((geom.m, BRANCH_W), BF16),
        compiler_params=_params(("parallel",)),
        name="rwkv_readout",
    )(y[0], y[1], r, k, v, p2, lp["rwkv_g_up"], lp["rwkv_r_k"].reshape(N_DIR, BRANCH_W),
      lp["rwkv_lnx_g"].reshape(1, BRANCH_W), lp["rwkv_lnx_b"].reshape(1, BRANCH_W),
      _head_sum_matrix(BRANCH_W, RWKV_HEAD))


def _rope_tables(geom):
    half = ATT_HEAD // 2
    nf = half // 2
    inv = ROPE_BASE ** (-jnp.arange(nf, dtype=F32) / nf)
    pos = jnp.arange(geom.seq, dtype=jnp.int32)
    row_ang = (pos // GRID_W).astype(F32)[:, None] * inv[None, :]
    col_ang = (pos % GRID_W).astype(F32)[:, None] * inv[None, :]
    cos = jnp.concatenate([jnp.cos(row_ang)] * 2 + [jnp.cos(col_ang)] * 2, axis=-1)
    sin = jnp.concatenate([-jnp.sin(row_ang), jnp.sin(row_ang), -jnp.sin(col_ang), jnp.sin(col_ang)], axis=-1)
    cos = jnp.concatenate([jnp.ones((geom.ctx_len, ATT_HEAD), F32), cos], axis=0)
    sin = jnp.concatenate([jnp.zeros((geom.ctx_len, ATT_HEAD), F32), sin], axis=0)
    return cos, sin


def _rotate(t, cos, sin):
    w = t.shape[-1]
    nf = ATT_HEAD // 4
    lane = lax.broadcasted_iota(jnp.int32, (1, w), 1)
    partner = jnp.where((lane % (2 * nf)) < nf, pltpu.roll(t, w - nf, 1), pltpu.roll(t, nf, 1))
    return t * cos + partner * sin


def _rope_kernel(p2_ref, kv_ref, cq_ref, sq_ref, ck_ref, sk_ref, q_o, k_o, v_o):
    q = p2_ref[:, GATE_LORA:]
    q_o[...] = (_rotate(q, cq_ref[...], sq_ref[...]) * (ATT_HEAD ** -0.5)).astype(q_o.dtype)
    kv = kv_ref[...]
    k_o[...] = _rotate(kv[:, :ATT_KV_W], ck_ref[...], sk_ref[...]).astype(k_o.dtype)
    v_o[...] = kv[:, ATT_KV_W:].astype(v_o.dtype)


def _rope(geom, p1, p2):
    t = ROW_TILE
    tb = geom.tiles_b
    cos, sin = _rope_tables(geom)
    cq, sq = jnp.tile(cos, (1, ATT_HEADS)), jnp.tile(sin, (1, ATT_HEADS))
    ck, sk = jnp.tile(cos, (1, ATT_KV_HEADS)), jnp.tile(sin, (1, ATT_KV_HEADS))
    qspec = pl.BlockSpec((t, BRANCH_W), lambda i: (i % tb, 0))
    kspec = pl.BlockSpec((t, ATT_KV_W), lambda i: (i % tb, 0))
    m = geom.m
    return pl.pallas_call(
        _rope_kernel,
        grid=(geom.tiles,),
        in_specs=[pl.BlockSpec((t, GATE_LORA + BRANCH_W), lambda i: (i, 0)),
                  pl.BlockSpec((t, 2 * ATT_KV_W), lambda i: (i, O_KV // (2 * ATT_KV_W))),
                  qspec, qspec, kspec, kspec],
        out_specs=[pl.BlockSpec((t, BRANCH_W), lambda i: (i, 0)),
                   pl.BlockSpec((t, ATT_KV_W), lambda i: (i, 0)),
                   pl.BlockSpec((t, ATT_KV_W), lambda i: (i, 0))],
        out_shape=[jax.ShapeDtypeStruct((m, BRANCH_W), BF16),
                   jax.ShapeDtypeStruct((m, ATT_KV_W), BF16),
                   jax.ShapeDtypeStruct((m, ATT_KV_W), BF16)],
        compiler_params=_params(("parallel",)),
        name="rope",
    )(p2, p1, cq, sq, ck, sk)


def _attn_kernel(q_ref, kc_ref, vc_ref, kp_ref, ko_ref, kn_ref, vp_ref, vo_ref, vn_ref, sink_ref, o_ref,
                 s_ref, p_ref, *, ctx_blocks, blocks_b):
    j = pl.program_id(1)
    q = q_ref[...]
    k_all = jnp.concatenate([kp_ref[...], ko_ref[...], kn_ref[...], kc_ref[...]], axis=0)
    v_all = jnp.concatenate([vp_ref[...], vo_ref[...], vn_ref[...], vc_ref[...]], axis=0)
    nloc = 3 * BLOCK
    slab = ATT_SLAB
    qi0 = lax.broadcasted_iota(jnp.int32, (slab, nloc), 0)
    ki = lax.broadcasted_iota(jnp.int32, (slab, nloc), 1)
    never = 4 * BLOCK
    prev_off = jnp.where(j > ctx_blocks, 0, never)
    own_hi = jnp.where(j >= ctx_blocks, 2 * BLOCK, BLOCK)
    next_off = 2 * BLOCK - jnp.where((j >= ctx_blocks) & (j < blocks_b - 1), 0, never)
    outs = []
    for g in range(ATT_KV_HEADS):
        gs = slice(g * ATT_HEAD, (g + 1) * ATT_HEAD)
        qg = jnp.concatenate([q[:, (g * ATT_REP + h) * ATT_HEAD:(g * ATT_REP + h + 1) * ATT_HEAD]
                              for h in range(ATT_REP)], axis=0)
        s_ref[g] = lax.dot_general(qg, k_all[:, gs], NT_DIMS, preferred_element_type=F32)
        dens = []
        for blk in range(ATT_REP * BLOCK // slab):
            rs = slice(blk * slab, (blk + 1) * slab)
            head = g * ATT_REP + blk * slab // BLOCK
            qi = qi0 + (blk * slab) % BLOCK
            valid = (((ki < BLOCK) & (ki >= qi + prev_off)) | ((ki >= BLOCK) & (ki < own_hi))
                     | ((ki >= 2 * BLOCK) & (ki <= qi + next_off)))
            s_loc = jnp.where(valid, s_ref[g, rs, :nloc], NEG_INF)
            s_ctx = s_ref[g, rs, nloc:]
            sink = sink_ref[head:head + 1, 0:1]
            mx = jnp.maximum(jnp.maximum(jnp.max(s_loc, axis=-1, keepdims=True),
                                         jnp.max(s_ctx, axis=-1, keepdims=True)), sink)
            e_loc = jnp.exp(s_loc - mx)
            e_ctx = jnp.exp(s_ctx - mx)
            dens.append(jnp.sum(e_loc, axis=-1, keepdims=True) + jnp.sum(e_ctx, axis=-1, keepdims=True)
                        + jnp.exp(sink - mx))
            p_ref[g, rs, :nloc] = e_loc.astype(BF16)
            p_ref[g, rs, nloc:] = e_ctx.astype(BF16)
        o = jnp.dot(p_ref[g], v_all[:, gs], preferred_element_type=F32) / jnp.concatenate(dens, axis=0)
        outs += [o[h * BLOCK:(h + 1) * BLOCK] for h in range(ATT_REP)]
    o_ref[...] = jnp.concatenate(outs, axis=-1).astype(o_ref.dtype)


def _attention(geom, q, k, v, sink):
    nb = geom.lt // BLOCK
    cb = geom.ctx_len // BLOCK
    row = lambda b, j: (b * nb + j, 0)
    prev = lambda b, j: (b * nb + jnp.maximum(j - 1, 0), 0)
    nxt = lambda b, j: (b * nb + jnp.minimum(j + 1, nb - 1), 0)
    ctx = lambda b, j: (b * (geom.lt // geom.ctx_len), 0)
    assert geom.lt % geom.ctx_len == 0
    kvs = lambda f: pl.BlockSpec((BLOCK, ATT_KV_W), f)
    cspec = pl.BlockSpec((geom.ctx_len, ATT_KV_W), ctx)
    sink_tab = jnp.broadcast_to(sink.astype(F32)[:, None], (ATT_HEADS, LANES))
    return pl.pallas_call(
        functools.partial(_attn_kernel, ctx_blocks=cb, blocks_b=nb),
        grid=(geom.batch, nb),
        in_specs=[pl.BlockSpec((BLOCK, BRANCH_W), row), cspec, cspec,
                  kvs(prev), kvs(row), kvs(nxt), kvs(prev), kvs(row), kvs(nxt),
                  pl.BlockSpec((ATT_HEADS, LANES), lambda b, j: (0, 0))],
        out_specs=pl.BlockSpec((BLOCK, BRANCH_W), row),
        out_shape=jax.ShapeDtypeStruct((geom.m, BRANCH_W), BF16),
        scratch_shapes=[pltpu.VMEM((ATT_KV_HEADS, ATT_REP * BLOCK, 3 * BLOCK + geom.ctx_len), F32),
                        pltpu.VMEM((ATT_KV_HEADS, ATT_REP * BLOCK, 3 * BLOCK + geom.ctx_len), BF16)],
        compiler_params=_params(("parallel", "parallel")),
        name="attention",
    )(q, k, v, k, k, k, v, v, v, sink_tab)


def _conv_kernel(u_ref, up_ref, un_ref, dw_ref, db_ref, lg_ref, lb_ref, o_ref, hp_ref, sh_ref,
                 *, tiles_b, ctx_tiles):
    j = pl.program_id(0) % tiles_b
    t = u_ref.shape[0]

    def glu(u):
        return u[:, :BRANCH_W] * _sigmoid(u[:, BRANCH_W:])

    at_start = (j == 0) | (j == ctx_tiles)
    at_end = (j == ctx_tiles - 1) | (j == tiles_b - 1)
    hp_ref[0:HALO] = jnp.where(at_start, 0.0, glu(up_ref[...]))
    hp_ref[HALO:HALO + t] = glu(u_ref[...])
    hp_ref[HALO + t:] = jnp.where(at_end, 0.0, glu(un_ref[...]))
    span = t + 2 * HALO - SUBLANES
    for r in range(1, SUBLANES):
        sh_ref[r, 0:span] = hp_ref[pl.ds(r, span)]
    for chunk in range(t // CONV_ROWS):
        acc = jnp.zeros((CONV_ROWS, BRANCH_W), F32) + db_ref[...]
        for tap in range(CONV_K):
            q, r = divmod(HALO - CONV_PAD + tap, SUBLANES)
            rows = pl.ds(chunk * CONV_ROWS + q * SUBLANES, CONV_ROWS)
            src = hp_ref[rows] if r == 0 else sh_ref[r, rows]
            acc = acc + src * dw_ref[tap:tap + 1]
        mean = jnp.mean(acc, axis=-1, keepdims=True)
        cen = acc - mean
        var = jnp.mean(cen * cen, axis=-1, keepdims=True)
        h = cen * lax.rsqrt(var + LN_EPS) * lg_ref[...] + lb_ref[...]
        o_ref[chunk * CONV_ROWS:(chunk + 1) * CONV_ROWS] = (h * _sigmoid(h)).astype(o_ref.dtype)


def _conv(geom, p4, lp):
    t = ROW_TILE
    nh = geom.m // HALO
    vspec = pl.BlockSpec((1, BRANCH_W), lambda i: (0, 0))
    return pl.pallas_call(
        functools.partial(_conv_kernel, tiles_b=geom.tiles_b, ctx_tiles=geom.ctx_tiles),
        grid=(geom.tiles,),
        in_specs=[pl.BlockSpec((t, 2 * BRANCH_W), lambda i: (i, 0)),
                  pl.BlockSpec((HALO, 2 * BRANCH_W), lambda i: (jnp.maximum(i * (t // HALO) - 1, 0), 0)),
                  pl.BlockSpec((HALO, 2 * BRANCH_W), lambda i: (jnp.minimum((i + 1) * (t // HALO), nh - 1), 0)),
                  pl.BlockSpec((CONV_K, BRANCH_W), lambda i: (0, 0)),
                  vspec, vspec, vspec],
        out_specs=pl.BlockSpec((t, BRANCH_W), lambda i: (i, 0)),
        out_shape=jax.ShapeDtypeStruct((geom.m, BRANCH_W), BF16),
        scratch_shapes=[pltpu.VMEM((t + 2 * HALO, BRANCH_W), F32),
                        pltpu.VMEM((SUBLANES, t + 2 * HALO, BRANCH_W), F32)],
        compiler_params=_params(("parallel",)),
        name="conformer_conv",
    )(p4, p4, p4, lp["conv_dw"], lp["conv_dw_b"].reshape(1, BRANCH_W),
      lp["conv_ln_g"].reshape(1, BRANCH_W), lp["conv_ln_b"].reshape(1, BRANCH_W))


def _dft_cos_sin(n, scale):
    idx = np.arange(n, dtype=np.int64)
    ang = 2.0 * np.pi * ((idx[:, None] * idx[None, :]) % n).astype(np.float64) / n
    return np.cos(ang) * scale, np.sin(ang) * scale


def _channel_dft():
    c, s = _dft_cos_sin(FNO_GROUP_W, FNO_GROUP_W ** -0.5)
    eye = np.eye(FNO_GROUPS)
    return jnp.asarray(np.concatenate([np.kron(eye, c), np.kron(eye, s)], axis=1), dtype=F32).astype(BF16)


def _dft_pos_kernel(c_ref, s_ref, gc_ref, gs_ref, o_ref):
    o_ref[0] = (jnp.dot(c_ref[...], gc_ref[0], preferred_element_type=F32)
                + jnp.dot(s_ref[...], gs_ref[0], preferred_element_type=F32)).astype(o_ref.dtype)


def _fourier(u):
    bsz, length, _ = u.shape
    gcs = _matmul(_RowPlan(None, 1024, "plain", bsz * length), u.reshape(bsz * length, BRANCH_W), _channel_dft(),
                  BF16, name="dft_channels")
    gcs = gcs.reshape(bsz, length, 2 * BRANCH_W)
    c, s = _dft_cos_sin(length, length ** -0.5)
    tm = _pick_tile(length, 512, ROW_TILE)
    return pl.pallas_call(
        _dft_pos_kernel,
        grid=(length // tm, bsz),
        in_specs=[pl.BlockSpec((tm, length), lambda i, b: (i, 0)),
                  pl.BlockSpec((tm, length), lambda i, b: (i, 0)),
                  pl.BlockSpec((1, length, BRANCH_W), lambda i, b: (b, 0, 0)),
                  pl.BlockSpec((1, length, BRANCH_W), lambda i, b: (b, 0, 1))],
        out_specs=pl.BlockSpec((1, tm, BRANCH_W), lambda i, b: (b, i, 0)),
        out_shape=jax.ShapeDtypeStruct((bsz, length, BRANCH_W), BF16),
        compiler_params=_params(("parallel", "parallel")),
        name="dft_positions",
    )(jnp.asarray(c, dtype=F32).astype(BF16), jnp.asarray(-s, dtype=F32).astype(BF16), gcs, gcs)


def _merge_kernel(f0, f1, f2, f3, w_ref, g0, g1, g2, g3, o_ref):
    acc = None
    for i, (f, g) in enumerate(((f0, g0), (f1, g1), (f2, g2), (f3, g3))):
        term = jnp.dot(f[...], w_ref[i], preferred_element_type=F32) * g[...].astype(F32)
        acc = term if acc is None else acc + term
    o_ref[...] = acc.astype(o_ref.dtype)


def _merge(feat_plan, plan, feats, w_branch, gate):
    assert (feat_plan.tm, feat_plan.n) == (plan.tm, plan.n)
    tn = 1024
    fspec = feat_plan.spec(BRANCH_W, lambda i, j: (i, 0))
    gspec = lambda br: plan.spec(tn, lambda i, j: (i, br * D_MODEL + j * tn))
    return pl.pallas_call(
        _merge_kernel,
        grid=(plan.n, D_MODEL // tn),
        in_specs=[fspec] * 4 + [pl.BlockSpec((N_BRANCH, BRANCH_W, tn), lambda i, j: (0, 0, j))]
        + [gspec(br) for br in range(N_BRANCH)],
        out_specs=plan.spec(tn, lambda i, j: (i, j * tn)),
        out_shape=jax.ShapeDtypeStruct((plan.rows, D_MODEL), BF16),
        compiler_params=_params(("parallel", "parallel")),
        name="branch_merge",
    )(*feats, w_branch, gate, gate, gate, gate)


def _mixer(geom, h, xall, modtab, lp, w_in, layer, norm2_g, latent_only):
    src, dst = ("latent", "packed") if latent_only else ("all", "all")
    rows_all = _RowPlan(geom, 1024)
    proj = lambda lo, hi, dt, name, **kw: _matmul_w32(kw.pop("plan", rows_all), h, w_in, layer, dt,
                                                      cols=(lo, hi - lo), name=name, **kw)
    p1 = proj(0, CTX_STATE_COLS, F32, "in_proj_state")
    p2 = proj(O_G, O_FNO, F32, "in_proj_gq")
    p3 = proj(O_FNO, O_CONV, BF16, "in_proj_fno")
    p4 = proj(O_CONV, O_GATE, F32, "in_proj_conv")
    gate = proj(O_GATE, IN_W, BF16, "in_proj_gate", epilogue="sigmoid", plan=_RowPlan(geom, 1024, src),
                out_plan=_RowPlan(geom, 1024, dst))

    ins = _rwkv_prep(geom, p1, lp)
    y = _rwkv_scan(geom, ins)
    rw = _rwkv_readout(geom, y, ins, p2, lp)

    q, k, v = _rope(geom, p1, p2)
    att = _attention(geom, q, k, v, lp["att_sink"])

    cv = _conv(geom, p4, lp)

    p3 = p3.reshape(geom.batch, geom.lt, BRANCH_W)
    fno = jnp.concatenate([_fourier(p3[:, :geom.ctx_len]), _fourier(p3[:, geom.ctx_len:])], axis=1)
    fno = fno.reshape(geom.m, BRANCH_W)

    half_src, half_dst = _RowPlan(geom, 512, src), _RowPlan(geom, 512, dst)
    mixed = _merge(half_src, half_dst, (fno, rw, att, cv), lp["w_branch"].astype(BF16), gate)
    return _matmul(half_dst, mixed, lp["w_out"].astype(BF16), F32, epilogue="resid", res=xall, res_plan=half_src,
                   modtab=modtab, gate_row=2, norm=(norm2_g, modtab, (3, 4)), name="out_proj")


def kernel(x, c, ctx, c_ctx, ada_w, ada_b, norm1_g, norm2_g, w_in, rwkv_mu, rwkv_w0, rwkv_w_up, rwkv_a0, rwkv_a_up, rwkv_k_k, rwkv_k_a, rwkv_r_k, rwkv_g_up, rwkv_lnx_g, rwkv_lnx_b, att_sink, conv_dw, conv_dw_b, conv_ln_g, conv_ln_b, w_branch, w_out, w_mlp1, w_mlp2, final_g):
    batch, seq, _ = x.shape
    geom = _Geom(batch, ctx.shape[1], seq)
    depth = w_in.shape[0]
    assert batch + 1 <= 8
    cond = jnp.zeros((8, D_MODEL), F32).at[:batch].set(c).at[batch].set(c_ctx)
    xall = jnp.concatenate([ctx, x], axis=1).reshape(geom.m, D_MODEL)
    modtabs = []
    for l in range(depth):
        mod = _ada_mod(cond, ada_w, l, ada_b[l])
        mod_x = mod[:batch].reshape(batch, 1, 6, D_MODEL)
        mod_c = jnp.broadcast_to(mod[batch].reshape(1, 1, 6, D_MODEL), (batch, 1, 6, D_MODEL))
        modtabs.append(jnp.concatenate([mod_c, mod_x], axis=1).reshape(2 * batch, 6, D_MODEL))
    h = _norm_mod(geom, xall, norm1_g[0], modtabs[0], rows=(0, 1))
    for l in range(depth):
        modtab = modtabs[l]
        lp = {
            "rwkv_mu": rwkv_mu[l], "rwkv_w0": rwkv_w0[l], "rwkv_w_up": rwkv_w_up[l],
            "rwkv_a0": rwkv_a0[l], "rwkv_a_up": rwkv_a_up[l], "rwkv_k_k": rwkv_k_k[l],
            "rwkv_k_a": rwkv_k_a[l], "rwkv_r_k": rwkv_r_k[l], "rwkv_g_up": rwkv_g_up[l],
            "rwkv_lnx_g": rwkv_lnx_g[l], "rwkv_lnx_b": rwkv_lnx_b[l], "att_sink": att_sink[l],
            "conv_dw": conv_dw[l], "conv_dw_b": conv_dw_b[l], "conv_ln_g": conv_ln_g[l],
            "conv_ln_b": conv_ln_b[l], "w_branch": w_branch[l], "w_out": w_out[l],
        }
        last = l + 1 == depth
        xall, h2 = _mixer(geom, h, xall, modtab, lp, w_in, l, norm2_g[l], latent_only=last)
        kind = "packed" if last else "all"
        hid = _matmul_w32(_RowPlan(geom, 1024, kind), h2, w_mlp1, l, BF16, epilogue="relu2", name="mlp_up")
        down = functools.partial(_matmul, a=hid, b=w_mlp2[l].astype(BF16), out_dtype=F32, epilogue="resid",
                                 res=xall, modtab=modtab, gate_row=5, name="mlp_down")
        if last:
            xall = down(_RowPlan(geom, 1024, kind))
        else:
            xall, h = down(_RowPlan(geom, 512), norm=(norm1_g[l + 1], modtabs[l + 1], (0, 1)))
    return _final_norm(geom, xall, final_g).reshape(batch, seq, D_MODEL)
```

```python
import functools
import math

import numpy as np
import jax
import jax.numpy as jnp
from jax import lax
from jax.experimental import pallas as pl
from jax.experimental.pallas import tpu as pltpu

F32 = jnp.float32
BF16 = jnp.bfloat16

D_MODEL = 2048
GRID_W = 64
NORM_EPS = 1e-6
N_BRANCH = 4
BRANCH_W = D_MODEL // N_BRANCH
FNO_GROUPS = 4
FNO_GROUP_W = BRANCH_W // FNO_GROUPS
RWKV_HEAD = 64
RWKV_HEADS = BRANCH_W // RWKV_HEAD
N_DIR = 2
DECAY_LORA = 64
AICL_LORA = 64
GATE_LORA = 128
DIR_LORA_W = DECAY_LORA + AICL_LORA
SHIFT_W = 3 * BRANCH_W + DIR_LORA_W
GN_EPS = 64e-5
ATT_HEAD = 64
ATT_HEADS = BRANCH_W // ATT_HEAD
ATT_KV_HEADS = 2
ATT_REP = ATT_HEADS // ATT_KV_HEADS
ATT_KV_W = ATT_KV_HEADS * ATT_HEAD
WINDOW = 128
BLOCK = 128
ROPE_BASE = 10000.0
NEG_INF = -1e30
CONV_K = 31
CONV_PAD = (CONV_K - 1) // 2
LN_EPS = 1e-5
MLP_HIDDEN = 4 * D_MODEL

O_LORA = 3 * BRANCH_W
O_KV = O_LORA + N_DIR * DIR_LORA_W
CTX_STATE_COLS = O_KV + 2 * ATT_KV_W
O_G = CTX_STATE_COLS
O_Q = O_G + GATE_LORA
O_FNO = O_Q + BRANCH_W
O_CONV = O_FNO + BRANCH_W
O_GATE = O_CONV + 2 * BRANCH_W
IN_W = O_GATE + N_BRANCH * D_MODEL

LANES = 128
ROW_TILE = 256
SCAN_CHUNK = 64
PAIR_W = 2 * RWKV_HEAD
N_PAIRS = BRANCH_W // PAIR_W
HALO = 16
ATT_SLAB = 64
SUBLANES = 8
MXU_COLS = 256
CAST_ROWS = 256
CONV_ROWS = 32
PREP_ROWS = 64
SCAN_STEPS = 2
VMEM_LIMIT = 56 * 1024 * 1024

NT_DIMS = (((1,), (1,)), ((), ()))
NN_DIMS = (((1,), (0,)), ((), ()))
TN_DIMS = (((0,), (0,)), ((), ()))


def _params(sem):
    return pltpu.CompilerParams(dimension_semantics=sem, vmem_limit_bytes=VMEM_LIMIT)


def _split2(a):
    hi = a.astype(BF16)
    lo = (a - hi.astype(F32)).astype(BF16)
    return hi, lo


def _dot3(a, b, dims=NN_DIMS):
    ah, al = _split2(a)
    bh, bl = _split2(b)
    dg = functools.partial(lax.dot_general, dimension_numbers=dims, preferred_element_type=F32)
    return dg(ah, bh) + (dg(ah, bl) + dg(al, bh))


def _dot_sel(a, sel_bf16):
    hi, lo = _split2(a)
    n = a.shape[0]
    both = jnp.dot(jnp.concatenate([hi, lo], axis=0), sel_bf16, preferred_element_type=F32)
    return both[:n] + both[n:]


def _sigmoid(x):
    return 1.0 / (1.0 + jnp.exp(-x))


def _ada_kernel(a_ref, w_ref, b_ref, o_ref):
    a = a_ref[...]
    s = a * _sigmoid(a)
    o_ref[...] = jnp.dot(s, w_ref[0], preferred_element_type=F32,
                         precision=lax.Precision.HIGHEST) + b_ref[...]


def _ada_mod(cond, w, layer, b):
    n = w.shape[2]
    tn = 1024
    return pl.pallas_call(
        _ada_kernel,
        grid=(n // tn,),
        in_specs=[pl.BlockSpec((8, D_MODEL), lambda j: (0, 0)),
                  pl.BlockSpec((1, D_MODEL, tn), lambda j: (layer, 0, j)),
                  pl.BlockSpec((1, tn), lambda j: (0, j))],
        out_specs=pl.BlockSpec((8, tn), lambda j: (0, j)),
        out_shape=jax.ShapeDtypeStruct((8, n), F32),
        compiler_params=_params(("parallel",)),
        name="ada_mod",
    )(cond, w, b.reshape(1, n))


class _Geom:
    def __init__(self, batch, ctx_len, seq):
        assert ctx_len % ROW_TILE == 0 and seq % ROW_TILE == 0
        assert seq % GRID_W == 0 and seq % BLOCK == 0 and ctx_len % BLOCK == 0
        self.batch = batch
        self.ctx_len = ctx_len
        self.seq = seq
        self.lt = ctx_len + seq
        self.m = batch * self.lt
        self.tiles_b = self.lt // ROW_TILE
        self.ctx_tiles = ctx_len // ROW_TILE
        self.tiles = batch * self.tiles_b

    def mod_row(self, i):
        return 2 * (i // self.tiles_b) + ((i % self.tiles_b) >= self.ctx_tiles).astype(jnp.int32)


def _norm_kernel(x_ref, g_ref, *rest, rows):
    x = x_ref[...]
    y = x * lax.rsqrt(jnp.mean(x * x, axis=-1, keepdims=True) + NORM_EPS) * g_ref[...]
    if rows is None:
        (o_ref,) = rest
    else:
        mod_ref, o_ref = rest
        mod = mod_ref[0]
        y = y * (1.0 + mod[rows[1]:rows[1] + 1]) + mod[rows[0]:rows[0] + 1]
    o_ref[...] = y.astype(o_ref.dtype)


def _norm_mod(geom, x, g, modtab, rows):
    return pl.pallas_call(
        functools.partial(_norm_kernel, rows=rows),
        grid=(geom.tiles,),
        in_specs=[pl.BlockSpec((ROW_TILE, D_MODEL), lambda i: (i, 0)),
                  pl.BlockSpec((1, D_MODEL), lambda i: (0, 0)),
                  pl.BlockSpec((1, 6, D_MODEL), lambda i: (geom.mod_row(i), 0, 0))],
        out_specs=pl.BlockSpec((ROW_TILE, D_MODEL), lambda i: (i, 0)),
        out_shape=jax.ShapeDtypeStruct((geom.m, D_MODEL), BF16),
        compiler_params=_params(("parallel",)),
        name="norm_mod",
    )(x, g.reshape(1, D_MODEL), modtab)


def _final_norm(geom, x, g):
    return pl.pallas_call(
        functools.partial(_norm_kernel, rows=None),
        grid=(geom.batch * geom.seq // ROW_TILE,),
        in_specs=[pl.BlockSpec((ROW_TILE, D_MODEL), lambda i: (i, 0)),
                  pl.BlockSpec((1, D_MODEL), lambda i: (0, 0))],
        out_specs=pl.BlockSpec((ROW_TILE, D_MODEL), lambda i: (i, 0)),
        out_shape=jax.ShapeDtypeStruct((geom.batch * geom.seq, D_MODEL), F32),
        compiler_params=_params(("parallel",)),
        name="final_norm",
    )(x, g.reshape(1, D_MODEL))


def _mm_store(acc, res_ref, mod_refs, o_ref, epilogue, gate_row, norm=None, cs=slice(None)):
    if epilogue == "resid":
        for s, mod_ref in enumerate(mod_refs):
            rs = slice(s * ROW_TILE, (s + 1) * ROW_TILE)
            xn = res_ref[rs, cs] + mod_ref[0, gate_row:gate_row + 1, cs] * acc[rs]
            o_ref[rs, cs] = xn
            if norm is not None:
                g_ref, nmod_refs, h_ref, rows = norm
                nmod = nmod_refs[s][0]
                y = xn * lax.rsqrt(jnp.mean(xn * xn, axis=-1, keepdims=True) + NORM_EPS) * g_ref[...]
                h_ref[rs] = (y * (1.0 + nmod[rows[1]:rows[1] + 1]) + nmod[rows[0]:rows[0] + 1]).astype(h_ref.dtype)
        return
    if epilogue == "sigmoid":
        acc = _sigmoid(acc)
    elif epilogue == "relu2":
        acc = jnp.square(jnp.maximum(acc, 0.0))
    o_ref[:, cs] = acc.astype(o_ref.dtype)


def _col_chunks(tn, norm):
    if norm is not None or tn % MXU_COLS:
        return [slice(None)]
    return [slice(c, c + MXU_COLS) for c in range(0, tn, MXU_COLS)]


def _mm_kernel(a_ref, b_ref, *rest, epilogue, gate_row, nk, n_sub, norm_rows):
    rest = list(rest)
    res_ref, mod_refs, norm = None, (), None
    if epilogue == "resid":
        res_ref, mod_refs, rest = rest[0], rest[1:1 + n_sub], rest[1 + n_sub:]
        if norm_rows is not None:
            norm = (rest[0], rest[1:1 + n_sub], rest[2 + n_sub], norm_rows)
            rest = [rest[1 + n_sub]] + rest[3 + n_sub:]
    o_ref = rest[0]
    chunks = _col_chunks(o_ref.shape[1], norm)
    if nk == 1:
        for cs in chunks:
            acc = jnp.dot(a_ref[...], b_ref[:, cs], preferred_element_type=F32)
            _mm_store(acc, res_ref, mod_refs, o_ref, epilogue, gate_row, norm, cs)
        return
    acc_ref = rest[1]
    k = pl.program_id(2)

    @pl.when(k == 0)
    def _():
        acc_ref[...] = jnp.dot(a_ref[...], b_ref[...], preferred_element_type=F32)

    @pl.when((k > 0) & (k < nk - 1))
    def _():
        acc_ref[...] += jnp.dot(a_ref[...], b_ref[...], preferred_element_type=F32)

    @pl.when(k == nk - 1)
    def _():
        for cs in chunks:
            acc = acc_ref[:, cs] + jnp.dot(a_ref[...], b_ref[:, cs], preferred_element_type=F32)
            _mm_store(acc, res_ref, mod_refs, o_ref, epilogue, gate_row, norm, cs)


def _mm_w32_kernel(a_ref, w_ref, *rest, epilogue, gate_row):
    res_ref, mod_refs = (rest[0], rest[1:-2]) if epilogue == "resid" else (None, ())
    o_ref, wb_ref = rest[-2:]

    @pl.when(pl.program_id(1) == 0)
    def _():
        def cast_rows(r, carry):
            rows = pl.ds(pl.multiple_of(r * CAST_ROWS, CAST_ROWS), CAST_ROWS)
            wb_ref[rows, :] = w_ref[0, rows, :].astype(BF16)
            return carry

        lax.fori_loop(0, wb_ref.shape[0] // CAST_ROWS, cast_rows, 0)

    for cs in _col_chunks(o_ref.shape[1], None):
        acc = jnp.dot(a_ref[...], wb_ref[:, cs], preferred_element_type=F32)
        _mm_store(acc, res_ref, mod_refs, o_ref, epilogue, gate_row, cs=cs)


def _pick_tile(n, cap, unit=LANES):
    t = (min(cap, n) // unit) * unit
    while n % t:
        t -= unit
    return t


class _RowPlan:
    def __init__(self, geom, tm_cap, kind="all", rows=None):
        if kind in ("latent", "packed"):
            self.tm = _pick_tile(geom.seq, tm_cap, ROW_TILE)
            per_b = geom.seq // self.tm
            self.n = geom.batch * per_b
            self.rows = geom.m if kind == "latent" else geom.batch * geom.seq
            if kind == "latent":
                self.off = lambda i: (i // per_b) * geom.lt + geom.ctx_len + (i % per_b) * self.tm
                self.mod_row = geom.mod_row
            else:
                self.off = lambda i: i * self.tm
                self.mod_row = lambda t: 2 * (t // (geom.seq // ROW_TILE)) + 1
        else:
            self.rows = geom.m if kind == "all" else rows
            self.tm = _pick_tile(self.rows, tm_cap, ROW_TILE)
            self.n = self.rows // self.tm
            self.off = lambda i: i * self.tm
            self.mod_row = geom.mod_row if kind == "all" else None
        self.n_sub = self.tm // ROW_TILE

    def spec(self, width, at):
        def index(*g):
            i, col = at(*g)
            col = col if isinstance(col, int) else pl.multiple_of(col, LANES)
            return pl.multiple_of(self.off(i), ROW_TILE), col
        return pl.BlockSpec((pl.Element(self.tm), pl.Element(width)), index)

    def mod_specs(self, tn, at):
        return [pl.BlockSpec((1, 6, tn), lambda *g, s=s: (
            self.mod_row(self.off(at(*g)[0]) // ROW_TILE + s), 0, at(*g)[1])) for s in range(self.n_sub)]


def _matmul(plan, a, b, out_dtype, epilogue="none", res=None, modtab=None, gate_row=0, norm=None,
            res_plan=None, out_plan=None, tn_cap=1024, tk_cap=2048, name="matmul"):
    kdim = a.shape[1]
    n = b.shape[1]
    out_plan = out_plan or plan
    res_plan = res_plan or out_plan
    tm, m = plan.tm, out_plan.rows
    assert (out_plan.tm, out_plan.n, res_plan.tm, res_plan.n) == (tm, plan.n, tm, plan.n)
    tn = n if norm is not None else _pick_tile(n, tn_cap)
    tk = _pick_tile(kdim, tk_cap)
    nk = kdim // tk
    row_j = lambda i, j, k: (i, j * tn)
    tile_ij = lambda i, j, k: (i, j)
    in_specs = [plan.spec(tk, lambda i, j, k: (i, k * tk)),
                pl.BlockSpec((tk, tn), lambda i, j, k: (k, j))]
    args = [a, b]
    out_specs = out_plan.spec(tn, row_j)
    out_shape = jax.ShapeDtypeStruct((m, n), out_dtype)
    if epilogue == "resid":
        in_specs += [res_plan.spec(tn, row_j)] + out_plan.mod_specs(tn, tile_ij)
        args += [res] + [modtab] * plan.n_sub
        if norm is not None:
            gain, nmodtab, _ = norm
            in_specs += [pl.BlockSpec((1, tn), lambda i, j, k: (0, 0))] + out_plan.mod_specs(tn, tile_ij)
            args += [gain.reshape(1, n)] + [nmodtab] * plan.n_sub
            out_specs = [out_specs, out_plan.spec(tn, row_j)]
            out_shape = [out_shape, jax.ShapeDtypeStruct((m, n), BF16)]
    return pl.pallas_call(
        functools.partial(_mm_kernel, epilogue=epilogue, gate_row=gate_row, nk=nk, n_sub=plan.n_sub,
                          norm_rows=None if norm is None else norm[2]),
        grid=(plan.n, n // tn, nk),
        in_specs=in_specs,
        out_specs=out_specs,
        out_shape=out_shape,
        scratch_shapes=[pltpu.VMEM((tm, tn), F32)] if nk > 1 else [],
        compiler_params=_params(("parallel", "parallel", "arbitrary")),
        name=name,
    )(*args)


def _matmul_w32(plan, a, w, layer, out_dtype, epilogue="none", cols=None, out_plan=None, tn_cap=1024,
                name="matmul_w32"):
    kdim = a.shape[1]
    col0, n = cols if cols is not None else (0, w.shape[2])
    assert col0 % LANES == 0 and epilogue != "resid"
    out_plan = out_plan or plan
    assert (out_plan.tm, out_plan.n) == (plan.tm, plan.n)
    tn = _pick_tile(n, tn_cap)
    in_specs = [plan.spec(kdim, lambda j, i: (i, 0)),
                pl.BlockSpec((pl.Element(1), pl.Element(kdim), pl.Element(tn)),
                             lambda j, i: (layer, 0, pl.multiple_of(col0 + j * tn, LANES)))]
    args = [a, w]
    return pl.pallas_call(
        functools.partial(_mm_w32_kernel, epilogue=epilogue, gate_row=0),
        grid=(n // tn, plan.n),
        in_specs=in_specs,
        out_specs=out_plan.spec(tn, lambda j, i: (i, j * tn)),
        out_shape=jax.ShapeDtypeStruct((out_plan.rows, n), out_dtype),
        scratch_shapes=[pltpu.VMEM((kdim, tn), BF16)],
        compiler_params=_params(("parallel", "arbitrary")),
        name=name,
    )(*args)


def _head_sum_matrix(width, head):
    idx = np.arange(width) // head
    return jnp.asarray((idx[:, None] == idx[None, :]).astype(np.float32), dtype=BF16)


def _prep_kernel(p_ref, lora_ref, halo_ref, hlora_ref, mu_ref, w0_ref, wup_ref, a0_ref, aup_ref,
                 kk_ref, ka_ref, hs_ref, r_o, lw_o, k_o, v_o, kkn_o, b_o, *, tiles_b, ctx_tiles):
    d = pl.program_id(0)
    j = pl.program_id(1) % tiles_b
    fwd = d == 0
    t = p_ref.shape[0]
    rkv_w = 3 * BRANCH_W
    at_start = (j == 0) | (j == ctx_tiles)
    at_end = (j == ctx_tiles - 1) | (j == tiles_b - 1)
    no_edge = (fwd & at_start) | (jnp.logical_not(fwd) & at_end)
    row = lax.broadcasted_iota(jnp.int32, (t, 1), 0)
    edge_row = jnp.where(fwd, 0, t - 1)

    def neighbour(x_ref, h_ref):
        edge = jnp.where(no_edge, 0.0, jnp.where(fwd, h_ref[7:8], h_ref[0:1]))
        x = x_ref[...]
        return jnp.where(row == edge_row, edge, jnp.where(fwd, pltpu.roll(x, 1, 0), pltpu.roll(x, t - 1, 0)))

    nb_rkv = neighbour(p_ref, halo_ref)
    nb_lora = neighbour(lora_ref, hlora_ref)
    mu = mu_ref[0]
    for c in range(t // PREP_ROWS):
        rs = slice(c * PREP_ROWS, (c + 1) * PREP_ROWS)
        f = p_ref[rs]
        f = f + mu[:, :rkv_w] * (nb_rkv[rs] - f)
        lo = lora_ref[rs]
        lo = lo + mu[:, rkv_w:] * (nb_lora[rs] - lo)
        r = f[:, 0:BRANCH_W]
        k = f[:, BRANCH_W:2 * BRANCH_W]
        v = f[:, 2 * BRANCH_W:rkv_w]
        w_raw = w0_ref[0] + _dot3(jnp.tanh(lo[:, :DECAY_LORA]), wup_ref[0])
        lw = -math.exp(-0.5) * _sigmoid(w_raw)
        a = _sigmoid(a0_ref[0] + _dot3(lo[:, DECAY_LORA:], aup_ref[0]))
        kk = k * kk_ref[...]
        norm = jnp.sqrt(_dot_sel(kk * kk, hs_ref[...]))
        kk = kk / jnp.maximum(norm, 1e-12)
        r_o[0, rs] = r
        lw_o[0, rs] = lw
        k_o[0, rs] = k * (1.0 + (a - 1.0) * ka_ref[...])
        v_o[0, rs] = v
        kkn_o[0, rs] = kk
        b_o[0, rs] = kk * a


def _rwkv_prep(geom, p1, lp):
    m = geom.m
    t = ROW_TILE
    tb = geom.tiles_b
    rkv_w = 3 * BRANCH_W
    lora_blk0 = O_LORA // DIR_LORA_W
    n_blk8 = m // 8

    def halo_idx(d, i):
        before = jnp.maximum(i * (t // 8) - 1, 0)
        after = jnp.minimum((i + 1) * (t // 8), n_blk8 - 1)
        return jnp.where(d == 0, before, after)

    out = jax.ShapeDtypeStruct((N_DIR, m, BRANCH_W), F32)
    ospec = pl.BlockSpec((1, t, BRANCH_W), lambda d, i: (d, i, 0))
    vec = lambda a: a.reshape(1, BRANCH_W)
    dvec = pl.BlockSpec((1, 1, BRANCH_W), lambda d, i: (d, 0, 0))
    return pl.pallas_call(
        functools.partial(_prep_kernel, tiles_b=tb, ctx_tiles=geom.ctx_tiles),
        grid=(N_DIR, geom.tiles),
        in_specs=[pl.BlockSpec((t, rkv_w), lambda d, i: (i, 0)),
                  pl.BlockSpec((t, DIR_LORA_W), lambda d, i: (i, lora_blk0 + d)),
                  pl.BlockSpec((8, rkv_w), lambda d, i: (halo_idx(d, i), 0)),
                  pl.BlockSpec((8, DIR_LORA_W), lambda d, i: (halo_idx(d, i), lora_blk0 + d)),
                  pl.BlockSpec((1, 1, SHIFT_W), lambda d, i: (d, 0, 0)),
                  dvec,
                  pl.BlockSpec((1, DECAY_LORA, BRANCH_W), lambda d, i: (d, 0, 0)),
                  dvec,
                  pl.BlockSpec((1, AICL_LORA, BRANCH_W), lambda d, i: (d, 0, 0)),
                  pl.BlockSpec((1, BRANCH_W), lambda d, i: (0, 0)),
                  pl.BlockSpec((1, BRANCH_W), lambda d, i: (0, 0)),
                  pl.BlockSpec((BRANCH_W, BRANCH_W), lambda d, i: (0, 0))],
        out_specs=[ospec] * 6,
        out_shape=[out] * 6,
        compiler_params=_params(("parallel", "parallel")),
        name="rwkv_prep",
    )(p1, p1, p1, p1, lp["rwkv_mu"].reshape(N_DIR, 1, SHIFT_W),
      lp["rwkv_w0"].reshape(N_DIR, 1, BRANCH_W), lp["rwkv_w_up"],
      lp["rwkv_a0"].reshape(N_DIR, 1, BRANCH_W), lp["rwkv_a_up"],
      vec(lp["rwkv_k_k"]), vec(lp["rwkv_k_a"]), _head_sum_matrix(BRANCH_W, RWKV_HEAD))


def _scan_kernel(*refs):
    yf_ref, yb_ref, h_ref = refs[12:]
    c = pl.program_id(1)
    C = SCAN_CHUNK
    W = 2 * C

    @pl.when(c == 0)
    def _():
        h_ref[...] = jnp.zeros_like(h_ref)

    rr = lax.broadcasted_iota(jnp.int32, (C, C), 0)
    cc = lax.broadcasted_iota(jnp.int32, (C, C), 1)
    lane = lax.broadcasted_iota(jnp.int32, (1, PAIR_W), 1)
    m_a = (lane < RWKV_HEAD).astype(F32)
    m_b = 1.0 - m_a
    r2 = lax.broadcasted_iota(jnp.int32, (W, W), 0)
    c2 = lax.broadcasted_iota(jnp.int32, (W, W), 1)
    same = (r2 // C) == (c2 // C)
    eye = (r2 == c2).astype(F32)

    def pairs(x):
        return [jnp.concatenate([x[:, p * PAIR_W:(p + 1) * PAIR_W] * m_a,
                                 x[:, p * PAIR_W:(p + 1) * PAIR_W] * m_b], axis=0) for p in range(N_PAIRS)]

    strict, incl, tri = [], [], []
    for sgn in (1, -1):
        dt = (r2 % C - c2 % C) * sgn
        strict += [(same & (dt > 0)).astype(F32)] * N_PAIRS
        incl += [(same & (dt >= 0)).astype(F32)] * N_PAIRS
        tri.append(jnp.where((rr - cc) * sgn >= 0, 1.0, 0.0).astype(BF16))
    strict = jnp.stack(strict)
    incl = jnp.stack(incl)

    h = h_ref[...]
    for step in range(SCAN_STEPS):
        rows = (pl.ds(step * C, C), pl.ds((SCAN_STEPS - 1 - step) * C, C))
        h = _scan_chunk(refs[:12], rows, tri, strict, incl, eye, pairs, h, (yf_ref, yb_ref))
    h_ref[...] = h


def _scan_chunk(in_refs, rows, tri, strict, incl, eye, pairs, h0, y_refs):
    C = SCAN_CHUNK
    W = 2 * C
    stacks = {name: [] for name in ("a", "b", "k", "r", "v", "bc", "kc", "pt")}
    for d in range(N_DIR):
        r_ref, lw_ref, k_ref, v_ref, kk_ref, b_ref = in_refs[d::N_DIR]
        rs = rows[d]
        lw = lw_ref[0, rs]
        lp_in = _dot_sel_lhs(tri[d], lw)
        tot = jnp.sum(lw, axis=0, keepdims=True)
        e_neg = jnp.exp(-lp_in)
        e_chk = jnp.exp(tot - lp_in)
        p_tot = jnp.exp(tot)
        stacks["a"] += pairs(-kk_ref[0, rs] * jnp.exp(lp_in - lw))
        stacks["b"] += pairs(b_ref[0, rs] * e_neg)
        stacks["k"] += pairs(k_ref[0, rs] * e_neg)
        stacks["r"] += pairs(r_ref[0, rs] * jnp.exp(lp_in))
        stacks["v"] += pairs(v_ref[0, rs])
        stacks["bc"] += pairs(b_ref[0, rs] * e_chk)
        stacks["kc"] += pairs(k_ref[0, rs] * e_chk)
        stacks["pt"] += [p_tot[:, p * PAIR_W:(p + 1) * PAIR_W] for p in range(N_PAIRS)]
    a_s, b_s, k_s, r_s, v_s, bc_s, kc_s, p_tot = (jnp.stack(stacks[n]) for n in
                                                  ("a", "b", "k", "r", "v", "bc", "kc", "pt"))

    a_b, b_b, k_b, r_b, v_b, bc_b, kc_b = (x.astype(BF16) for x in (a_s, b_s, k_s, r_s, v_s, bc_s, kc_s))
    big = _bdot(jnp.concatenate([a_b, r_b], axis=1), jnp.concatenate([b_b, k_b], axis=1), BNT_DIMS)
    l_ab = big[:, :W, :W] * strict
    l_ak = big[:, :W, W:] * strict
    m_rb = (big[:, W:, :W] * incl).astype(BF16)
    m_rk = big[:, W:, W:] * incl
    t_inv = eye + l_ab
    pw_b = l_ab.astype(BF16)
    pw_b = _bdot(pw_b, pw_b).astype(BF16)
    for _ in range(int(math.log2(C)) - 2):
        both = _bdot(jnp.concatenate([t_inv.astype(BF16), pw_b], axis=1), pw_b)
        t_inv = t_inv + both[:, :W]
        pw_b = both[:, W:].astype(BF16)
    t_inv = t_inv + _bdot(t_inv, pw_b)
    x1 = _bdot(t_inv, jnp.concatenate([a_b, _bdot(l_ak, v_b).astype(BF16)], axis=2)).astype(BF16)
    x2 = _bdot(m_rb, x1)
    r_hat = r_s + x2[:, :, :PAIR_W]
    y0 = x2[:, :, PAIR_W:] + _bdot(m_rk, v_b)
    x3 = _bdot(bc_b, x1, BTN_DIMS)
    g = eye * p_tot + x3[:, :, :PAIR_W]
    h_inc = x3[:, :, PAIR_W:] + _bdot(kc_b, v_b, BTN_DIMS)
    x4 = _bdot(jnp.concatenate([r_hat, g], axis=1), h0)
    ys = x4[:, :W] + y0
    for d, y_ref in enumerate(y_refs):
        for p in range(N_PAIRS):
            y_ref[rows[d], p * PAIR_W:(p + 1) * PAIR_W] = ys[d * N_PAIRS + p, :C] + ys[d * N_PAIRS + p, C:]
    return x4[:, W:] + h_inc


BNN_DIMS = (((2,), (1,)), ((0,), (0,)))
BNT_DIMS = (((2,), (2,)), ((0,), (0,)))
BTN_DIMS = (((1,), (1,)), ((0,), (0,)))


def _bdot(a, b, dims=BNN_DIMS):
    return lax.dot_general(a.astype(BF16), b.astype(BF16), dims, preferred_element_type=F32)


def _dot_sel_lhs(sel_bf16, a):
    hi = a.astype(BF16)
    r1 = a - hi.astype(F32)
    mid = r1.astype(BF16)
    lo = (r1 - mid.astype(F32)).astype(BF16)
    dg = functools.partial(jnp.dot, preferred_element_type=F32)
    return dg(sel_bf16, hi) + (dg(sel_bf16, mid) + dg(sel_bf16, lo))


def _rwkv_scan(geom, ins):
    C = SCAN_CHUNK * SCAN_STEPS
    assert geom.ctx_len % C == 0 and geom.seq % C == 0
    nch = geom.lt // C
    nctx = geom.ctx_len // C

    def rev(c):
        return jnp.where(c < nctx, nctx - 1 - c, nch - 1 + nctx - c)

    fwd = pl.BlockSpec((1, C, BRANCH_W), lambda b, c: (0, b * nch + c, 0))
    bwd = pl.BlockSpec((1, C, BRANCH_W), lambda b, c: (1, b * nch + rev(c), 0))
    out = jax.ShapeDtypeStruct((geom.m, BRANCH_W), F32)
    return pl.pallas_call(
        _scan_kernel,
        grid=(geom.batch, nch),
        in_specs=[fwd, bwd] * 6,
        out_specs=[pl.BlockSpec((C, BRANCH_W), lambda b, c: (b * nch + c, 0)),
                   pl.BlockSpec((C, BRANCH_W), lambda b, c: (b * nch + rev(c), 0))],
        out_shape=[out, out],
        scratch_shapes=[pltpu.VMEM((N_DIR * N_PAIRS, PAIR_W, PAIR_W), F32)],
        compiler_params=_params(("parallel", "arbitrary")),
        name="rwkv_scan",
    )(*[a for a in ins for _ in range(N_DIR)])


def _readout_kernel(yf_ref, yb_ref, r_ref, k_ref, v_ref, p2_ref, gup_ref, rk_ref, lg_ref, lb_ref, hs_ref, o_ref):
    hs = hs_ref[...]
    y = yf_ref[...] + yb_ref[...]
    inv_n = 1.0 / RWKV_HEAD
    mean = _dot_sel(y, hs) * inv_n
    yc = y - mean
    var = _dot_sel(yc * yc, hs) * inv_n
    yn = yc * lax.rsqrt(var + GN_EPS) * lg_ref[...] + lb_ref[...]
    bonus = jnp.zeros_like(y)
    for d in range(N_DIR):
        bonus = bonus + _dot_sel(r_ref[d] * k_ref[d] * rk_ref[d:d + 1], hs) * v_ref[d]
    g = _dot3(_sigmoid(p2_ref[:, 0:GATE_LORA]), gup_ref[...])
    o_ref[...] = ((yn + bonus) * g).astype(o_ref.dtype)


def _rwkv_readout(geom, y, ins, p2, lp):
    t = ROW_TILE
    dspec = pl.BlockSpec((N_DIR, t, BRANCH_W), lambda i: (0, i, 0))
    vspec = pl.BlockSpec((1, BRANCH_W), lambda i: (0, 0))
    r, _, k, v, _, _ = ins
    return pl.pallas_call(
        _readout_kernel,
        grid=(geom.tiles,),
        in_specs=[pl.BlockSpec((t, BRANCH_W), lambda i: (i, 0)),
                  pl.BlockSpec((t, BRANCH_W), lambda i: (i, 0)),
                  dspec, dspec, dspec,
                  pl.BlockSpec((t, GATE_LORA + BRANCH_W), lambda i: (i, 0)),
                  pl.BlockSpec((GATE_LORA, BRANCH_W), lambda i: (0, 0)),
                  pl.BlockSpec((N_DIR, BRANCH_W), lambda i: (0, 0)),
                  vspec, vspec,
                  pl.BlockSpec((BRANCH_W, BRANCH_W), lambda i: (0, 0))],
        out_specs=pl.BlockSpec((t, BRANCH_W), lambda i: (i, 0)),
        out_shape=jax.ShapeDtypeStruct((geom.m, BRANCH_W), BF16),
        compiler_params=_params(("parallel",)),
        name="rwkv_readout",
    )(y[0], y[1], r, k, v, p2, lp["rwkv_g_up"], lp["rwkv_r_k"].reshape(N_DIR, BRANCH_W),
      lp["rwkv_lnx_g"].reshape(1, BRANCH_W), lp["rwkv_lnx_b"].reshape(1, BRANCH_W),
      _head_sum_matrix(BRANCH_W, RWKV_HEAD))


def _rope_tables(geom):
    half = ATT_HEAD // 2
    nf = half // 2
    inv = ROPE_BASE ** (-jnp.arange(nf, dtype=F32) / nf)
    pos = jnp.arange(geom.seq, dtype=jnp.int32)
    row_ang = (pos // GRID_W).astype(F32)[:, None] * inv[None, :]
    col_ang = (pos % GRID_W).astype(F32)[:, None] * inv[None, :]
    cos = jnp.concatenate([jnp.cos(row_ang)] * 2 + [jnp.cos(col_ang)] * 2, axis=-1)
    sin = jnp.concatenate([-jnp.sin(row_ang), jnp.sin(row_ang), -jnp.sin(col_ang), jnp.sin(col_ang)], axis=-1)
    cos = jnp.concatenate([jnp.ones((geom.ctx_len, ATT_HEAD), F32), cos], axis=0)
    sin = jnp.concatenate([jnp.zeros((geom.ctx_len, ATT_HEAD), F32), sin], axis=0)
    return cos, sin


def _rotate(t, cos, sin):
    w = t.shape[-1]
    nf = ATT_HEAD // 4
    lane = lax.broadcasted_iota(jnp.int32, (1, w), 1)
    partner = jnp.where((lane % (2 * nf)) < nf, pltpu.roll(t, w - nf, 1), pltpu.roll(t, nf, 1))
    return t * cos + partner * sin


def _rope_kernel(p2_ref, kv_ref, cq_ref, sq_ref, ck_ref, sk_ref, q_o, k_o, v_o):
    q = p2_ref[:, GATE_LORA:]
    q_o[...] = (_rotate(q, cq_ref[...], sq_ref[...]) * (ATT_HEAD ** -0.5)).astype(q_o.dtype)
    kv = kv_ref[...]
    k_o[...] = _rotate(kv[:, :ATT_KV_W], ck_ref[...], sk_ref[...]).astype(k_o.dtype)
    v_o[...] = kv[:, ATT_KV_W:].astype(v_o.dtype)


def _rope(geom, p1, p2):
    t = ROW_TILE
    tb = geom.tiles_b
    cos, sin = _rope_tables(geom)
    cq, sq = jnp.tile(cos, (1, ATT_HEADS)), jnp.tile(sin, (1, ATT_HEADS))
    ck, sk = jnp.tile(cos, (1, ATT_KV_HEADS)), jnp.tile(sin, (1, ATT_KV_HEADS))
    qspec = pl.BlockSpec((t, BRANCH_W), lambda i: (i % tb, 0))
    kspec = pl.BlockSpec((t, ATT_KV_W), lambda i: (i % tb, 0))
    m = geom.m
    return pl.pallas_call(
        _rope_kernel,
        grid=(geom.tiles,),
        in_specs=[pl.BlockSpec((t, GATE_LORA + BRANCH_W), lambda i: (i, 0)),
                  pl.BlockSpec((t, 2 * ATT_KV_W), lambda i: (i, O_KV // (2 * ATT_KV_W))),
                  qspec, qspec, kspec, kspec],
        out_specs=[pl.BlockSpec((t, BRANCH_W), lambda i: (i, 0)),
                   pl.BlockSpec((t, ATT_KV_W), lambda i: (i, 0)),
                   pl.BlockSpec((t, ATT_KV_W), lambda i: (i, 0))],
        out_shape=[jax.ShapeDtypeStruct((m, BRANCH_W), BF16),
                   jax.ShapeDtypeStruct((m, ATT_KV_W), BF16),
                   jax.ShapeDtypeStruct((m, ATT_KV_W), BF16)],
        compiler_params=_params(("parallel",)),
        name="rope",
    )(p2, p1, cq, sq, ck, sk)


def _attn_kernel(q_ref, kc_ref, vc_ref, kp_ref, ko_ref, kn_ref, vp_ref, vo_ref, vn_ref, sink_ref, o_ref,
                 s_ref, p_ref, *, ctx_blocks, blocks_b):
    j = pl.program_id(1)
    q = q_ref[...]
    k_all = jnp.concatenate([kp_ref[...], ko_ref[...], kn_ref[...], kc_ref[...]], axis=0)
    v_all = jnp.concatenate([vp_ref[...], vo_ref[...], vn_ref[...], vc_ref[...]], axis=0)
    nloc = 3 * BLOCK
    slab = ATT_SLAB
    qi0 = lax.broadcasted_iota(jnp.int32, (slab, nloc), 0)
    ki = lax.broadcasted_iota(jnp.int32, (slab, nloc), 1)
    never = 4 * BLOCK
    prev_off = jnp.where(j > ctx_blocks, 0, never)
    own_hi = jnp.where(j >= ctx_blocks, 2 * BLOCK, BLOCK)
    next_off = 2 * BLOCK - jnp.where((j >= ctx_blocks) & (j < blocks_b - 1), 0, never)
    outs = []
    for g in range(ATT_KV_HEADS):
        gs = slice(g * ATT_HEAD, (g + 1) * ATT_HEAD)
        qg = jnp.concatenate([q[:, (g * ATT_REP + h) * ATT_HEAD:(g * ATT_REP + h + 1) * ATT_HEAD]
                              for h in range(ATT_REP)], axis=0)
        s_ref[g] = lax.dot_general(qg, k_all[:, gs], NT_DIMS, preferred_element_type=F32)
        dens = []
        for blk in range(ATT_REP * BLOCK // slab):
            rs = slice(blk * slab, (blk + 1) * slab)
            head = g * ATT_REP + blk * slab // BLOCK
            qi = qi0 + (blk * slab) % BLOCK
            valid = (((ki < BLOCK) & (ki >= qi + prev_off)) | ((ki >= BLOCK) & (ki < own_hi))
                     | ((ki >= 2 * BLOCK) & (ki <= qi + next_off)))
            s_loc = jnp.where(valid, s_ref[g, rs, :nloc], NEG_INF)
            s_ctx = s_ref[g, rs, nloc:]
            sink = sink_ref[head:head + 1, 0:1]
            mx = jnp.maximum(jnp.maximum(jnp.max(s_loc, axis=-1, keepdims=True),
                                         jnp.max(s_ctx, axis=-1, keepdims=True)), sink)
            e_loc = jnp.exp(s_loc - mx)
            e_ctx = jnp.exp(s_ctx - mx)
            dens.append(jnp.sum(e_loc, axis=-1, keepdims=True) + jnp.sum(e_ctx, axis=-1, keepdims=True)
                        + jnp.exp(sink - mx))
            p_ref[g, rs, :nloc] = e_loc.astype(BF16)
            p_ref[g, rs, nloc:] = e_ctx.astype(BF16)
        o = jnp.dot(p_ref[g], v_all[:, gs], preferred_element_type=F32) / jnp.concatenate(dens, axis=0)
        outs += [o[h * BLOCK:(h + 1) * BLOCK] for h in range(ATT_REP)]
    o_ref[...] = jnp.concatenate(outs, axis=-1).astype(o_ref.dtype)


def _attention(geom, q, k, v, sink):
    nb = geom.lt // BLOCK
    cb = geom.ctx_len // BLOCK
    row = lambda b, j: (b * nb + j, 0)
    prev = lambda b, j: (b * nb + jnp.maximum(j - 1, 0), 0)
    nxt = lambda b, j: (b * nb + jnp.minimum(j + 1, nb - 1), 0)
    ctx = lambda b, j: (b * (geom.lt // geom.ctx_len), 0)
    assert geom.lt % geom.ctx_len == 0
    kvs = lambda f: pl.BlockSpec((BLOCK, ATT_KV_W), f)
    cspec = pl.BlockSpec((geom.ctx_len, ATT_KV_W), ctx)
    sink_tab = jnp.broadcast_to(sink.astype(F32)[:, None], (ATT_HEADS, LANES))
    return pl.pallas_call(
        functools.partial(_attn_kernel, ctx_blocks=cb, blocks_b=nb),
        grid=(geom.batch, nb),
        in_specs=[pl.BlockSpec((BLOCK, BRANCH_W), row), cspec, cspec,
                  kvs(prev), kvs(row), kvs(nxt), kvs(prev), kvs(row), kvs(nxt),
                  pl.BlockSpec((ATT_HEADS, LANES), lambda b, j: (0, 0))],
        out_specs=pl.BlockSpec((BLOCK, BRANCH_W), row),
        out_shape=jax.ShapeDtypeStruct((geom.m, BRANCH_W), BF16),
        scratch_shapes=[pltpu.VMEM((ATT_KV_HEADS, ATT_REP * BLOCK, 3 * BLOCK + geom.ctx_len), F32),
                        pltpu.VMEM((ATT_KV_HEADS, ATT_REP * BLOCK, 3 * BLOCK + geom.ctx_len), BF16)],
        compiler_params=_params(("parallel", "parallel")),
        name="attention",
    )(q, k, v, k, k, k, v, v, v, sink_tab)


def _conv_kernel(u_ref, up_ref, un_ref, dw_ref, db_ref, lg_ref, lb_ref, o_ref, hp_ref, sh_ref,
                 *, tiles_b, ctx_tiles):
    j = pl.program_id(0) % tiles_b
    t = u_ref.shape[0]

    def glu(u):
        return u[:, :BRANCH_W] * _sigmoid(u[:, BRANCH_W:])

    at_start = (j == 0) | (j == ctx_tiles)
    at_end = (j == ctx_tiles - 1) | (j == tiles_b - 1)
    hp_ref[0:HALO] = jnp.where(at_start, 0.0, glu(up_ref[...]))
    hp_ref[HALO:HALO + t] = glu(u_ref[...])
    hp_ref[HALO + t:] = jnp.where(at_end, 0.0, glu(un_ref[...]))
    span = t + 2 * HALO - SUBLANES
    for r in range(1, SUBLANES):
        sh_ref[r, 0:span] = hp_ref[pl.ds(r, span)]
    for chunk in range(t // CONV_ROWS):
        acc = jnp.zeros((CONV_ROWS, BRANCH_W), F32) + db_ref[...]
        for tap in range(CONV_K):
            q, r = divmod(HALO - CONV_PAD + tap, SUBLANES)
            rows = pl.ds(chunk * CONV_ROWS + q * SUBLANES, CONV_ROWS)
            src = hp_ref[rows] if r == 0 else sh_ref[r, rows]
            acc = acc + src * dw_ref[tap:tap + 1]
        mean = jnp.mean(acc, axis=-1, keepdims=True)
        cen = acc - mean
        var = jnp.mean(cen * cen, axis=-1, keepdims=True)
        h = cen * lax.rsqrt(var + LN_EPS) * lg_ref[...] + lb_ref[...]
        o_ref[chunk * CONV_ROWS:(chunk + 1) * CONV_ROWS] = (h * _sigmoid(h)).astype(o_ref.dtype)


def _conv(geom, p4, lp):
    t = ROW_TILE
    nh = geom.m // HALO
    vspec = pl.BlockSpec((1, BRANCH_W), lambda i: (0, 0))
    return pl.pallas_call(
        functools.partial(_conv_kernel, tiles_b=geom.tiles_b, ctx_tiles=geom.ctx_tiles),
        grid=(geom.tiles,),
        in_specs=[pl.BlockSpec((t, 2 * BRANCH_W), lambda i: (i, 0)),
                  pl.BlockSpec((HALO, 2 * BRANCH_W), lambda i: (jnp.maximum(i * (t // HALO) - 1, 0), 0)),
                  pl.BlockSpec((HALO, 2 * BRANCH_W), lambda i: (jnp.minimum((i + 1) * (t // HALO), nh - 1), 0)),
                  pl.BlockSpec((CONV_K, BRANCH_W), lambda i: (0, 0)),
                  vspec, vspec, vspec],
        out_specs=pl.BlockSpec((t, BRANCH_W), lambda i: (i, 0)),
        out_shape=jax.ShapeDtypeStruct((geom.m, BRANCH_W), BF16),
        scratch_shapes=[pltpu.VMEM((t + 2 * HALO, BRANCH_W), F32),
                        pltpu.VMEM((SUBLANES, t + 2 * HALO, BRANCH_W), F32)],
        compiler_params=_params(("parallel",)),
        name="conformer_conv",
    )(p4, p4, p4, lp["conv_dw"], lp["conv_dw_b"].reshape(1, BRANCH_W),
      lp["conv_ln_g"].reshape(1, BRANCH_W), lp["conv_ln_b"].reshape(1, BRANCH_W))


def _dft_cos_sin(n, scale):
    idx = np.arange(n, dtype=np.int64)
    ang = 2.0 * np.pi * ((idx[:, None] * idx[None, :]) % n).astype(np.float64) / n
    return np.cos(ang) * scale, np.sin(ang) * scale


def _channel_dft():
    c, s = _dft_cos_sin(FNO_GROUP_W, FNO_GROUP_W ** -0.5)
    eye = np.eye(FNO_GROUPS)
    return jnp.asarray(np.concatenate([np.kron(eye, c), np.kron(eye, s)], axis=1), dtype=F32).astype(BF16)


def _dft_pos_kernel(c_ref, s_ref, gc_ref, gs_ref, o_ref):
    o_ref[0] = (jnp.dot(c_ref[...], gc_ref[0], preferred_element_type=F32)
                + jnp.dot(s_ref[...], gs_ref[0], preferred_element_type=F32)).astype(o_ref.dtype)


def _fourier(u):
    bsz, length, _ = u.shape
    gcs = _matmul(_RowPlan(None, 1024, "plain", bsz * length), u.reshape(bsz * length, BRANCH_W), _channel_dft(),
                  BF16, name="dft_channels")
    gcs = gcs.reshape(bsz, length, 2 * BRANCH_W)
    c, s = _dft_cos_sin(length, length ** -0.5)
    tm = _pick_tile(length, 512, ROW_TILE)
    return pl.pallas_call(
        _dft_pos_kernel,
        grid=(length // tm, bsz),
        in_specs=[pl.BlockSpec((tm, length), lambda i, b: (i, 0)),
                  pl.BlockSpec((tm, length), lambda i, b: (i, 0)),
                  pl.BlockSpec((1, length, BRANCH_W), lambda i, b: (b, 0, 0)),
                  pl.BlockSpec((1, length, BRANCH_W), lambda i, b: (b, 0, 1))],
        out_specs=pl.BlockSpec((1, tm, BRANCH_W), lambda i, b: (b, i, 0)),
        out_shape=jax.ShapeDtypeStruct((bsz, length, BRANCH_W), BF16),
        compiler_params=_params(("parallel", "parallel")),
        name="dft_positions",
    )(jnp.asarray(c, dtype=F32).astype(BF16), jnp.asarray(-s, dtype=F32).astype(BF16), gcs, gcs)


def _merge_kernel(f0, f1, f2, f3, w_ref, g0, g1, g2, g3, o_ref):
    acc = None
    for i, (f, g) in enumerate(((f0, g0), (f1, g1), (f2, g2), (f3, g3))):
        term = jnp.dot(f[...], w_ref[i], preferred_element_type=F32) * g[...].astype(F32)
        acc = term if acc is None else acc + term
    o_ref[...] = acc.astype(o_ref.dtype)


def _merge(feat_plan, plan, feats, w_branch, gate):
    assert (feat_plan.tm, feat_plan.n) == (plan.tm, plan.n)
    tn = 1024
    fspec = feat_plan.spec(BRANCH_W, lambda i, j: (i, 0))
    gspec = lambda br: plan.spec(tn, lambda i, j: (i, br * D_MODEL + j * tn))
    return pl.pallas_call(
        _merge_kernel,
        grid=(plan.n, D_MODEL // tn),
        in_specs=[fspec] * 4 + [pl.BlockSpec((N_BRANCH, BRANCH_W, tn), lambda i, j: (0, 0, j))]
        + [gspec(br) for br in range(N_BRANCH)],
        out_specs=plan.spec(tn, lambda i, j: (i, j * tn)),
        out_shape=jax.ShapeDtypeStruct((plan.rows, D_MODEL), BF16),
        compiler_params=_params(("parallel", "parallel")),
        name="branch_merge",
    )(*feats, w_branch, gate, gate, gate, gate)


def _mixer(geom, h, xall, modtab, lp, w_in, layer, norm2_g, latent_only):
    src, dst = ("latent", "packed") if latent_only else ("all", "all")
    rows_all = _RowPlan(geom, 1024)
    proj = lambda lo, hi, dt, name, **kw: _matmul_w32(kw.pop("plan", rows_all), h, w_in, layer, dt,
                                                      cols=(lo, hi - lo), name=name, **kw)
    p1 = proj(0, CTX_STATE_COLS, F32, "in_proj_state")
    p2 = proj(O_G, O_FNO, F32, "in_proj_gq")
    p3 = proj(O_FNO, O_CONV, BF16, "in_proj_fno")
    p4 = proj(O_CONV, O_GATE, F32, "in_proj_conv")
    gate = proj(O_GATE, IN_W, BF16, "in_proj_gate", epilogue="sigmoid", plan=_RowPlan(geom, 1024, src),
                out_plan=_RowPlan(geom, 1024, dst))

    ins = _rwkv_prep(geom, p1, lp)
    y = _rwkv_scan(geom, ins)
    rw = _rwkv_readout(geom, y, ins, p2, lp)

    q, k, v = _rope(geom, p1, p2)
    att = _attention(geom, q, k, v, lp["att_sink"])

    cv = _conv(geom, p4, lp)

    p3 = p3.reshape(geom.batch, geom.lt, BRANCH_W)
    fno = jnp.concatenate([_fourier(p3[:, :geom.ctx_len]), _fourier(p3[:, geom.ctx_len:])], axis=1)
    fno = fno.reshape(geom.m, BRANCH_W)

    half_src, half_dst = _RowPlan(geom, 512, src), _RowPlan(geom, 512, dst)
    mixed = _merge(half_src, half_dst, (fno, rw, att, cv), lp["w_branch"].astype(BF16), gate)
    return _matmul(half_dst, mixed, lp["w_out"].astype(BF16), F32, epilogue="resid", res=xall, res_plan=half_src,
                   modtab=modtab, gate_row=2, norm=(norm2_g, modtab, (3, 4)), name="out_proj")


def kernel(x, c, ctx, c_ctx, ada_w, ada_b, norm1_g, norm2_g, w_in, rwkv_mu, rwkv_w0, rwkv_w_up, rwkv_a0, rwkv_a_up, rwkv_k_k, rwkv_k_a, rwkv_r_k, rwkv_g_up, rwkv_lnx_g, rwkv_lnx_b, att_sink, conv_dw, conv_dw_b, conv_ln_g, conv_ln_b, w_branch, w_out, w_mlp1, w_mlp2, final_g):
    batch, seq, _ = x.shape
    geom = _Geom(batch, ctx.shape[1], seq)
    depth = w_in.shape[0]
    assert batch + 1 <= 8
    cond = jnp.zeros((8, D_MODEL), F32).at[:batch].set(c).at[batch].set(c_ctx)
    xall = jnp.concatenate([ctx, x], axis=1).reshape(geom.m, D_MODEL)
    modtabs = []
    for l in range(depth):
        mod = _ada_mod(cond, ada_w, l, ada_b[l])
        mod_x = mod[:batch].reshape(batch, 1, 6, D_MODEL)
        mod_c = jnp.broadcast_to(mod[batch].reshape(1, 1, 6, D_MODEL), (batch, 1, 6, D_MODEL))
        modtabs.append(jnp.concatenate([mod_c, mod_x], axis=1).reshape(2 * batch, 6, D_MODEL))
    h = _norm_mod(geom, xall, norm1_g[0], modtabs[0], rows=(0, 1))
    for l in range(depth):
        modtab = modtabs[l]
        lp = {
            "rwkv_mu": rwkv_mu[l], "rwkv_w0": rwkv_w0[l], "rwkv_w_up": rwkv_w_up[l],
            "rwkv_a0": rwkv_a0[l], "rwkv_a_up": rwkv_a_up[l], "rwkv_k_k": rwkv_k_k[l],
            "rwkv_k_a": rwkv_k_a[l], "rwkv_r_k": rwkv_r_k[l], "rwkv_g_up": rwkv_g_up[l],
            "rwkv_lnx_g": rwkv_lnx_g[l], "rwkv_lnx_b": rwkv_lnx_b[l], "att_sink": att_sink[l],
            "conv_dw": conv_dw[l], "conv_dw_b": conv_dw_b[l], "conv_ln_g": conv_ln_g[l],
            "conv_ln_b": conv_ln_b[l], "w_branch": w_branch[l], "w_out": w_out[l],
        }
        last = l + 1 == depth
        xall, h2 = _mixer(geom, h, xall, modtab, lp, w_in, l, norm2_g[l], latent_only=last)
        kind = "packed" if last else "all"
        hid = _matmul_w32(_RowPlan(geom, 1024, kind), h2, w_mlp1, l, BF16, epilogue="relu2", name="mlp_up")
        down = functools.partial(_matmul, a=hid, b=w_mlp2[l].astype(BF16), out_dtype=F32, epilogue="resid",
                                 res=xall, modtab=modtab, gate_row=5, name="mlp_down")
        if last:
            xall = down(_RowPlan(geom, 1024, kind))
        else:
            xall, h = down(_RowPlan(geom, 512), norm=(norm1_g[l + 1], modtabs[l + 1], (0, 1)))
    return _final_norm(geom, xall, final_g).reshape(batch, seq, D_MODEL)
```

```python
import functools
import math

import numpy as np
import jax
import jax.numpy as jnp
from jax import lax
from jax.experimental import pallas as pl
from jax.experimental.pallas import tpu as pltpu

F32 = jnp.float32
BF16 = jnp.bfloat16

D_MODEL = 2048
GRID_W = 64
NORM_EPS = 1e-6
N_BRANCH = 4
BRANCH_W = D_MODEL // N_BRANCH
FNO_GROUPS = 4
FNO_GROUP_W = BRANCH_W // FNO_GROUPS
RWKV_HEAD = 64
RWKV_HEADS = BRANCH_W // RWKV_HEAD
N_DIR = 2
DECAY_LORA = 64
AICL_LORA = 64
GATE_LORA = 128
DIR_LORA_W = DECAY_LORA + AICL_LORA
SHIFT_W = 3 * BRANCH_W + DIR_LORA_W
GN_EPS = 64e-5
ATT_HEAD = 64
ATT_HEADS = BRANCH_W // ATT_HEAD
ATT_KV_HEADS = 2
ATT_REP = ATT_HEADS // ATT_KV_HEADS
ATT_KV_W = ATT_KV_HEADS * ATT_HEAD
WINDOW = 128
BLOCK = 128
ROPE_BASE = 10000.0
NEG_INF = -1e30
CONV_K = 31
CONV_PAD = (CONV_K - 1) // 2
LN_EPS = 1e-5
MLP_HIDDEN = 4 * D_MODEL

O_LORA = 3 * BRANCH_W
O_KV = O_LORA + N_DIR * DIR_LORA_W
CTX_STATE_COLS = O_KV + 2 * ATT_KV_W
O_G = CTX_STATE_COLS
O_Q = O_G + GATE_LORA
O_FNO = O_Q + BRANCH_W
O_CONV = O_FNO + BRANCH_W
O_GATE = O_CONV + 2 * BRANCH_W
IN_W = O_GATE + N_BRANCH * D_MODEL

LANES = 128
ROW_TILE = 256
SCAN_CHUNK = 64
PAIR_W = 2 * RWKV_HEAD
N_PAIRS = BRANCH_W // PAIR_W
HALO = 16
ATT_SLAB = 64
SUBLANES = 8
MXU_COLS = 256
CAST_ROWS = 256
CONV_ROWS = 32
PREP_ROWS = 64
SCAN_STEPS = 2
VMEM_LIMIT = 56 * 1024 * 1024

NT_DIMS = (((1,), (1,)), ((), ()))
NN_DIMS = (((1,), (0,)), ((), ()))
TN_DIMS = (((0,), (0,)), ((), ()))


def _params(sem):
    return pltpu.CompilerParams(dimension_semantics=sem, vmem_limit_bytes=VMEM_LIMIT)


def _split2(a):
    hi = a.astype(BF16)
    lo = (a - hi.astype(F32)).astype(BF16)
    return hi, lo


def _dot3(a, b, dims=NN_DIMS):
    ah, al = _split2(a)
    bh, bl = _split2(b)
    dg = functools.partial(lax.dot_general, dimension_numbers=dims, preferred_element_type=F32)
    return dg(ah, bh) + (dg(ah, bl) + dg(al, bh))


def _dot_sel(a, sel_bf16):
    hi, lo = _split2(a)
    n = a.shape[0]
    both = jnp.dot(jnp.concatenate([hi, lo], axis=0), sel_bf16, preferred_element_type=F32)
    return both[:n] + both[n:]


def _sigmoid(x):
    return 1.0 / (1.0 + jnp.exp(-x))


def _ada_kernel(a_ref, w_ref, b_ref, o_ref):
    a = a_ref[...]
    s = a * _sigmoid(a)
    o_ref[...] = jnp.dot(s, w_ref[0], preferred_element_type=F32,
                         precision=lax.Precision.HIGHEST) + b_ref[...]


def _ada_mod(cond, w, layer, b):
    n = w.shape[2]
    tn = 1024
    return pl.pallas_call(
        _ada_kernel,
        grid=(n // tn,),
        in_specs=[pl.BlockSpec((8, D_MODEL), lambda j: (0, 0)),
                  pl.BlockSpec((1, D_MODEL, tn), lambda j: (layer, 0, j)),
                  pl.BlockSpec((1, tn), lambda j: (0, j))],
        out_specs=pl.BlockSpec((8, tn), lambda j: (0, j)),
        out_shape=jax.ShapeDtypeStruct((8, n), F32),
        compiler_params=_params(("parallel",)),
        name="ada_mod",
    )(cond, w, b.reshape(1, n))


class _Geom:
    def __init__(self, batch, ctx_len, seq):
        assert ctx_len % ROW_TILE == 0 and seq % ROW_TILE == 0
        assert seq % GRID_W == 0 and seq % BLOCK == 0 and ctx_len % BLOCK == 0
        self.batch = batch
        self.ctx_len = ctx_len
        self.seq = seq
        self.lt = ctx_len + seq
        self.m = batch * self.lt
        self.tiles_b = self.lt // ROW_TILE
        self.ctx_tiles = ctx_len // ROW_TILE
        self.tiles = batch * self.tiles_b

    def mod_row(self, i):
        return 2 * (i // self.tiles_b) + ((i % self.tiles_b) >= self.ctx_tiles).astype(jnp.int32)


def _norm_kernel(x_ref, g_ref, *rest, rows):
    x = x_ref[...]
    y = x * lax.rsqrt(jnp.mean(x * x, axis=-1, keepdims=True) + NORM_EPS) * g_ref[...]
    if rows is None:
        (o_ref,) = rest
    else:
        mod_ref, o_ref = rest
        mod = mod_ref[0]
        y = y * (1.0 + mod[rows[1]:rows[1] + 1]) + mod[rows[0]:rows[0] + 1]
    o_ref[...] = y.astype(o_ref.dtype)


def _norm_mod(geom, x, g, modtab, rows):
    return pl.pallas_call(
        functools.partial(_norm_kernel, rows=rows),
        grid=(geom.tiles,),
        in_specs=[pl.BlockSpec((ROW_TILE, D_MODEL), lambda i: (i, 0)),
                  pl.BlockSpec((1, D_MODEL), lambda i: (0, 0)),
                  pl.BlockSpec((1, 6, D_MODEL), lambda i: (geom.mod_row(i), 0, 0))],
        out_specs=pl.BlockSpec((ROW_TILE, D_MODEL), lambda i: (i, 0)),
        out_shape=jax.ShapeDtypeStruct((geom.m, D_MODEL), BF16),
        compiler_params=_params(("parallel",)),
        name="norm_mod",
    )(x, g.reshape(1, D_MODEL), modtab)


def _final_norm(geom, x, g):
    return pl.pallas_call(
        functools.partial(_norm_kernel, rows=None),
        grid=(geom.batch * geom.seq // ROW_TILE,),
        in_specs=[pl.BlockSpec((ROW_TILE, D_MODEL), lambda i: (i, 0)),
                  pl.BlockSpec((1, D_MODEL), lambda i: (0, 0))],
        out_specs=pl.BlockSpec((ROW_TILE, D_MODEL), lambda i: (i, 0)),
        out_shape=jax.ShapeDtypeStruct((geom.batch * geom.seq, D_MODEL), F32),
        compiler_params=_params(("parallel",)),
        name="final_norm",
    )(x, g.reshape(1, D_MODEL))


def _mm_store(acc, res_ref, mod_refs, o_ref, epilogue, gate_row, norm=None, cs=slice(None)):
    if epilogue == "resid":
        for s, mod_ref in enumerate(mod_refs):
            rs = slice(s * ROW_TILE, (s + 1) * ROW_TILE)
            xn = res_ref[rs, cs] + mod_ref[0, gate_row:gate_row + 1, cs] * acc[rs]
            o_ref[rs, cs] = xn
            if norm is not None:
                g_ref, nmod_refs, h_ref, rows = norm
                nmod = nmod_refs[s][0]
                y = xn * lax.rsqrt(jnp.mean(xn * xn, axis=-1, keepdims=True) + NORM_EPS) * g_ref[...]
                h_ref[rs] = (y * (1.0 + nmod[rows[1]:rows[1] + 1]) + nmod[rows[0]:rows[0] + 1]).astype(h_ref.dtype)
        return
    if epilogue == "sigmoid":
        acc = _sigmoid(acc)
    elif epilogue == "relu2":
        acc = jnp.square(jnp.maximum(acc, 0.0))
    o_ref[:, cs] = acc.astype(o_ref.dtype)


def _col_chunks(tn, norm):
    if norm is not None or tn % MXU_COLS:
        return [slice(None)]
    return [slice(c, c + MXU_COLS) for c in range(0, tn, MXU_COLS)]


def _mm_kernel(a_ref, b_ref, *rest, epilogue, gate_row, nk, n_sub, norm_rows):
    rest = list(rest)
    res_ref, mod_refs, norm = None, (), None
    if epilogue == "resid":
        res_ref, mod_refs, rest = rest[0], rest[1:1 + n_sub], rest[1 + n_sub:]
        if norm_rows is not None:
            norm = (rest[0], rest[1:1 + n_sub], rest[2 + n_sub], norm_rows)
            rest = [rest[1 + n_sub]] + rest[3 + n_sub:]
    o_ref = rest[0]
    chunks = _col_chunks(o_ref.shape[1], norm)
    if nk == 1:
        for cs in chunks:
            acc = jnp.dot(a_ref[...], b_ref[:, cs], preferred_element_type=F32)
            _mm_store(acc, res_ref, mod_refs, o_ref, epilogue, gate_row, norm, cs)
        return
    acc_ref = rest[1]
    k = pl.program_id(2)

    @pl.when(k == 0)
    def _():
        acc_ref[...] = jnp.dot(a_ref[...], b_ref[...], preferred_element_type=F32)

    @pl.when((k > 0) & (k < nk - 1))
    def _():
        acc_ref[...] += jnp.dot(a_ref[...], b_ref[...], preferred_element_type=F32)

    @pl.when(k == nk - 1)
    def _():
        for cs in chunks:
            acc = acc_ref[:, cs] + jnp.dot(a_ref[...], b_ref[:, cs], preferred_element_type=F32)
            _mm_store(acc, res_ref, mod_refs, o_ref, epilogue, gate_row, norm, cs)


def _mm_w32_kernel(a_ref, w_ref, *rest, epilogue, gate_row):
    res_ref, mod_refs = (rest[0], rest[1:-2]) if epilogue == "resid" else (None, ())
    o_ref, wb_ref = rest[-2:]

    @pl.when(pl.program_id(1) == 0)
    def _():
        def cast_rows(r, carry):
            rows = pl.ds(pl.multiple_of(r * CAST_ROWS, CAST_ROWS), CAST_ROWS)
            wb_ref[rows, :] = w_ref[0, rows, :].astype(BF16)
            return carry

        lax.fori_loop(0, wb_ref.shape[0] // CAST_ROWS, cast_rows, 0)

    for cs in _col_chunks(o_ref.shape[1], None):
        acc = jnp.dot(a_ref[...], wb_ref[:, cs], preferred_element_type=F32)
        _mm_store(acc, res_ref, mod_refs, o_ref, epilogue, gate_row, cs=cs)


def _pick_tile(n, cap, unit=LANES):
    t = (min(cap, n) // unit) * unit
    while n % t:
        t -= unit
    return t


class _RowPlan:
    def __init__(self, geom, tm_cap, kind="all", rows=None):
        if kind in ("latent", "packed"):
            self.tm = _pick_tile(geom.seq, tm_cap, ROW_TILE)
            per_b = geom.seq // self.tm
            self.n = geom.batch * per_b
            self.rows = geom.m if kind == "latent" else geom.batch * geom.seq
            if kind == "latent":
                self.off = lambda i: (i // per_b) * geom.lt + geom.ctx_len + (i % per_b) * self.tm
                self.mod_row = geom.mod_row
            else:
                self.off = lambda i: i * self.tm
                self.mod_row = lambda t: 2 * (t // (geom.seq // ROW_TILE)) + 1
        else:
            self.rows = geom.m if kind == "all" else rows
            self.tm = _pick_tile(self.rows, tm_cap, ROW_TILE)
            self.n = self.rows // self.tm
            self.off = lambda i: i * self.tm
            self.mod_row = geom.mod_row if kind == "all" else None
        self.n_sub = self.tm // ROW_TILE

    def spec(self, width, at):
        def index(*g):
            i, col = at(*g)
            col = col if isinstance(col, int) else pl.multiple_of(col, LANES)
            return pl.multiple_of(self.off(i), ROW_TILE), col
        return pl.BlockSpec((pl.Element(self.tm), pl.Element(width)), index)

    def mod_specs(self, tn, at):
        return [pl.BlockSpec((1, 6, tn), lambda *g, s=s: (
            self.mod_row(self.off(at(*g)[0]) // ROW_TILE + s), 0, at(*g)[1])) for s in range(self.n_sub)]


def _matmul(plan, a, b, out_dtype, epilogue="none", res=None, modtab=None, gate_row=0, norm=None,
            res_plan=None, out_plan=None, tn_cap=1024, tk_cap=2048, name="matmul"):
    kdim = a.shape[1]
    n = b.shape[1]
    out_plan = out_plan or plan
    res_plan = res_plan or out_plan
    tm, m = plan.tm, out_plan.rows
    assert (out_plan.tm, out_plan.n, res_plan.tm, res_plan.n) == (tm, plan.n, tm, plan.n)
    tn = n if norm is not None else _pick_tile(n, tn_cap)
    tk = _pick_tile(kdim, tk_cap)
    nk = kdim // tk
    row_j = lambda i, j, k: (i, j * tn)
    tile_ij = lambda i, j, k: (i, j)
    in_specs = [plan.spec(tk, lambda i, j, k: (i, k * tk)),
                pl.BlockSpec((tk, tn), lambda i, j, k: (k, j))]
    args = [a, b]
    out_specs = out_plan.spec(tn, row_j)
    out_shape = jax.ShapeDtypeStruct((m, n), out_dtype)
    if epilogue == "resid":
        in_specs += [res_plan.spec(tn, row_j)] + out_plan.mod_specs(tn, tile_ij)
        args += [res] + [modtab] * plan.n_sub
        if norm is not None:
            gain, nmodtab, _ = norm
            in_specs += [pl.BlockSpec((1, tn), lambda i, j, k: (0, 0))] + out_plan.mod_specs(tn, tile_ij)
            args += [gain.reshape(1, n)] + [nmodtab] * plan.n_sub
            out_specs = [out_specs, out_plan.spec(tn, row_j)]
            out_shape = [out_shape, jax.ShapeDtypeStruct((m, n), BF16)]
    return pl.pallas_call(
        functools.partial(_mm_kernel, epilogue=epilogue, gate_row=gate_row, nk=nk, n_sub=plan.n_sub,
                          norm_rows=None if norm is None else norm[2]),
        grid=(plan.n, n // tn, nk),
        in_specs=in_specs,
        out_specs=out_specs,
        out_shape=out_shape,
        scratch_shapes=[pltpu.VMEM((tm, tn), F32)] if nk > 1 else [],
        compiler_params=_params(("parallel", "parallel", "arbitrary")),
        name=name,
    )(*args)


def _matmul_w32(plan, a, w, layer, out_dtype, epilogue="none", cols=None, out_plan=None, tn_cap=1024,
                name="matmul_w32"):
    kdim = a.shape[1]
    col0, n = cols if cols is not None else (0, w.shape[2])
    assert col0 % LANES == 0 and epilogue != "resid"
    out_plan = out_plan or plan
    assert (out_plan.tm, out_plan.n) == (plan.tm, plan.n)
    tn = _pick_tile(n, tn_cap)
    in_specs = [plan.spec(kdim, lambda j, i: (i, 0)),
                pl.BlockSpec((pl.Element(1), pl.Element(kdim), pl.Element(tn)),
                             lambda j, i: (layer, 0, pl.multiple_of(col0 + j * tn, LANES)))]
    args = [a, w]
    return pl.pallas_call(
        functools.partial(_mm_w32_kernel, epilogue=epilogue, gate_row=0),
        grid=(n // tn, plan.n),
        in_specs=in_specs,
        out_specs=out_plan.spec(tn, lambda j, i: (i, j * tn)),
        out_shape=jax.ShapeDtypeStruct((out_plan.rows, n), out_dtype),
        scratch_shapes=[pltpu.VMEM((kdim, tn), BF16)],
        compiler_params=_params(("parallel", "arbitrary")),
        name=name,
    )(*args)


def _head_sum_matrix(width, head):
    idx = np.arange(width) // head
    return jnp.asarray((idx[:, None] == idx[None, :]).astype(np.float32), dtype=BF16)


def _prep_kernel(p_ref, lora_ref, halo_ref, hlora_ref, mu_ref, w0_ref, wup_ref, a0_ref, aup_ref,
                 kk_ref, ka_ref, hs_ref, r_o, lw_o, k_o, v_o, kkn_o, b_o, *, tiles_b, ctx_tiles):
    d = pl.program_id(0)
    j = pl.program_id(1) % tiles_b
    fwd = d == 0
    t = p_ref.shape[0]
    rkv_w = 3 * BRANCH_W
    at_start = (j == 0) | (j == ctx_tiles)
    at_end = (j == ctx_tiles - 1) | (j == tiles_b - 1)
    no_edge = (fwd & at_start) | (jnp.logical_not(fwd) & at_end)
    row = lax.broadcasted_iota(jnp.int32, (t, 1), 0)
    edge_row = jnp.where(fwd, 0, t - 1)

    def neighbour(x_ref, h_ref):
        edge = jnp.where(no_edge, 0.0, jnp.where(fwd, h_ref[7:8], h_ref[0:1]))
        x = x_ref[...]
        return jnp.where(row == edge_row, edge, jnp.where(fwd, pltpu.roll(x, 1, 0), pltpu.roll(x, t - 1, 0)))

    nb_rkv = neighbour(p_ref, halo_ref)
    nb_lora = neighbour(lora_ref, hlora_ref)
    mu = mu_ref[0]
    for c in range(t // PREP_ROWS):
        rs = slice(c * PREP_ROWS, (c + 1) * PREP_ROWS)
        f = p_ref[rs]
        f = f + mu[:, :rkv_w] * (nb_rkv[rs] - f)
        lo = lora_ref[rs]
        lo = lo + mu[:, rkv_w:] * (nb_lora[rs] - lo)
        r = f[:, 0:BRANCH_W]
        k = f[:, BRANCH_W:2 * BRANCH_W]
        v = f[:, 2 * BRANCH_W:rkv_w]
        w_raw = w0_ref[0] + _dot3(jnp.tanh(lo[:, :DECAY_LORA]), wup_ref[0])
        lw = -math.exp(-0.5) * _sigmoid(w_raw)
        a = _sigmoid(a0_ref[0] + _dot3(lo[:, DECAY_LORA:], aup_ref[0]))
        kk = k * kk_ref[...]
        norm = jnp.sqrt(_dot_sel(kk * kk, hs_ref[...]))
        kk = kk / jnp.maximum(norm, 1e-12)
        r_o[0, rs] = r.astype(r_o.dtype)
        lw_o[0, rs] = lw
        k_o[0, rs] = (k * (1.0 + (a - 1.0) * ka_ref[...])).astype(k_o.dtype)
        v_o[0, rs] = v.astype(v_o.dtype)
        kkn_o[0, rs] = kk.astype(kkn_o.dtype)
        b_o[0, rs] = (kk * a).astype(b_o.dtype)


def _rwkv_prep(geom, p1, lp):
    m = geom.m
    t = ROW_TILE
    tb = geom.tiles_b
    rkv_w = 3 * BRANCH_W
    lora_blk0 = O_LORA // DIR_LORA_W
    n_blk8 = m // 8

    def halo_idx(d, i):
        before = jnp.maximum(i * (t // 8) - 1, 0)
        after = jnp.minimum((i + 1) * (t // 8), n_blk8 - 1)
        return jnp.where(d == 0, before, after)

    out = jax.ShapeDtypeStruct((N_DIR, m, BRANCH_W), F32)
    ospec = pl.BlockSpec((1, t, BRANCH_W), lambda d, i: (d, i, 0))
    vec = lambda a: a.reshape(1, BRANCH_W)
    dvec = pl.BlockSpec((1, 1, BRANCH_W), lambda d, i: (d, 0, 0))
    return pl.pallas_call(
        functools.partial(_prep_kernel, tiles_b=tb, ctx_tiles=geom.ctx_tiles),
        grid=(N_DIR, geom.tiles),
        in_specs=[pl.BlockSpec((t, rkv_w), lambda d, i: (i, 0)),
                  pl.BlockSpec((t, DIR_LORA_W), lambda d, i: (i, lora_blk0 + d)),
                  pl.BlockSpec((8, rkv_w), lambda d, i: (halo_idx(d, i), 0)),
                  pl.BlockSpec((8, DIR_LORA_W), lambda d, i: (halo_idx(d, i), lora_blk0 + d)),
                  pl.BlockSpec((1, 1, SHIFT_W), lambda d, i: (d, 0, 0)),
                  dvec,
                  pl.BlockSpec((1, DECAY_LORA, BRANCH_W), lambda d, i: (d, 0, 0)),
                  dvec,
                  pl.BlockSpec((1, AICL_LORA, BRANCH_W), lambda d, i: (d, 0, 0)),
                  pl.BlockSpec((1, BRANCH_W), lambda d, i: (0, 0)),
                  pl.BlockSpec((1, BRANCH_W), lambda d, i: (0, 0)),
                  pl.BlockSpec((BRANCH_W, BRANCH_W), lambda d, i: (0, 0))],
        out_specs=[ospec] * 6,
        out_shape=[jax.ShapeDtypeStruct(out.shape, F32 if i == 1 else BF16) for i in range(6)],
        compiler_params=_params(("parallel", "parallel")),
        name="rwkv_prep",
    )(p1, p1, p1, p1, lp["rwkv_mu"].reshape(N_DIR, 1, SHIFT_W),
      lp["rwkv_w0"].reshape(N_DIR, 1, BRANCH_W), lp["rwkv_w_up"],
      lp["rwkv_a0"].reshape(N_DIR, 1, BRANCH_W), lp["rwkv_a_up"],
      vec(lp["rwkv_k_k"]), vec(lp["rwkv_k_a"]), _head_sum_matrix(BRANCH_W, RWKV_HEAD))


def _scan_kernel(*refs):
    yf_ref, yb_ref, h_ref = refs[12:]
    c = pl.program_id(1)
    C = SCAN_CHUNK
    W = 2 * C

    @pl.when(c == 0)
    def _():
        h_ref[...] = jnp.zeros_like(h_ref)

    rr = lax.broadcasted_iota(jnp.int32, (C, C), 0)
    cc = lax.broadcasted_iota(jnp.int32, (C, C), 1)
    lane = lax.broadcasted_iota(jnp.int32, (1, PAIR_W), 1)
    m_a = (lane < RWKV_HEAD).astype(F32)
    m_b = 1.0 - m_a
    r2 = lax.broadcasted_iota(jnp.int32, (W, W), 0)
    c2 = lax.broadcasted_iota(jnp.int32, (W, W), 1)
    same = (r2 // C) == (c2 // C)
    eye = (r2 == c2).astype(F32)

    def pairs(x):
        return [jnp.concatenate([x[:, p * PAIR_W:(p + 1) * PAIR_W] * m_a,
                                 x[:, p * PAIR_W:(p + 1) * PAIR_W] * m_b], axis=0) for p in range(N_PAIRS)]

    strict, incl, tri = [], [], []
    for sgn in (1, -1):
        dt = (r2 % C - c2 % C) * sgn
        strict += [(same & (dt > 0)).astype(F32)] * N_PAIRS
        incl += [(same & (dt >= 0)).astype(F32)] * N_PAIRS
        tri.append(jnp.where((rr - cc) * sgn >= 0, 1.0, 0.0).astype(BF16))
    strict = jnp.stack(strict)
    incl = jnp.stack(incl)

    h = h_ref[...]
    for step in range(SCAN_STEPS):
        rows = (pl.ds(step * C, C), pl.ds((SCAN_STEPS - 1 - step) * C, C))
        h = _scan_chunk(refs[:12], rows, tri, strict, incl, eye, pairs, h, (yf_ref, yb_ref))
    h_ref[...] = h


def _scan_chunk(in_refs, rows, tri, strict, incl, eye, pairs, h0, y_refs):
    C = SCAN_CHUNK
    W = 2 * C
    stacks = {name: [] for name in ("a", "b", "k", "r", "v", "bc", "kc", "pt")}
    for d in range(N_DIR):
        r_ref, lw_ref, k_ref, v_ref, kk_ref, b_ref = in_refs[d::N_DIR]
        rs = rows[d]
        lw = lw_ref[0, rs]
        lp_in = _dot_sel_lhs(tri[d], lw)
        tot = jnp.sum(lw, axis=0, keepdims=True)
        e_neg = jnp.exp(-lp_in)
        e_chk = jnp.exp(tot - lp_in)
        p_tot = jnp.exp(tot)
        stacks["a"] += pairs(-kk_ref[0, rs] * jnp.exp(lp_in - lw))
        stacks["b"] += pairs(b_ref[0, rs] * e_neg)
        stacks["k"] += pairs(k_ref[0, rs] * e_neg)
        stacks["r"] += pairs(r_ref[0, rs] * jnp.exp(lp_in))
        stacks["v"] += pairs(v_ref[0, rs])
        stacks["bc"] += pairs(b_ref[0, rs] * e_chk)
        stacks["kc"] += pairs(k_ref[0, rs] * e_chk)
        stacks["pt"] += [p_tot[:, p * PAIR_W:(p + 1) * PAIR_W] for p in range(N_PAIRS)]
    a_s, b_s, k_s, r_s, v_s, bc_s, kc_s, p_tot = (jnp.stack(stacks[n]) for n in
                                                  ("a", "b", "k", "r", "v", "bc", "kc", "pt"))

    a_b, b_b, k_b, r_b, v_b, bc_b, kc_b = (x.astype(BF16) for x in (a_s, b_s, k_s, r_s, v_s, bc_s, kc_s))
    big = _bdot(jnp.concatenate([a_b, r_b], axis=1), jnp.concatenate([b_b, k_b], axis=1), BNT_DIMS)
    l_ab = big[:, :W, :W] * strict
    l_ak = big[:, :W, W:] * strict
    m_rb = (big[:, W:, :W] * incl).astype(BF16)
    m_rk = big[:, W:, W:] * incl
    t_inv = eye + l_ab
    pw_b = l_ab.astype(BF16)
    pw_b = _bdot(pw_b, pw_b).astype(BF16)
    for _ in range(int(math.log2(C)) - 2):
        both = _bdot(jnp.concatenate([t_inv.astype(BF16), pw_b], axis=1), pw_b)
        t_inv = t_inv + both[:, :W]
        pw_b = both[:, W:].astype(BF16)
    t_inv = t_inv + _bdot(t_inv, pw_b)
    x1 = _bdot(t_inv, jnp.concatenate([a_b, _bdot(l_ak, v_b).astype(BF16)], axis=2)).astype(BF16)
    x2 = _bdot(m_rb, x1)
    r_hat = r_s + x2[:, :, :PAIR_W]
    y0 = x2[:, :, PAIR_W:] + _bdot(m_rk, v_b)
    x3 = _bdot(bc_b, x1, BTN_DIMS)
    g = eye * p_tot + x3[:, :, :PAIR_W]
    h_inc = x3[:, :, PAIR_W:] + _bdot(kc_b, v_b, BTN_DIMS)
    x4 = _bdot(jnp.concatenate([r_hat, g], axis=1), h0)
    ys = x4[:, :W] + y0
    for d, y_ref in enumerate(y_refs):
        for p in range(N_PAIRS):
            y_ref[rows[d], p * PAIR_W:(p + 1) * PAIR_W] = ys[d * N_PAIRS + p, :C] + ys[d * N_PAIRS + p, C:]
    return x4[:, W:] + h_inc


BNN_DIMS = (((2,), (1,)), ((0,), (0,)))
BNT_DIMS = (((2,), (2,)), ((0,), (0,)))
BTN_DIMS = (((1,), (1,)), ((0,), (0,)))


def _bdot(a, b, dims=BNN_DIMS):
    return lax.dot_general(a.astype(BF16), b.astype(BF16), dims, preferred_element_type=F32)


def _dot_sel_lhs(sel_bf16, a):
    hi = a.astype(BF16)
    r1 = a - hi.astype(F32)
    mid = r1.astype(BF16)
    lo = (r1 - mid.astype(F32)).astype(BF16)
    dg = functools.partial(jnp.dot, preferred_element_type=F32)
    return dg(sel_bf16, hi) + (dg(sel_bf16, mid) + dg(sel_bf16, lo))


def _rwkv_scan(geom, ins):
    C = SCAN_CHUNK * SCAN_STEPS
    assert geom.ctx_len % C == 0 and geom.seq % C == 0
    nch = geom.lt // C
    nctx = geom.ctx_len // C

    def rev(c):
        return jnp.where(c < nctx, nctx - 1 - c, nch - 1 + nctx - c)

    fwd = pl.BlockSpec((1, C, BRANCH_W), lambda b, c: (0, b * nch + c, 0))
    bwd = pl.BlockSpec((1, C, BRANCH_W), lambda b, c: (1, b * nch + rev(c), 0))
    out = jax.ShapeDtypeStruct((geom.m, BRANCH_W), F32)
    return pl.pallas_call(
        _scan_kernel,
        grid=(geom.batch, nch),
        in_specs=[fwd, bwd] * 6,
        out_specs=[pl.BlockSpec((C, BRANCH_W), lambda b, c: (b * nch + c, 0)),
                   pl.BlockSpec((C, BRANCH_W), lambda b, c: (b * nch + rev(c), 0))],
        out_shape=[out, out],
        scratch_shapes=[pltpu.VMEM((N_DIR * N_PAIRS, PAIR_W, PAIR_W), F32)],
        compiler_params=_params(("parallel", "arbitrary")),
        name="rwkv_scan",
    )(*[a for a in ins for _ in range(N_DIR)])


def _readout_kernel(yf_ref, yb_ref, r_ref, k_ref, v_ref, p2_ref, gup_ref, rk_ref, lg_ref, lb_ref, hs_ref, o_ref):
    hs = hs_ref[...]
    y = yf_ref[...] + yb_ref[...]
    inv_n = 1.0 / RWKV_HEAD
    mean = _dot_sel(y, hs) * inv_n
    yc = y - mean
    var = _dot_sel(yc * yc, hs) * inv_n
    yn = yc * lax.rsqrt(var + GN_EPS) * lg_ref[...] + lb_ref[...]
    bonus = jnp.zeros_like(y)
    for d in range(N_DIR):
        rk = r_ref[d].astype(F32) * k_ref[d].astype(F32) * rk_ref[d:d + 1]
        bonus = bonus + _dot_sel(rk, hs) * v_ref[d].astype(F32)
    g = _dot3(_sigmoid(p2_ref[:, 0:GATE_LORA]), gup_ref[...])
    o_ref[...] = ((yn + bonus) * g).astype(o_ref.dtype)


def _rwkv_readout(geom, y, ins, p2, lp):
    t = ROW_TILE
    dspec = pl.BlockSpec((N_DIR, t, BRANCH_W), lambda i: (0, i, 0))
    vspec = pl.BlockSpec((1, BRANCH_W), lambda i: (0, 0))
    r, _, k, v, _, _ = ins
    return pl.pallas_call(
        _readout_kernel,
        grid=(geom.tiles,),
        in_specs=[pl.BlockSpec((t, BRANCH_W), lambda i: (i, 0)),
                  pl.BlockSpec((t, BRANCH_W), lambda i: (i, 0)),
                  dspec, dspec, dspec,
                  pl.BlockSpec((t, GATE_LORA + BRANCH_W), lambda i: (i, 0)),
                  pl.BlockSpec((GATE_LORA, BRANCH_W), lambda i: (0, 0)),
                  pl.BlockSpec((N_DIR, BRANCH_W), lambda i: (0, 0)),
                  vspec, vspec,
                  pl.BlockSpec((BRANCH_W, BRANCH_W), lambda i: (0, 0))],
        out_specs=pl.BlockSpec((t, BRANCH_W), lambda i: (i, 0)),
        out_shape=jax.ShapeDtypeStruct((geom.m, BRANCH_W), BF16),
        compiler_params=_params(("parallel",)),
        name="rwkv_readout",
    )(y[0], y[1], r, k, v, p2, lp["rwkv_g_up"], lp["rwkv_r_k"].reshape(N_DIR, BRANCH_W),
      lp["rwkv_lnx_g"].reshape(1, BRANCH_W), lp["rwkv_lnx_b"].reshape(1, BRANCH_W),
      _head_sum_matrix(BRANCH_W, RWKV_HEAD))


def _rope_tables(geom):
    half = ATT_HEAD // 2
    nf = half // 2
    inv = ROPE_BASE ** (-jnp.arange(nf, dtype=F32) / nf)
    pos = jnp.arange(geom.seq, dtype=jnp.int32)
    row_ang = (pos // GRID_W).astype(F32)[:, None] * inv[None, :]
    col_ang = (pos % GRID_W).astype(F32)[:, None] * inv[None, :]
    cos = jnp.concatenate([jnp.cos(row_ang)] * 2 + [jnp.cos(col_ang)] * 2, axis=-1)
    sin = jnp.concatenate([-jnp.sin(row_ang), jnp.sin(row_ang), -jnp.sin(col_ang), jnp.sin(col_ang)], axis=-1)
    cos = jnp.concatenate([jnp.ones((geom.ctx_len, ATT_HEAD), F32), cos], axis=0)
    sin = jnp.concatenate([jnp.zeros((geom.ctx_len, ATT_HEAD), F32), sin], axis=0)
    return cos, sin


def _rotate(t, cos, sin):
    w = t.shape[-1]
    nf = ATT_HEAD // 4
    lane = lax.broadcasted_iota(jnp.int32, (1, w), 1)
    partner = jnp.where((lane % (2 * nf)) < nf, pltpu.roll(t, w - nf, 1), pltpu.roll(t, nf, 1))
    return t * cos + partner * sin


def _rope_kernel(p2_ref, kv_ref, cq_ref, sq_ref, ck_ref, sk_ref, q_o, k_o, v_o):
    q = p2_ref[:, GATE_LORA:]
    q_o[...] = (_rotate(q, cq_ref[...], sq_ref[...]) * (ATT_HEAD ** -0.5)).astype(q_o.dtype)
    kv = kv_ref[...]
    k_o[...] = _rotate(kv[:, :ATT_KV_W], ck_ref[...], sk_ref[...]).astype(k_o.dtype)
    v_o[...] = kv[:, ATT_KV_W:].astype(v_o.dtype)


def _rope(geom, p1, p2):
    t = ROW_TILE
    tb = geom.tiles_b
    cos, sin = _rope_tables(geom)
    cq, sq = jnp.tile(cos, (1, ATT_HEADS)), jnp.tile(sin, (1, ATT_HEADS))
    ck, sk = jnp.tile(cos, (1, ATT_KV_HEADS)), jnp.tile(sin, (1, ATT_KV_HEADS))
    qspec = pl.BlockSpec((t, BRANCH_W), lambda i: (i % tb, 0))
    kspec = pl.BlockSpec((t, ATT_KV_W), lambda i: (i % tb, 0))
    m = geom.m
    return pl.pallas_call(
        _rope_kernel,
        grid=(geom.tiles,),
        in_specs=[pl.BlockSpec((t, GATE_LORA + BRANCH_W), lambda i: (i, 0)),
                  pl.BlockSpec((t, 2 * ATT_KV_W), lambda i: (i, O_KV // (2 * ATT_KV_W))),
                  qspec, qspec, kspec, kspec],
        out_specs=[pl.BlockSpec((t, BRANCH_W), lambda i: (i, 0)),
                   pl.BlockSpec((t, ATT_KV_W), lambda i: (i, 0)),
                   pl.BlockSpec((t, ATT_KV_W), lambda i: (i, 0))],
        out_shape=[jax.ShapeDtypeStruct((m, BRANCH_W), BF16),
                   jax.ShapeDtypeStruct((m, ATT_KV_W), BF16),
                   jax.ShapeDtypeStruct((m, ATT_KV_W), BF16)],
        compiler_params=_params(("parallel",)),
        name="rope",
    )(p2, p1, cq, sq, ck, sk)


def _attn_kernel(q_ref, kc_ref, vc_ref, kp_ref, ko_ref, kn_ref, vp_ref, vo_ref, vn_ref, sink_ref, o_ref,
                 s_ref, p_ref, *, ctx_blocks, blocks_b):
    j = pl.program_id(1)
    q = q_ref[...]
    k_all = jnp.concatenate([kp_ref[...], ko_ref[...], kn_ref[...], kc_ref[...]], axis=0)
    v_all = jnp.concatenate([vp_ref[...], vo_ref[...], vn_ref[...], vc_ref[...]], axis=0)
    nloc = 3 * BLOCK
    slab = ATT_SLAB
    qi0 = lax.broadcasted_iota(jnp.int32, (slab, nloc), 0)
    ki = lax.broadcasted_iota(jnp.int32, (slab, nloc), 1)
    never = 4 * BLOCK
    prev_off = jnp.where(j > ctx_blocks, 0, never)
    own_hi = jnp.where(j >= ctx_blocks, 2 * BLOCK, BLOCK)
    next_off = 2 * BLOCK - jnp.where((j >= ctx_blocks) & (j < blocks_b - 1), 0, never)
    outs = []
    for g in range(ATT_KV_HEADS):
        gs = slice(g * ATT_HEAD, (g + 1) * ATT_HEAD)
        qg = jnp.concatenate([q[:, (g * ATT_REP + h) * ATT_HEAD:(g * ATT_REP + h + 1) * ATT_HEAD]
                              for h in range(ATT_REP)], axis=0)
        s_ref[g] = lax.dot_general(qg, k_all[:, gs], NT_DIMS, preferred_element_type=F32)
        dens = []
        for blk in range(ATT_REP * BLOCK // slab):
            rs = slice(blk * slab, (blk + 1) * slab)
            head = g * ATT_REP + blk * slab // BLOCK
            qi = qi0 + (blk * slab) % BLOCK
            valid = (((ki < BLOCK) & (ki >= qi + prev_off)) | ((ki >= BLOCK) & (ki < own_hi))
                     | ((ki >= 2 * BLOCK) & (ki <= qi + next_off)))
            s_loc = jnp.where(valid, s_ref[g, rs, :nloc], NEG_INF)
            s_ctx = s_ref[g, rs, nloc:]
            sink = sink_ref[head:head + 1, 0:1]
            mx = jnp.maximum(jnp.maximum(jnp.max(s_loc, axis=-1, keepdims=True),
                                         jnp.max(s_ctx, axis=-1, keepdims=True)), sink)
            e_loc = jnp.exp(s_loc - mx)
            e_ctx = jnp.exp(s_ctx - mx)
            dens.append(jnp.sum(e_loc, axis=-1, keepdims=True) + jnp.sum(e_ctx, axis=-1, keepdims=True)
                        + jnp.exp(sink - mx))
            p_ref[g, rs, :nloc] = e_loc.astype(BF16)
            p_ref[g, rs, nloc:] = e_ctx.astype(BF16)
        o = jnp.dot(p_ref[g], v_all[:, gs], preferred_element_type=F32) / jnp.concatenate(dens, axis=0)
        outs += [o[h * BLOCK:(h + 1) * BLOCK] for h in range(ATT_REP)]
    o_ref[...] = jnp.concatenate(outs, axis=-1).astype(o_ref.dtype)


def _attention(geom, q, k, v, sink):
    nb = geom.lt // BLOCK
    cb = geom.ctx_len // BLOCK
    row = lambda b, j: (b * nb + j, 0)
    prev = lambda b, j: (b * nb + jnp.maximum(j - 1, 0), 0)
    nxt = lambda b, j: (b * nb + jnp.minimum(j + 1, nb - 1), 0)
    ctx = lambda b, j: (b * (geom.lt // geom.ctx_len), 0)
    assert geom.lt % geom.ctx_len == 0
    kvs = lambda f: pl.BlockSpec((BLOCK, ATT_KV_W), f)
    cspec = pl.BlockSpec((geom.ctx_len, ATT_KV_W), ctx)
    sink_tab = jnp.broadcast_to(sink.astype(F32)[:, None], (ATT_HEADS, LANES))
    return pl.pallas_call(
        functools.partial(_attn_kernel, ctx_blocks=cb, blocks_b=nb),
        grid=(geom.batch, nb),
        in_specs=[pl.BlockSpec((BLOCK, BRANCH_W), row), cspec, cspec,
                  kvs(prev), kvs(row), kvs(nxt), kvs(prev), kvs(row), kvs(nxt),
                  pl.BlockSpec((ATT_HEADS, LANES), lambda b, j: (0, 0))],
        out_specs=pl.BlockSpec((BLOCK, BRANCH_W), row),
        out_shape=jax.ShapeDtypeStruct((geom.m, BRANCH_W), BF16),
        scratch_shapes=[pltpu.VMEM((ATT_KV_HEADS, ATT_REP * BLOCK, 3 * BLOCK + geom.ctx_len), F32),
                        pltpu.VMEM((ATT_KV_HEADS, ATT_REP * BLOCK, 3 * BLOCK + geom.ctx_len), BF16)],
        compiler_params=_params(("parallel", "parallel")),
        name="attention",
    )(q, k, v, k, k, k, v, v, v, sink_tab)


def _conv_kernel(u_ref, up_ref, un_ref, dw_ref, db_ref, lg_ref, lb_ref, o_ref, hp_ref, sh_ref,
                 *, tiles_b, ctx_tiles):
    j = pl.program_id(0) % tiles_b
    t = u_ref.shape[0]

    def glu(u):
        return u[:, :BRANCH_W] * _sigmoid(u[:, BRANCH_W:])

    at_start = (j == 0) | (j == ctx_tiles)
    at_end = (j == ctx_tiles - 1) | (j == tiles_b - 1)
    hp_ref[0:HALO] = jnp.where(at_start, 0.0, glu(up_ref[...]))
    hp_ref[HALO:HALO + t] = glu(u_ref[...])
    hp_ref[HALO + t:] = jnp.where(at_end, 0.0, glu(un_ref[...]))
    span = t + 2 * HALO - SUBLANES
    for r in range(1, SUBLANES):
        sh_ref[r, 0:span] = hp_ref[pl.ds(r, span)]
    for chunk in range(t // CONV_ROWS):
        acc = jnp.zeros((CONV_ROWS, BRANCH_W), F32) + db_ref[...]
        for tap in range(CONV_K):
            q, r = divmod(HALO - CONV_PAD + tap, SUBLANES)
            rows = pl.ds(chunk * CONV_ROWS + q * SUBLANES, CONV_ROWS)
            src = hp_ref[rows] if r == 0 else sh_ref[r, rows]
            acc = acc + src * dw_ref[tap:tap + 1]
        mean = jnp.mean(acc, axis=-1, keepdims=True)
        cen = acc - mean
        var = jnp.mean(cen * cen, axis=-1, keepdims=True)
        h = cen * lax.rsqrt(var + LN_EPS) * lg_ref[...] + lb_ref[...]
        o_ref[chunk * CONV_ROWS:(chunk + 1) * CONV_ROWS] = (h * _sigmoid(h)).astype(o_ref.dtype)


def _conv(geom, p4, lp):
    t = ROW_TILE
    nh = geom.m // HALO
    vspec = pl.BlockSpec((1, BRANCH_W), lambda i: (0, 0))
    return pl.pallas_call(
        functools.partial(_conv_kernel, tiles_b=geom.tiles_b, ctx_tiles=geom.ctx_tiles),
        grid=(geom.tiles,),
        in_specs=[pl.BlockSpec((t, 2 * BRANCH_W), lambda i: (i, 0)),
                  pl.BlockSpec((HALO, 2 * BRANCH_W), lambda i: (jnp.maximum(i * (t // HALO) - 1, 0), 0)),
                  pl.BlockSpec((HALO, 2 * BRANCH_W), lambda i: (jnp.minimum((i + 1) * (t // HALO), nh - 1), 0)),
                  pl.BlockSpec((CONV_K, BRANCH_W), lambda i: (0, 0)),
                  vspec, vspec, vspec],
        out_specs=pl.BlockSpec((t, BRANCH_W), lambda i: (i, 0)),
        out_shape=jax.ShapeDtypeStruct((geom.m, BRANCH_W), BF16),
        scratch_shapes=[pltpu.VMEM((t + 2 * HALO, BRANCH_W), F32),
                        pltpu.VMEM((SUBLANES, t + 2 * HALO, BRANCH_W), F32)],
        compiler_params=_params(("parallel",)),
        name="conformer_conv",
    )(p4, p4, p4, lp["conv_dw"], lp["conv_dw_b"].reshape(1, BRANCH_W),
      lp["conv_ln_g"].reshape(1, BRANCH_W), lp["conv_ln_b"].reshape(1, BRANCH_W))


def _dft_cos_sin(n, scale):
    idx = np.arange(n, dtype=np.int64)
    ang = 2.0 * np.pi * ((idx[:, None] * idx[None, :]) % n).astype(np.float64) / n
    return np.cos(ang) * scale, np.sin(ang) * scale


def _channel_dft():
    c, s = _dft_cos_sin(FNO_GROUP_W, FNO_GROUP_W ** -0.5)
    eye = np.eye(FNO_GROUPS)
    return jnp.asarray(np.concatenate([np.kron(eye, c), np.kron(eye, s)], axis=1), dtype=F32).astype(BF16)


def _dft_pos_kernel(c_ref, s_ref, gc_ref, gs_ref, o_ref):
    o_ref[0] = (jnp.dot(c_ref[...], gc_ref[0], preferred_element_type=F32)
                + jnp.dot(s_ref[...], gs_ref[0], preferred_element_type=F32)).astype(o_ref.dtype)


def _fourier(u):
    bsz, length, _ = u.shape
    gcs = _matmul(_RowPlan(None, 1024, "plain", bsz * length), u.reshape(bsz * length, BRANCH_W), _channel_dft(),
                  BF16, name="dft_channels")
    gcs = gcs.reshape(bsz, length, 2 * BRANCH_W)
    c, s = _dft_cos_sin(length, length ** -0.5)
    tm = _pick_tile(length, 512, ROW_TILE)
    return pl.pallas_call(
        _dft_pos_kernel,
        grid=(length // tm, bsz),
        in_specs=[pl.BlockSpec((tm, length), lambda i, b: (i, 0)),
                  pl.BlockSpec((tm, length), lambda i, b: (i, 0)),
                  pl.BlockSpec((1, length, BRANCH_W), lambda i, b: (b, 0, 0)),
                  pl.BlockSpec((1, length, BRANCH_W), lambda i, b: (b, 0, 1))],
        out_specs=pl.BlockSpec((1, tm, BRANCH_W), lambda i, b: (b, i, 0)),
        out_shape=jax.ShapeDtypeStruct((bsz, length, BRANCH_W), BF16),
        compiler_params=_params(("parallel", "parallel")),
        name="dft_positions",
    )(jnp.asarray(c, dtype=F32).astype(BF16), jnp.asarray(-s, dtype=F32).astype(BF16), gcs, gcs)


def _merge_kernel(f0, f1, f2, f3, w_ref, g0, g1, g2, g3, o_ref):
    acc = None
    for i, (f, g) in enumerate(((f0, g0), (f1, g1), (f2, g2), (f3, g3))):
        term = jnp.dot(f[...], w_ref[i], preferred_element_type=F32) * g[...].astype(F32)
        acc = term if acc is None else acc + term
    o_ref[...] = acc.astype(o_ref.dtype)


def _merge(feat_plan, plan, feats, w_branch, gate):
    assert (feat_plan.tm, feat_plan.n) == (plan.tm, plan.n)
    tn = 1024
    fspec = feat_plan.spec(BRANCH_W, lambda i, j: (i, 0))
    gspec = lambda br: plan.spec(tn, lambda i, j: (i, br * D_MODEL + j * tn))
    return pl.pallas_call(
        _merge_kernel,
        grid=(plan.n, D_MODEL // tn),
        in_specs=[fspec] * 4 + [pl.BlockSpec((N_BRANCH, BRANCH_W, tn), lambda i, j: (0, 0, j))]
        + [gspec(br) for br in range(N_BRANCH)],
        out_specs=plan.spec(tn, lambda i, j: (i, j * tn)),
        out_shape=jax.ShapeDtypeStruct((plan.rows, D_MODEL), BF16),
        compiler_params=_params(("parallel", "parallel")),
        name="branch_merge",
    )(*feats, w_branch, gate, gate, gate, gate)


def _mixer(geom, h, xall, modtab, lp, w_in, layer, norm2_g, latent_only):
    src, dst = ("latent", "packed") if latent_only else ("all", "all")
    rows_all = _RowPlan(geom, 1024)
    proj = lambda lo, hi, dt, name, **kw: _matmul_w32(kw.pop("plan", rows_all), h, w_in, layer, dt,
                                                      cols=(lo, hi - lo), name=name, **kw)
    p1 = proj(0, CTX_STATE_COLS, F32, "in_proj_state")
    p2 = proj(O_G, O_FNO, F32, "in_proj_gq")
    p3 = proj(O_FNO, O_CONV, BF16, "in_proj_fno")
    p4 = proj(O_CONV, O_GATE, F32, "in_proj_conv")
    gate = proj(O_GATE, IN_W, BF16, "in_proj_gate", epilogue="sigmoid", plan=_RowPlan(geom, 1024, src),
                out_plan=_RowPlan(geom, 1024, dst))

    ins = _rwkv_prep(geom, p1, lp)
    y = _rwkv_scan(geom, ins)
    rw = _rwkv_readout(geom, y, ins, p2, lp)

    q, k, v = _rope(geom, p1, p2)
    att = _attention(geom, q, k, v, lp["att_sink"])

    cv = _conv(geom, p4, lp)

    p3 = p3.reshape(geom.batch, geom.lt, BRANCH_W)
    fno = jnp.concatenate([_fourier(p3[:, :geom.ctx_len]), _fourier(p3[:, geom.ctx_len:])], axis=1)
    fno = fno.reshape(geom.m, BRANCH_W)

    half_src, half_dst = _RowPlan(geom, 512, src), _RowPlan(geom, 512, dst)
    mixed = _merge(half_src, half_dst, (fno, rw, att, cv), lp["w_branch"].astype(BF16), gate)
    return _matmul(half_dst, mixed, lp["w_out"].astype(BF16), F32, epilogue="resid", res=xall, res_plan=half_src,
                   modtab=modtab, gate_row=2, norm=(norm2_g, modtab, (3, 4)), name="out_proj")


def kernel(x, c, ctx, c_ctx, ada_w, ada_b, norm1_g, norm2_g, w_in, rwkv_mu, rwkv_w0, rwkv_w_up, rwkv_a0, rwkv_a_up, rwkv_k_k, rwkv_k_a, rwkv_r_k, rwkv_g_up, rwkv_lnx_g, rwkv_lnx_b, att_sink, conv_dw, conv_dw_b, conv_ln_g, conv_ln_b, w_branch, w_out, w_mlp1, w_mlp2, final_g):
    batch, seq, _ = x.shape
    geom = _Geom(batch, ctx.shape[1], seq)
    depth = w_in.shape[0]
    assert batch + 1 <= 8
    cond = jnp.zeros((8, D_MODEL), F32).at[:batch].set(c).at[batch].set(c_ctx)
    xall = jnp.concatenate([ctx, x], axis=1).reshape(geom.m, D_MODEL)
    modtabs = []
    for l in range(depth):
        mod = _ada_mod(cond, ada_w, l, ada_b[l])
        mod_x = mod[:batch].reshape(batch, 1, 6, D_MODEL)
        mod_c = jnp.broadcast_to(mod[batch].reshape(1, 1, 6, D_MODEL), (batch, 1, 6, D_MODEL))
        modtabs.append(jnp.concatenate([mod_c, mod_x], axis=1).reshape(2 * batch, 6, D_MODEL))
    h = _norm_mod(geom, xall, norm1_g[0], modtabs[0], rows=(0, 1))
    for l in range(depth):
        modtab = modtabs[l]
        lp = {
            "rwkv_mu": rwkv_mu[l], "rwkv_w0": rwkv_w0[l], "rwkv_w_up": rwkv_w_up[l],
            "rwkv_a0": rwkv_a0[l], "rwkv_a_up": rwkv_a_up[l], "rwkv_k_k": rwkv_k_k[l],
            "rwkv_k_a": rwkv_k_a[l], "rwkv_r_k": rwkv_r_k[l], "rwkv_g_up": rwkv_g_up[l],
            "rwkv_lnx_g": rwkv_lnx_g[l], "rwkv_lnx_b": rwkv_lnx_b[l], "att_sink": att_sink[l],
            "conv_dw": conv_dw[l], "conv_dw_b": conv_dw_b[l], "conv_ln_g": conv_ln_g[l],
            "conv_ln_b": conv_ln_b[l], "w_branch": w_branch[l], "w_out": w_out[l],
        }
        last = l + 1 == depth
        xall, h2 = _mixer(geom, h, xall, modtab, lp, w_in, l, norm2_g[l], latent_only=last)
        kind = "packed" if last else "all"
        hid = _matmul_w32(_RowPlan(geom, 1024, kind), h2, w_mlp1, l, BF16, epilogue="relu2", name="mlp_up")
        down = functools.partial(_matmul, a=hid, b=w_mlp2[l].astype(BF16), out_dtype=F32, epilogue="resid",
                                 res=xall, modtab=modtab, gate_row=5, name="mlp_down")
        if last:
            xall = down(_RowPlan(geom, 1024, kind))
        else:
            xall, h = down(_RowPlan(geom, 512), norm=(norm1_g[l + 1], modtabs[l + 1], (0, 1)))
    return _final_norm(geom, xall, final_g).reshape(batch, seq, D_MODEL)
```

```python
import functools
import math

import numpy as np
import jax
import jax.numpy as jnp
from jax import lax
from jax.experimental import pallas as pl
from jax.experimental.pallas import tpu as pltpu

F32 = jnp.float32
BF16 = jnp.bfloat16

D_MODEL = 2048
GRID_W = 64
NORM_EPS = 1e-6
N_BRANCH = 4
BRANCH_W = D_MODEL // N_BRANCH
FNO_GROUPS = 4
FNO_GROUP_W = BRANCH_W // FNO_GROUPS
RWKV_HEAD = 64
RWKV_HEADS = BRANCH_W // RWKV_HEAD
N_DIR = 2
DECAY_LORA = 64
AICL_LORA = 64
GATE_LORA = 128
DIR_LORA_W = DECAY_LORA + AICL_LORA
SHIFT_W = 3 * BRANCH_W + DIR_LORA_W
GN_EPS = 64e-5
ATT_HEAD = 64
ATT_HEADS = BRANCH_W // ATT_HEAD
ATT_KV_HEADS = 2
ATT_REP = ATT_HEADS // ATT_KV_HEADS
ATT_KV_W = ATT_KV_HEADS * ATT_HEAD
BLOCK = 128
ROPE_BASE = 10000.0
NEG_INF = -1e30
CONV_K = 31
CONV_PAD = (CONV_K - 1) // 2
LN_EPS = 1e-5

O_LORA = 3 * BRANCH_W
O_KV = O_LORA + N_DIR * DIR_LORA_W
CTX_STATE_COLS = O_KV + 2 * ATT_KV_W
O_G = CTX_STATE_COLS
O_Q = O_G + GATE_LORA
O_FNO = O_Q + BRANCH_W
O_CONV = O_FNO + BRANCH_W
O_GATE = O_CONV + 2 * BRANCH_W
IN_W = O_GATE + N_BRANCH * D_MODEL

LANES = 128
ROW_TILE = 256
SCAN_CHUNK = 64
PAIR_W = 2 * RWKV_HEAD
N_PAIRS = BRANCH_W // PAIR_W
HALO = 16
ATT_SLAB = 64
SUBLANES = 8
MXU_COLS = 256
CAST_ROWS = 256
CONV_ROWS = 32
PREP_ROWS = 64
SCAN_STEPS = 2
VMEM_LIMIT = 56 * 1024 * 1024

NT_DIMS = (((1,), (1,)), ((), ()))
NN_DIMS = (((1,), (0,)), ((), ()))


def _params(sem):
    return pltpu.CompilerParams(dimension_semantics=sem, vmem_limit_bytes=VMEM_LIMIT)


def _split2(a):
    hi = a.astype(BF16)
    lo = (a - hi.astype(F32)).astype(BF16)
    return hi, lo


def _dot3(a, b, dims=NN_DIMS):
    ah, al = _split2(a)
    bh, bl = _split2(b)
    dg = functools.partial(lax.dot_general, dimension_numbers=dims, preferred_element_type=F32)
    return dg(ah, bh) + (dg(ah, bl) + dg(al, bh))


def _dot_sel(a, sel_bf16):
    hi, lo = _split2(a)
    n = a.shape[0]
    both = jnp.dot(jnp.concatenate([hi, lo], axis=0), sel_bf16, preferred_element_type=F32)
    return both[:n] + both[n:]


def _sigmoid(x):
    return 1.0 / (1.0 + jnp.exp(-x))


def _ada_kernel(a_ref, w_ref, b_ref, o_ref):
    a = a_ref[...]
    s = a * _sigmoid(a)
    o_ref[...] = jnp.dot(s, w_ref[0], preferred_element_type=F32,
                         precision=lax.Precision.HIGHEST) + b_ref[...]


def _ada_mod(cond, w, layer, b):
    n = w.shape[2]
    tn = 1024
    return pl.pallas_call(
        _ada_kernel,
        grid=(n // tn,),
        in_specs=[pl.BlockSpec((8, D_MODEL), lambda j: (0, 0)),
                  pl.BlockSpec((1, D_MODEL, tn), lambda j: (layer, 0, j)),
                  pl.BlockSpec((1, tn), lambda j: (0, j))],
        out_specs=pl.BlockSpec((8, tn), lambda j: (0, j)),
        out_shape=jax.ShapeDtypeStruct((8, n), F32),
        compiler_params=_params(("parallel",)),
        name="ada_mod",
    )(cond, w, b.reshape(1, n))


class _Geom:
    def __init__(self, batch, ctx_len, seq):
        assert ctx_len % ROW_TILE == 0 and seq % ROW_TILE == 0
        assert seq % GRID_W == 0 and seq % BLOCK == 0 and ctx_len % BLOCK == 0
        self.batch = batch
        self.ctx_len = ctx_len
        self.seq = seq
        self.lt = ctx_len + seq
        self.m = batch * self.lt
        self.tiles_b = self.lt // ROW_TILE
        self.ctx_tiles = ctx_len // ROW_TILE
        self.tiles = batch * self.tiles_b

    def mod_row(self, i):
        return 2 * (i // self.tiles_b) + ((i % self.tiles_b) >= self.ctx_tiles).astype(jnp.int32)


def _norm_kernel(x_ref, g_ref, *rest, rows):
    x = x_ref[...]
    y = x * lax.rsqrt(jnp.mean(x * x, axis=-1, keepdims=True) + NORM_EPS) * g_ref[...]
    if rows is None:
        (o_ref,) = rest
    else:
        mod_ref, o_ref = rest
        mod = mod_ref[0]
        y = y * (1.0 + mod[rows[1]:rows[1] + 1]) + mod[rows[0]:rows[0] + 1]
    o_ref[...] = y.astype(o_ref.dtype)


def _norm_mod(geom, x, g, modtab, rows):
    return pl.pallas_call(
        functools.partial(_norm_kernel, rows=rows),
        grid=(geom.tiles,),
        in_specs=[pl.BlockSpec((ROW_TILE, D_MODEL), lambda i: (i, 0)),
                  pl.BlockSpec((1, D_MODEL), lambda i: (0, 0)),
                  pl.BlockSpec((1, 6, D_MODEL), lambda i: (geom.mod_row(i), 0, 0))],
        out_specs=pl.BlockSpec((ROW_TILE, D_MODEL), lambda i: (i, 0)),
        out_shape=jax.ShapeDtypeStruct((geom.m, D_MODEL), BF16),
        compiler_params=_params(("parallel",)),
        name="norm_mod",
    )(x, g.reshape(1, D_MODEL), modtab)


def _final_norm(geom, x, g):
    return pl.pallas_call(
        functools.partial(_norm_kernel, rows=None),
        grid=(geom.batch * geom.seq // ROW_TILE,),
        in_specs=[pl.BlockSpec((ROW_TILE, D_MODEL), lambda i: (i, 0)),
                  pl.BlockSpec((1, D_MODEL), lambda i: (0, 0))],
        out_specs=pl.BlockSpec((ROW_TILE, D_MODEL), lambda i: (i, 0)),
        out_shape=jax.ShapeDtypeStruct((geom.batch * geom.seq, D_MODEL), F32),
        compiler_params=_params(("parallel",)),
        name="final_norm",
    )(x, g.reshape(1, D_MODEL))


def _mm_store(acc, res_ref, mod_refs, o_ref, epilogue, gate_row, norm=None, cs=slice(None)):
    if epilogue == "resid":
        for s, mod_ref in enumerate(mod_refs):
            rs = slice(s * ROW_TILE, (s + 1) * ROW_TILE)
            xn = res_ref[rs, cs] + mod_ref[0, gate_row:gate_row + 1, cs] * acc[rs]
            o_ref[rs, cs] = xn
            if norm is not None:
                g_ref, nmod_refs, h_ref, rows = norm
                nmod = nmod_refs[s][0]
                y = xn * lax.rsqrt(jnp.mean(xn * xn, axis=-1, keepdims=True) + NORM_EPS) * g_ref[...]
                h_ref[rs] = (y * (1.0 + nmod[rows[1]:rows[1] + 1]) + nmod[rows[0]:rows[0] + 1]).astype(h_ref.dtype)
        return
    if epilogue == "sigmoid":
        acc = _sigmoid(acc)
    elif epilogue == "relu2":
        acc = jnp.square(jnp.maximum(acc, 0.0))
    o_ref[:, cs] = acc.astype(o_ref.dtype)


def _col_chunks(tn, norm):
    if norm is not None or tn % MXU_COLS:
        return [slice(None)]
    return [slice(c, c + MXU_COLS) for c in range(0, tn, MXU_COLS)]


def _mm_kernel(a_ref, b_ref, *rest, epilogue, gate_row, nk, n_sub, norm_rows):
    rest = list(rest)
    res_ref, mod_refs, norm = None, (), None
    if epilogue == "resid":
        res_ref, mod_refs, rest = rest[0], rest[1:1 + n_sub], rest[1 + n_sub:]
        if norm_rows is not None:
            norm = (rest[0], rest[1:1 + n_sub], rest[2 + n_sub], norm_rows)
            rest = [rest[1 + n_sub]] + rest[3 + n_sub:]
    o_ref = rest[0]
    chunks = _col_chunks(o_ref.shape[1], norm)
    if nk == 1:
        for cs in chunks:
            acc = jnp.dot(a_ref[...], b_ref[:, cs], preferred_element_type=F32)
            _mm_store(acc, res_ref, mod_refs, o_ref, epilogue, gate_row, norm, cs)
        return
    acc_ref = rest[1]
    k = pl.program_id(2)

    @pl.when(k == 0)
    def _():
        acc_ref[...] = jnp.dot(a_ref[...], b_ref[...], preferred_element_type=F32)

    @pl.when((k > 0) & (k < nk - 1))
    def _():
        acc_ref[...] += jnp.dot(a_ref[...], b_ref[...], preferred_element_type=F32)

    @pl.when(k == nk - 1)
    def _():
        for cs in chunks:
            acc = acc_ref[:, cs] + jnp.dot(a_ref[...], b_ref[:, cs], preferred_element_type=F32)
            _mm_store(acc, res_ref, mod_refs, o_ref, epilogue, gate_row, norm, cs)


def _mm_w32_kernel(a_ref, w_ref, o_ref, wb_ref, *, epilogue):
    @pl.when(pl.program_id(1) == 0)
    def _():
        def cast_rows(r, carry):
            rows = pl.ds(pl.multiple_of(r * CAST_ROWS, CAST_ROWS), CAST_ROWS)
            wb_ref[rows, :] = w_ref[0, rows, :].astype(BF16)
            return carry

        lax.fori_loop(0, wb_ref.shape[0] // CAST_ROWS, cast_rows, 0)

    for cs in _col_chunks(o_ref.shape[1], None):
        acc = jnp.dot(a_ref[...], wb_ref[:, cs], preferred_element_type=F32)
        _mm_store(acc, None, (), o_ref, epilogue, 0, cs=cs)


def _pick_tile(n, cap, unit=LANES):
    t = (min(cap, n) // unit) * unit
    while n % t:
        t -= unit
    return t


class _RowPlan:
    def __init__(self, geom, tm_cap, kind="all", rows=None):
        if kind in ("latent", "packed"):
            self.tm = _pick_tile(geom.seq, tm_cap, ROW_TILE)
            per_b = geom.seq // self.tm
            self.n = geom.batch * per_b
            self.rows = geom.m if kind == "latent" else geom.batch * geom.seq
            if kind == "latent":
                self.off = lambda i: (i // per_b) * geom.lt + geom.ctx_len + (i % per_b) * self.tm
                self.mod_row = geom.mod_row
            else:
                self.off = lambda i: i * self.tm
                self.mod_row = lambda t: 2 * (t // (geom.seq // ROW_TILE)) + 1
        else:
            self.rows = geom.m if kind == "all" else rows
            self.tm = _pick_tile(self.rows, tm_cap, ROW_TILE)
            self.n = self.rows // self.tm
            self.off = lambda i: i * self.tm
            self.mod_row = geom.mod_row if kind == "all" else None
        self.n_sub = self.tm // ROW_TILE

    def spec(self, width, at):
        def index(*g):
            i, col = at(*g)
            col = col if isinstance(col, int) else pl.multiple_of(col, LANES)
            return pl.multiple_of(self.off(i), ROW_TILE), col
        return pl.BlockSpec((pl.Element(self.tm), pl.Element(width)), index)

    def mod_specs(self, tn, at):
        return [pl.BlockSpec((1, 6, tn), lambda *g, s=s: (
            self.mod_row(self.off(at(*g)[0]) // ROW_TILE + s), 0, at(*g)[1])) for s in range(self.n_sub)]


def _matmul(plan, a, b, out_dtype, epilogue="none", res=None, modtab=None, gate_row=0, norm=None,
            res_plan=None, out_plan=None, tn_cap=1024, tk_cap=2048, name="matmul"):
    kdim = a.shape[1]
    n = b.shape[1]
    out_plan = out_plan or plan
    res_plan = res_plan or out_plan
    tm, m = plan.tm, out_plan.rows
    assert (out_plan.tm, out_plan.n, res_plan.tm, res_plan.n) == (tm, plan.n, tm, plan.n)
    tn = n if norm is not None else _pick_tile(n, tn_cap)
    tk = _pick_tile(kdim, tk_cap)
    nk = kdim // tk
    row_j = lambda i, j, k: (i, j * tn)
    tile_ij = lambda i, j, k: (i, j)
    in_specs = [plan.spec(tk, lambda i, j, k: (i, k * tk)),
                pl.BlockSpec((tk, tn), lambda i, j, k: (k, j))]
    args = [a, b]
    out_specs = out_plan.spec(tn, row_j)
    out_shape = jax.ShapeDtypeStruct((m, n), out_dtype)
    if epilogue == "resid":
        in_specs += [res_plan.spec(tn, row_j)] + out_plan.mod_specs(tn, tile_ij)
        args += [res] + [modtab] * plan.n_sub
        if norm is not None:
            gain, nmodtab, _ = norm
            in_specs += [pl.BlockSpec((1, tn), lambda i, j, k: (0, 0))] + out_plan.mod_specs(tn, tile_ij)
            args += [gain.reshape(1, n)] + [nmodtab] * plan.n_sub
            out_specs = [out_specs, out_plan.spec(tn, row_j)]
            out_shape = [out_shape, jax.ShapeDtypeStruct((m, n), BF16)]
    return pl.pallas_call(
        functools.partial(_mm_kernel, epilogue=epilogue, gate_row=gate_row, nk=nk, n_sub=plan.n_sub,
                          norm_rows=None if norm is None else norm[2]),
        grid=(plan.n, n // tn, nk),
        in_specs=in_specs,
        out_specs=out_specs,
        out_shape=out_shape,
        scratch_shapes=[pltpu.VMEM((tm, tn), F32)] if nk > 1 else [],
        compiler_params=_params(("parallel", "parallel", "arbitrary")),
        name=name,
    )(*args)


def _matmul_w32(plan, a, w, layer, out_dtype, epilogue="none", cols=None, out_plan=None, tn_cap=1024,
                name="matmul_w32"):
    kdim = a.shape[1]
    col0, n = cols if cols is not None else (0, w.shape[2])
    assert col0 % LANES == 0 and epilogue != "resid"
    out_plan = out_plan or plan
    assert (out_plan.tm, out_plan.n) == (plan.tm, plan.n)
    tn = _pick_tile(n, tn_cap)
    in_specs = [plan.spec(kdim, lambda j, i: (i, 0)),
                pl.BlockSpec((pl.Element(1), pl.Element(kdim), pl.Element(tn)),
                             lambda j, i: (layer, 0, pl.multiple_of(col0 + j * tn, LANES)))]
    args = [a, w]
    return pl.pallas_call(
        functools.partial(_mm_w32_kernel, epilogue=epilogue),
        grid=(n // tn, plan.n),
        in_specs=in_specs,
        out_specs=out_plan.spec(tn, lambda j, i: (i, j * tn)),
        out_shape=jax.ShapeDtypeStruct((out_plan.rows, n), out_dtype),
        scratch_shapes=[pltpu.VMEM((kdim, tn), BF16)],
        compiler_params=_params(("parallel", "arbitrary")),
        name=name,
    )(*args)


def _head_sum_matrix(width, head):
    idx = np.arange(width) // head
    return jnp.asarray((idx[:, None] == idx[None, :]).astype(np.float32), dtype=BF16)


def _prep_kernel(p_ref, lora_ref, halo_ref, hlora_ref, mu_ref, w0_ref, wup_ref, a0_ref, aup_ref,
                 kk_ref, ka_ref, hs_ref, r_o, lw_o, k_o, v_o, kkn_o, b_o, *, tiles_b, ctx_tiles):
    d = pl.program_id(0)
    j = pl.program_id(1) % tiles_b
    fwd = d == 0
    t = p_ref.shape[0]
    rkv_w = 3 * BRANCH_W
    at_start = (j == 0) | (j == ctx_tiles)
    at_end = (j == ctx_tiles - 1) | (j == tiles_b - 1)
    no_edge = (fwd & at_start) | (jnp.logical_not(fwd) & at_end)
    row = lax.broadcasted_iota(jnp.int32, (t, 1), 0)
    edge_row = jnp.where(fwd, 0, t - 1)

    def neighbour(x_ref, h_ref):
        edge = jnp.where(no_edge, 0.0, jnp.where(fwd, h_ref[7:8], h_ref[0:1]))
        x = x_ref[...]
        return jnp.where(row == edge_row, edge, jnp.where(fwd, pltpu.roll(x, 1, 0), pltpu.roll(x, t - 1, 0)))

    nb_rkv = neighbour(p_ref, halo_ref)
    nb_lora = neighbour(lora_ref, hlora_ref)
    mu = mu_ref[0]
    for c in range(t // PREP_ROWS):
        rs = slice(c * PREP_ROWS, (c + 1) * PREP_ROWS)
        f = p_ref[rs]
        f = f + mu[:, :rkv_w] * (nb_rkv[rs] - f)
        lo = lora_ref[rs]
        lo = lo + mu[:, rkv_w:] * (nb_lora[rs] - lo)
        r = f[:, 0:BRANCH_W]
        k = f[:, BRANCH_W:2 * BRANCH_W]
        v = f[:, 2 * BRANCH_W:rkv_w]
        w_raw = w0_ref[0] + _dot3(jnp.tanh(lo[:, :DECAY_LORA]), wup_ref[0])
        lw = -math.exp(-0.5) * _sigmoid(w_raw)
        a = _sigmoid(a0_ref[0] + _dot3(lo[:, DECAY_LORA:], aup_ref[0]))
        kk = k * kk_ref[...]
        norm = jnp.sqrt(_dot_sel(kk * kk, hs_ref[...]))
        kk = kk / jnp.maximum(norm, 1e-12)
        r_o[0, rs] = r.astype(r_o.dtype)
        lw_o[0, rs] = lw
        k_o[0, rs] = (k * (1.0 + (a - 1.0) * ka_ref[...])).astype(k_o.dtype)
        v_o[0, rs] = v.astype(v_o.dtype)
        kkn_o[0, rs] = kk.astype(kkn_o.dtype)
        b_o[0, rs] = (kk * a).astype(b_o.dtype)


def _rwkv_prep(geom, p1, lp):
    m = geom.m
    t = ROW_TILE
    tb = geom.tiles_b
    rkv_w = 3 * BRANCH_W
    lora_blk0 = O_LORA // DIR_LORA_W
    n_blk8 = m // 8

    def halo_idx(d, i):
        before = jnp.maximum(i * (t // 8) - 1, 0)
        after = jnp.minimum((i + 1) * (t // 8), n_blk8 - 1)
        return jnp.where(d == 0, before, after)

    out = jax.ShapeDtypeStruct((N_DIR, m, BRANCH_W), F32)
    ospec = pl.BlockSpec((1, t, BRANCH_W), lambda d, i: (d, i, 0))
    vec = lambda a: a.reshape(1, BRANCH_W)
    dvec = pl.BlockSpec((1, 1, BRANCH_W), lambda d, i: (d, 0, 0))
    return pl.pallas_call(
        functools.partial(_prep_kernel, tiles_b=tb, ctx_tiles=geom.ctx_tiles),
        grid=(N_DIR, geom.tiles),
        in_specs=[pl.BlockSpec((t, rkv_w), lambda d, i: (i, 0)),
                  pl.BlockSpec((t, DIR_LORA_W), lambda d, i: (i, lora_blk0 + d)),
                  pl.BlockSpec((8, rkv_w), lambda d, i: (halo_idx(d, i), 0)),
                  pl.BlockSpec((8, DIR_LORA_W), lambda d, i: (halo_idx(d, i), lora_blk0 + d)),
                  pl.BlockSpec((1, 1, SHIFT_W), lambda d, i: (d, 0, 0)),
                  dvec,
                  pl.BlockSpec((1, DECAY_LORA, BRANCH_W), lambda d, i: (d, 0, 0)),
                  dvec,
                  pl.BlockSpec((1, AICL_LORA, BRANCH_W), lambda d, i: (d, 0, 0)),
                  pl.BlockSpec((1, BRANCH_W), lambda d, i: (0, 0)),
                  pl.BlockSpec((1, BRANCH_W), lambda d, i: (0, 0)),
                  pl.BlockSpec((BRANCH_W, BRANCH_W), lambda d, i: (0, 0))],
        out_specs=[ospec] * 6,
        out_shape=[jax.ShapeDtypeStruct(out.shape, F32 if i == 1 else BF16) for i in range(6)],
        compiler_params=_params(("parallel", "parallel")),
        name="rwkv_prep",
    )(p1, p1, p1, p1, lp["rwkv_mu"].reshape(N_DIR, 1, SHIFT_W),
      lp["rwkv_w0"].reshape(N_DIR, 1, BRANCH_W), lp["rwkv_w_up"],
      lp["rwkv_a0"].reshape(N_DIR, 1, BRANCH_W), lp["rwkv_a_up"],
      vec(lp["rwkv_k_k"]), vec(lp["rwkv_k_a"]), _head_sum_matrix(BRANCH_W, RWKV_HEAD))


def _scan_kernel(*refs):
    yf_ref, yb_ref, h_ref = refs[12:]
    c = pl.program_id(1)
    C = SCAN_CHUNK
    W = 2 * C

    @pl.when(c == 0)
    def _():
        h_ref[...] = jnp.zeros_like(h_ref)

    rr = lax.broadcasted_iota(jnp.int32, (C, C), 0)
    cc = lax.broadcasted_iota(jnp.int32, (C, C), 1)
    lane = lax.broadcasted_iota(jnp.int32, (1, PAIR_W), 1)
    m_a = (lane < RWKV_HEAD).astype(F32)
    m_b = 1.0 - m_a
    r2 = lax.broadcasted_iota(jnp.int32, (W, W), 0)
    c2 = lax.broadcasted_iota(jnp.int32, (W, W), 1)
    same = (r2 // C) == (c2 // C)
    eye = (r2 == c2).astype(F32)

    def pairs(x):
        return [jnp.concatenate([x[:, p * PAIR_W:(p + 1) * PAIR_W] * m_a,
                                 x[:, p * PAIR_W:(p + 1) * PAIR_W] * m_b], axis=0) for p in range(N_PAIRS)]

    strict, incl, tri = [], [], []
    for sgn in (1, -1):
        dt = (r2 % C - c2 % C) * sgn
        strict += [(same & (dt > 0)).astype(F32)] * N_PAIRS
        incl += [(same & (dt >= 0)).astype(F32)] * N_PAIRS
        tri.append(jnp.where((rr - cc) * sgn >= 0, 1.0, 0.0).astype(BF16))
    strict = jnp.stack(strict)
    incl = jnp.stack(incl)

    h = h_ref[...]
    for step in range(SCAN_STEPS):
        rows = (pl.ds(step * C, C), pl.ds((SCAN_STEPS - 1 - step) * C, C))
        h = _scan_chunk(refs[:12], rows, tri, strict, incl, eye, pairs, h, (yf_ref, yb_ref))
    h_ref[...] = h


def _scan_chunk(in_refs, rows, tri, strict, incl, eye, pairs, h0, y_refs):
    C = SCAN_CHUNK
    W = 2 * C
    stacks = {name: [] for name in ("a", "b", "k", "r", "v", "bc", "kc", "pt")}
    for d in range(N_DIR):
        r_ref, lw_ref, k_ref, v_ref, kk_ref, b_ref = in_refs[d::N_DIR]
        rs = rows[d]
        lw = lw_ref[0, rs]
        lp_in = _dot_sel_lhs(tri[d], lw)
        tot = jnp.sum(lw, axis=0, keepdims=True)
        e_neg = jnp.exp(-lp_in)
        e_chk = jnp.exp(tot - lp_in)
        p_tot = jnp.exp(tot)
        stacks["a"] += pairs(-kk_ref[0, rs] * jnp.exp(lp_in - lw))
        stacks["b"] += pairs(b_ref[0, rs] * e_neg)
        stacks["k"] += pairs(k_ref[0, rs] * e_neg)
        stacks["r"] += pairs(r_ref[0, rs] * jnp.exp(lp_in))
        stacks["v"] += pairs(v_ref[0, rs])
        stacks["bc"] += pairs(b_ref[0, rs] * e_chk)
        stacks["kc"] += pairs(k_ref[0, rs] * e_chk)
        stacks["pt"] += [p_tot[:, p * PAIR_W:(p + 1) * PAIR_W] for p in range(N_PAIRS)]
    a_s, b_s, k_s, r_s, v_s, bc_s, kc_s, p_tot = (jnp.stack(stacks[n]) for n in
                                                  ("a", "b", "k", "r", "v", "bc", "kc", "pt"))

    a_b, b_b, k_b, r_b, v_b, bc_b, kc_b = (x.astype(BF16) for x in (a_s, b_s, k_s, r_s, v_s, bc_s, kc_s))
    big = _bdot(jnp.concatenate([a_b, r_b], axis=1), jnp.concatenate([b_b, k_b], axis=1), BNT_DIMS)
    l_ab = big[:, :W, :W] * strict
    l_ak = big[:, :W, W:] * strict
    m_rb = (big[:, W:, :W] * incl).astype(BF16)
    m_rk = big[:, W:, W:] * incl
    t_inv = eye + l_ab
    pw_b = l_ab.astype(BF16)
    pw_b = _bdot(pw_b, pw_b).astype(BF16)
    for _ in range(int(math.log2(C)) - 2):
        both = _bdot(jnp.concatenate([t_inv.astype(BF16), pw_b], axis=1), pw_b)
        t_inv = t_inv + both[:, :W]
        pw_b = both[:, W:].astype(BF16)
    t_inv = t_inv + _bdot(t_inv, pw_b)
    x1 = _bdot(t_inv, jnp.concatenate([a_b, _bdot(l_ak, v_b).astype(BF16)], axis=2)).astype(BF16)
    x2 = _bdot(m_rb, x1)
    r_hat = r_s + x2[:, :, :PAIR_W]
    y0 = x2[:, :, PAIR_W:] + _bdot(m_rk, v_b)
    x3 = _bdot(bc_b, x1, BTN_DIMS)
    g = eye * p_tot + x3[:, :, :PAIR_W]
    h_inc = x3[:, :, PAIR_W:] + _bdot(kc_b, v_b, BTN_DIMS)
    x4 = _bdot(jnp.concatenate([r_hat, g], axis=1), h0)
    ys = x4[:, :W] + y0
    for d, y_ref in enumerate(y_refs):
        for p in range(N_PAIRS):
            y_ref[rows[d], p * PAIR_W:(p + 1) * PAIR_W] = ys[d * N_PAIRS + p, :C] + ys[d * N_PAIRS + p, C:]
    return x4[:, W:] + h_inc


BNN_DIMS = (((2,), (1,)), ((0,), (0,)))
BNT_DIMS = (((2,), (2,)), ((0,), (0,)))
BTN_DIMS = (((1,), (1,)), ((0,), (0,)))


def _bdot(a, b, dims=BNN_DIMS):
    return lax.dot_general(a.astype(BF16), b.astype(BF16), dims, preferred_element_type=F32)


def _dot_sel_lhs(sel_bf16, a):
    hi = a.astype(BF16)
    r1 = a - hi.astype(F32)
    mid = r1.astype(BF16)
    lo = (r1 - mid.astype(F32)).astype(BF16)
    dg = functools.partial(jnp.dot, preferred_element_type=F32)
    return dg(sel_bf16, hi) + (dg(sel_bf16, mid) + dg(sel_bf16, lo))


def _rwkv_scan(geom, ins):
    C = SCAN_CHUNK * SCAN_STEPS
    assert geom.ctx_len % C == 0 and geom.seq % C == 0
    nch = geom.lt // C
    nctx = geom.ctx_len // C

    def rev(c):
        return jnp.where(c < nctx, nctx - 1 - c, nch - 1 + nctx - c)

    fwd = pl.BlockSpec((1, C, BRANCH_W), lambda b, c: (0, b * nch + c, 0))
    bwd = pl.BlockSpec((1, C, BRANCH_W), lambda b, c: (1, b * nch + rev(c), 0))
    out = jax.ShapeDtypeStruct((geom.m, BRANCH_W), F32)
    return pl.pallas_call(
        _scan_kernel,
        grid=(geom.batch, nch),
        in_specs=[fwd, bwd] * 6,
        out_specs=[pl.BlockSpec((C, BRANCH_W), lambda b, c: (b * nch + c, 0)),
                   pl.BlockSpec((C, BRANCH_W), lambda b, c: (b * nch + rev(c), 0))],
        out_shape=[out, out],
        scratch_shapes=[pltpu.VMEM((N_DIR * N_PAIRS, PAIR_W, PAIR_W), F32)],
        compiler_params=_params(("parallel", "arbitrary")),
        name="rwkv_scan",
    )(*[a for a in ins for _ in range(N_DIR)])


def _readout_kernel(yf_ref, yb_ref, r_ref, k_ref, v_ref, p2_ref, gup_ref, rk_ref, lg_ref, lb_ref, hs_ref, o_ref):
    hs = hs_ref[...]
    y = yf_ref[...] + yb_ref[...]
    inv_n = 1.0 / RWKV_HEAD
    mean = _dot_sel(y, hs) * inv_n
    yc = y - mean
    var = _dot_sel(yc * yc, hs) * inv_n
    yn = yc * lax.rsqrt(var + GN_EPS) * lg_ref[...] + lb_ref[...]
    bonus = jnp.zeros_like(y)
    for d in range(N_DIR):
        rk = r_ref[d].astype(F32) * k_ref[d].astype(F32) * rk_ref[d:d + 1]
        bonus = bonus + _dot_sel(rk, hs) * v_ref[d].astype(F32)
    g = _dot3(_sigmoid(p2_ref[:, 0:GATE_LORA]), gup_ref[...])
    o_ref[...] = ((yn + bonus) * g).astype(o_ref.dtype)


def _rwkv_readout(geom, y, ins, p2, lp):
    t = ROW_TILE
    dspec = pl.BlockSpec((N_DIR, t, BRANCH_W), lambda i: (0, i, 0))
    vspec = pl.BlockSpec((1, BRANCH_W), lambda i: (0, 0))
    r, _, k, v, _, _ = ins
    return pl.pallas_call(
        _readout_kernel,
        grid=(geom.tiles,),
        in_specs=[pl.BlockSpec((t, BRANCH_W), lambda i: (i, 0)),
                  pl.BlockSpec((t, BRANCH_W), lambda i: (i, 0)),
                  dspec, dspec, dspec,
                  pl.BlockSpec((t, GATE_LORA + BRANCH_W), lambda i: (i, 0)),
                  pl.BlockSpec((GATE_LORA, BRANCH_W), lambda i: (0, 0)),
                  pl.BlockSpec((N_DIR, BRANCH_W), lambda i: (0, 0)),
                  vspec, vspec,
                  pl.BlockSpec((BRANCH_W, BRANCH_W), lambda i: (0, 0))],
        out_specs=pl.BlockSpec((t, BRANCH_W), lambda i: (i, 0)),
        out_shape=jax.ShapeDtypeStruct((geom.m, BRANCH_W), BF16),
        compiler_params=_params(("parallel",)),
        name="rwkv_readout",
    )(y[0], y[1], r, k, v, p2, lp["rwkv_g_up"], lp["rwkv_r_k"].reshape(N_DIR, BRANCH_W),
      lp["rwkv_lnx_g"].reshape(1, BRANCH_W), lp["rwkv_lnx_b"].reshape(1, BRANCH_W),
      _head_sum_matrix(BRANCH_W, RWKV_HEAD))


def _rope_tables(geom):
    half = ATT_HEAD // 2
    nf = half // 2
    inv = ROPE_BASE ** (-jnp.arange(nf, dtype=F32) / nf)
    pos = jnp.arange(geom.seq, dtype=jnp.int32)
    row_ang = (pos // GRID_W).astype(F32)[:, None] * inv[None, :]
    col_ang = (pos % GRID_W).astype(F32)[:, None] * inv[None, :]
    cos = jnp.concatenate([jnp.cos(row_ang)] * 2 + [jnp.cos(col_ang)] * 2, axis=-1)
    sin = jnp.concatenate([-jnp.sin(row_ang), jnp.sin(row_ang), -jnp.sin(col_ang), jnp.sin(col_ang)], axis=-1)
    cos = jnp.concatenate([jnp.ones((geom.ctx_len, ATT_HEAD), F32), cos], axis=0)
    sin = jnp.concatenate([jnp.zeros((geom.ctx_len, ATT_HEAD), F32), sin], axis=0)
    return cos, sin


def _rotate(t, cos, sin):
    w = t.shape[-1]
    nf = ATT_HEAD // 4
    lane = lax.broadcasted_iota(jnp.int32, (1, w), 1)
    partner = jnp.where((lane % (2 * nf)) < nf, pltpu.roll(t, w - nf, 1), pltpu.roll(t, nf, 1))
    return t * cos + partner * sin


def _rope_kernel(p2_ref, kv_ref, cq_ref, sq_ref, ck_ref, sk_ref, q_o, k_o, v_o):
    q = p2_ref[:, GATE_LORA:]
    q_o[...] = (_rotate(q, cq_ref[...], sq_ref[...]) * (ATT_HEAD ** -0.5)).astype(q_o.dtype)
    kv = kv_ref[...]
    k_o[...] = _rotate(kv[:, :ATT_KV_W], ck_ref[...], sk_ref[...]).astype(k_o.dtype)
    v_o[...] = kv[:, ATT_KV_W:].astype(v_o.dtype)


def _rope(geom, p1, p2):
    t = ROW_TILE
    tb = geom.tiles_b
    cos, sin = _rope_tables(geom)
    cq, sq = jnp.tile(cos, (1, ATT_HEADS)), jnp.tile(sin, (1, ATT_HEADS))
    ck, sk = jnp.tile(cos, (1, ATT_KV_HEADS)), jnp.tile(sin, (1, ATT_KV_HEADS))
    qspec = pl.BlockSpec((t, BRANCH_W), lambda i: (i % tb, 0))
    kspec = pl.BlockSpec((t, ATT_KV_W), lambda i: (i % tb, 0))
    m = geom.m
    return pl.pallas_call(
        _rope_kernel,
        grid=(geom.tiles,),
        in_specs=[pl.BlockSpec((t, GATE_LORA + BRANCH_W), lambda i: (i, 0)),
                  pl.BlockSpec((t, 2 * ATT_KV_W), lambda i: (i, O_KV // (2 * ATT_KV_W))),
                  qspec, qspec, kspec, kspec],
        out_specs=[pl.BlockSpec((t, BRANCH_W), lambda i: (i, 0)),
                   pl.BlockSpec((t, ATT_KV_W), lambda i: (i, 0)),
                   pl.BlockSpec((t, ATT_KV_W), lambda i: (i, 0))],
        out_shape=[jax.ShapeDtypeStruct((m, BRANCH_W), BF16),
                   jax.ShapeDtypeStruct((m, ATT_KV_W), BF16),
                   jax.ShapeDtypeStruct((m, ATT_KV_W), BF16)],
        compiler_params=_params(("parallel",)),
        name="rope",
    )(p2, p1, cq, sq, ck, sk)


def _attn_kernel(q_ref, kc_ref, vc_ref, kp_ref, ko_ref, kn_ref, vp_ref, vo_ref, vn_ref, sink_ref, o_ref,
                 s_ref, p_ref, *, ctx_blocks, blocks_b):
    j = pl.program_id(1)
    q = q_ref[...]
    k_all = jnp.concatenate([kp_ref[...], ko_ref[...], kn_ref[...], kc_ref[...]], axis=0)
    v_all = jnp.concatenate([vp_ref[...], vo_ref[...], vn_ref[...], vc_ref[...]], axis=0)
    nloc = 3 * BLOCK
    slab = ATT_SLAB
    qi0 = lax.broadcasted_iota(jnp.int32, (slab, nloc), 0)
    ki = lax.broadcasted_iota(jnp.int32, (slab, nloc), 1)
    never = 4 * BLOCK
    prev_off = jnp.where(j > ctx_blocks, 0, never)
    own_hi = jnp.where(j >= ctx_blocks, 2 * BLOCK, BLOCK)
    next_off = 2 * BLOCK - jnp.where((j >= ctx_blocks) & (j < blocks_b - 1), 0, never)
    outs = []
    for g in range(ATT_KV_HEADS):
        gs = slice(g * ATT_HEAD, (g + 1) * ATT_HEAD)
        qg = jnp.concatenate([q[:, (g * ATT_REP + h) * ATT_HEAD:(g * ATT_REP + h + 1) * ATT_HEAD]
                              for h in range(ATT_REP)], axis=0)
        s_ref[g] = lax.dot_general(qg, k_all[:, gs], NT_DIMS, preferred_element_type=F32)
        dens = []
        for blk in range(ATT_REP * BLOCK // slab):
            rs = slice(blk * slab, (blk + 1) * slab)
            head = g * ATT_REP + blk * slab // BLOCK
            qi = qi0 + (blk * slab) % BLOCK
            valid = (((ki < BLOCK) & (ki >= qi + prev_off)) | ((ki >= BLOCK) & (ki < own_hi))
                     | ((ki >= 2 * BLOCK) & (ki <= qi + next_off)))
            s_loc = jnp.where(valid, s_ref[g, rs, :nloc], NEG_INF)
            s_ctx = s_ref[g, rs, nloc:]
            sink = sink_ref[head:head + 1, 0:1]
            mx = jnp.maximum(jnp.maximum(jnp.max(s_loc, axis=-1, keepdims=True),
                                         jnp.max(s_ctx, axis=-1, keepdims=True)), sink)
            e_loc = jnp.exp(s_loc - mx)
            e_ctx = jnp.exp(s_ctx - mx)
            dens.append(jnp.sum(e_loc, axis=-1, keepdims=True) + jnp.sum(e_ctx, axis=-1, keepdims=True)
                        + jnp.exp(sink - mx))
            p_ref[g, rs, :nloc] = e_loc.astype(BF16)
            p_ref[g, rs, nloc:] = e_ctx.astype(BF16)
        o = jnp.dot(p_ref[g], v_all[:, gs], preferred_element_type=F32) / jnp.concatenate(dens, axis=0)
        outs += [o[h * BLOCK:(h + 1) * BLOCK] for h in range(ATT_REP)]
    o_ref[...] = jnp.concatenate(outs, axis=-1).astype(o_ref.dtype)


def _attention(geom, q, k, v, sink):
    nb = geom.lt // BLOCK
    cb = geom.ctx_len // BLOCK
    row = lambda b, j: (b * nb + j, 0)
    prev = lambda b, j: (b * nb + jnp.maximum(j - 1, 0), 0)
    nxt = lambda b, j: (b * nb + jnp.minimum(j + 1, nb - 1), 0)
    ctx = lambda b, j: (b * (geom.lt // geom.ctx_len), 0)
    assert geom.lt % geom.ctx_len == 0
    kvs = lambda f: pl.BlockSpec((BLOCK, ATT_KV_W), f)
    cspec = pl.BlockSpec((geom.ctx_len, ATT_KV_W), ctx)
    sink_tab = jnp.broadcast_to(sink.astype(F32)[:, None], (ATT_HEADS, LANES))
    return pl.pallas_call(
        functools.partial(_attn_kernel, ctx_blocks=cb, blocks_b=nb),
        grid=(geom.batch, nb),
        in_specs=[pl.BlockSpec((BLOCK, BRANCH_W), row), cspec, cspec,
                  kvs(prev), kvs(row), kvs(nxt), kvs(prev), kvs(row), kvs(nxt),
                  pl.BlockSpec((ATT_HEADS, LANES), lambda b, j: (0, 0))],
        out_specs=pl.BlockSpec((BLOCK, BRANCH_W), row),
        out_shape=jax.ShapeDtypeStruct((geom.m, BRANCH_W), BF16),
        scratch_shapes=[pltpu.VMEM((ATT_KV_HEADS, ATT_REP * BLOCK, 3 * BLOCK + geom.ctx_len), F32),
                        pltpu.VMEM((ATT_KV_HEADS, ATT_REP * BLOCK, 3 * BLOCK + geom.ctx_len), BF16)],
        compiler_params=_params(("parallel", "parallel")),
        name="attention",
    )(q, k, v, k, k, k, v, v, v, sink_tab)


def _conv_kernel(u_ref, up_ref, un_ref, dw_ref, db_ref, lg_ref, lb_ref, o_ref, hp_ref, sh_ref,
                 *, tiles_b, ctx_tiles):
    j = pl.program_id(0) % tiles_b
    t = u_ref.shape[0]

    def glu(u):
        return u[:, :BRANCH_W] * _sigmoid(u[:, BRANCH_W:])

    at_start = (j == 0) | (j == ctx_tiles)
    at_end = (j == ctx_tiles - 1) | (j == tiles_b - 1)
    hp_ref[0:HALO] = jnp.where(at_start, 0.0, glu(up_ref[...]))
    hp_ref[HALO:HALO + t] = glu(u_ref[...])
    hp_ref[HALO + t:] = jnp.where(at_end, 0.0, glu(un_ref[...]))
    span = t + 2 * HALO - SUBLANES
    for r in range(1, SUBLANES):
        sh_ref[r, 0:span] = hp_ref[pl.ds(r, span)]
    for chunk in range(t // CONV_ROWS):
        acc = jnp.zeros((CONV_ROWS, BRANCH_W), F32) + db_ref[...]
        for tap in range(CONV_K):
            q, r = divmod(HALO - CONV_PAD + tap, SUBLANES)
            rows = pl.ds(chunk * CONV_ROWS + q * SUBLANES, CONV_ROWS)
            src = hp_ref[rows] if r == 0 else sh_ref[r, rows]
            acc = acc + src * dw_ref[tap:tap + 1]
        mean = jnp.mean(acc, axis=-1, keepdims=True)
        cen = acc - mean
        var = jnp.mean(cen * cen, axis=-1, keepdims=True)
        h = cen * lax.rsqrt(var + LN_EPS) * lg_ref[...] + lb_ref[...]
        o_ref[chunk * CONV_ROWS:(chunk + 1) * CONV_ROWS] = (h * _sigmoid(h)).astype(o_ref.dtype)


def _conv(geom, p4, lp):
    t = ROW_TILE
    nh = geom.m // HALO
    vspec = pl.BlockSpec((1, BRANCH_W), lambda i: (0, 0))
    return pl.pallas_call(
        functools.partial(_conv_kernel, tiles_b=geom.tiles_b, ctx_tiles=geom.ctx_tiles),
        grid=(geom.tiles,),
        in_specs=[pl.BlockSpec((t, 2 * BRANCH_W), lambda i: (i, 0)),
                  pl.BlockSpec((HALO, 2 * BRANCH_W), lambda i: (jnp.maximum(i * (t // HALO) - 1, 0), 0)),
                  pl.BlockSpec((HALO, 2 * BRANCH_W), lambda i: (jnp.minimum((i + 1) * (t // HALO), nh - 1), 0)),
                  pl.BlockSpec((CONV_K, BRANCH_W), lambda i: (0, 0)),
                  vspec, vspec, vspec],
        out_specs=pl.BlockSpec((t, BRANCH_W), lambda i: (i, 0)),
        out_shape=jax.ShapeDtypeStruct((geom.m, BRANCH_W), BF16),
        scratch_shapes=[pltpu.VMEM((t + 2 * HALO, BRANCH_W), F32),
                        pltpu.VMEM((SUBLANES, t + 2 * HALO, BRANCH_W), F32)],
        compiler_params=_params(("parallel",)),
        name="conformer_conv",
    )(p4, p4, p4, lp["conv_dw"], lp["conv_dw_b"].reshape(1, BRANCH_W),
      lp["conv_ln_g"].reshape(1, BRANCH_W), lp["conv_ln_b"].reshape(1, BRANCH_W))


def _dft_cos_sin(n, scale):
    idx = np.arange(n, dtype=np.int64)
    ang = 2.0 * np.pi * ((idx[:, None] * idx[None, :]) % n).astype(np.float64) / n
    return np.cos(ang) * scale, np.sin(ang) * scale


def _channel_dft():
    c, s = _dft_cos_sin(FNO_GROUP_W, FNO_GROUP_W ** -0.5)
    eye = np.eye(FNO_GROUPS)
    return jnp.asarray(np.concatenate([np.kron(eye, c), np.kron(eye, s)], axis=1), dtype=F32).astype(BF16)


def _dft_pos_kernel(c_ref, s_ref, gc_ref, gs_ref, o_ref):
    o_ref[0] = (jnp.dot(c_ref[...], gc_ref[0], preferred_element_type=F32)
                + jnp.dot(s_ref[...], gs_ref[0], preferred_element_type=F32)).astype(o_ref.dtype)


def _fourier(u):
    bsz, length, _ = u.shape
    gcs = _matmul(_RowPlan(None, 1024, "plain", bsz * length), u.reshape(bsz * length, BRANCH_W), _channel_dft(),
                  BF16, name="dft_channels")
    gcs = gcs.reshape(bsz, length, 2 * BRANCH_W)
    c, s = _dft_cos_sin(length, length ** -0.5)
    tm = _pick_tile(length, 512, ROW_TILE)
    return pl.pallas_call(
        _dft_pos_kernel,
        grid=(length // tm, bsz),
        in_specs=[pl.BlockSpec((tm, length), lambda i, b: (i, 0)),
                  pl.BlockSpec((tm, length), lambda i, b: (i, 0)),
                  pl.BlockSpec((1, length, BRANCH_W), lambda i, b: (b, 0, 0)),
                  pl.BlockSpec((1, length, BRANCH_W), lambda i, b: (b, 0, 1))],
        out_specs=pl.BlockSpec((1, tm, BRANCH_W), lambda i, b: (b, i, 0)),
        out_shape=jax.ShapeDtypeStruct((bsz, length, BRANCH_W), BF16),
        compiler_params=_params(("parallel", "parallel")),
        name="dft_positions",
    )(jnp.asarray(c, dtype=F32).astype(BF16), jnp.asarray(-s, dtype=F32).astype(BF16), gcs, gcs)


def _merge_kernel(f0, f1, f2, f3, w_ref, g0, g1, g2, g3, o_ref):
    acc = None
    for i, (f, g) in enumerate(((f0, g0), (f1, g1), (f2, g2), (f3, g3))):
        term = jnp.dot(f[...], w_ref[i], preferred_element_type=F32) * g[...].astype(F32)
        acc = term if acc is None else acc + term
    o_ref[...] = acc.astype(o_ref.dtype)


def _merge(feat_plan, plan, feats, w_branch, gate):
    assert (feat_plan.tm, feat_plan.n) == (plan.tm, plan.n)
    tn = 1024
    fspec = feat_plan.spec(BRANCH_W, lambda i, j: (i, 0))
    gspec = lambda br: plan.spec(tn, lambda i, j: (i, br * D_MODEL + j * tn))
    return pl.pallas_call(
        _merge_kernel,
        grid=(plan.n, D_MODEL // tn),
        in_specs=[fspec] * 4 + [pl.BlockSpec((N_BRANCH, BRANCH_W, tn), lambda i, j: (0, 0, j))]
        + [gspec(br) for br in range(N_BRANCH)],
        out_specs=plan.spec(tn, lambda i, j: (i, j * tn)),
        out_shape=jax.ShapeDtypeStruct((plan.rows, D_MODEL), BF16),
        compiler_params=_params(("parallel", "parallel")),
        name="branch_merge",
    )(*feats, w_branch, gate, gate, gate, gate)


def _mixer(geom, h, xall, modtab, lp, w_in, layer, norm2_g, latent_only):
    src, dst = ("latent", "packed") if latent_only else ("all", "all")
    rows_all = _RowPlan(geom, 1024)
    proj = lambda lo, hi, dt, name, **kw: _matmul_w32(kw.pop("plan", rows_all), h, w_in, layer, dt,
                                                      cols=(lo, hi - lo), name=name, **kw)
    p1 = proj(0, CTX_STATE_COLS, F32, "in_proj_state")
    p2 = proj(O_G, O_FNO, F32, "in_proj_gq")
    p3 = proj(O_FNO, O_CONV, BF16, "in_proj_fno")
    p4 = proj(O_CONV, O_GATE, F32, "in_proj_conv")
    gate = proj(O_GATE, IN_W, BF16, "in_proj_gate", epilogue="sigmoid", plan=_RowPlan(geom, 1024, src),
                out_plan=_RowPlan(geom, 1024, dst))

    ins = _rwkv_prep(geom, p1, lp)
    y = _rwkv_scan(geom, ins)
    rw = _rwkv_readout(geom, y, ins, p2, lp)

    q, k, v = _rope(geom, p1, p2)
    att = _attention(geom, q, k, v, lp["att_sink"])

    cv = _conv(geom, p4, lp)

    p3 = p3.reshape(geom.batch, geom.lt, BRANCH_W)
    fno = jnp.concatenate([_fourier(p3[:, :geom.ctx_len]), _fourier(p3[:, geom.ctx_len:])], axis=1)
    fno = fno.reshape(geom.m, BRANCH_W)

    half_src, half_dst = _RowPlan(geom, 512, src), _RowPlan(geom, 512, dst)
    mixed = _merge(half_src, half_dst, (fno, rw, att, cv), lp["w_branch"].astype(BF16), gate)
    return _matmul(half_dst, mixed, lp["w_out"].astype(BF16), F32, epilogue="resid", res=xall, res_plan=half_src,
                   modtab=modtab, gate_row=2, norm=(norm2_g, modtab, (3, 4)), name="out_proj")


def kernel(x, c, ctx, c_ctx, ada_w, ada_b, norm1_g, norm2_g, w_in, rwkv_mu, rwkv_w0, rwkv_w_up, rwkv_a0, rwkv_a_up, rwkv_k_k, rwkv_k_a, rwkv_r_k, rwkv_g_up, rwkv_lnx_g, rwkv_lnx_b, att_sink, conv_dw, conv_dw_b, conv_ln_g, conv_ln_b, w_branch, w_out, w_mlp1, w_mlp2, final_g):
    batch, seq, _ = x.shape
    geom = _Geom(batch, ctx.shape[1], seq)
    depth = w_in.shape[0]
    assert batch + 1 <= 8
    cond = jnp.zeros((8, D_MODEL), F32).at[:batch].set(c).at[batch].set(c_ctx)
    xall = jnp.concatenate([ctx, x], axis=1).reshape(geom.m, D_MODEL)
    modtabs = []
    for l in range(depth):
        mod = _ada_mod(cond, ada_w, l, ada_b[l])
        mod_x = mod[:batch].reshape(batch, 1, 6, D_MODEL)
        mod_c = jnp.broadcast_to(mod[batch].reshape(1, 1, 6, D_MODEL), (batch, 1, 6, D_MODEL))
        modtabs.append(jnp.concatenate([mod_c, mod_x], axis=1).reshape(2 * batch, 6, D_MODEL))
    h = _norm_mod(geom, xall, norm1_g[0], modtabs[0], rows=(0, 1))
    for l in range(depth):
        modtab = modtabs[l]
        lp = {
            "rwkv_mu": rwkv_mu[l], "rwkv_w0": rwkv_w0[l], "rwkv_w_up": rwkv_w_up[l],
            "rwkv_a0": rwkv_a0[l], "rwkv_a_up": rwkv_a_up[l], "rwkv_k_k": rwkv_k_k[l],
            "rwkv_k_a": rwkv_k_a[l], "rwkv_r_k": rwkv_r_k[l], "rwkv_g_up": rwkv_g_up[l],
            "rwkv_lnx_g": rwkv_lnx_g[l], "rwkv_lnx_b": rwkv_lnx_b[l], "att_sink": att_sink[l],
            "conv_dw": conv_dw[l], "conv_dw_b": conv_dw_b[l], "conv_ln_g": conv_ln_g[l],
            "conv_ln_b": conv_ln_b[l], "w_branch": w_branch[l], "w_out": w_out[l],
        }
        last = l + 1 == depth
        xall, h2 = _mixer(geom, h, xall, modtab, lp, w_in, l, norm2_g[l], latent_only=last)
        kind = "packed" if last else "all"
        hid = _matmul_w32(_RowPlan(geom, 1024, kind), h2, w_mlp1, l, BF16, epilogue="relu2", name="mlp_up")
        down = functools.partial(_matmul, a=hid, b=w_mlp2[l].astype(BF16), out_dtype=F32, epilogue="resid",
                                 res=xall, modtab=modtab, gate_row=5, name="mlp_down")
        if last:
            xall = down(_RowPlan(geom, 1024, kind))
        else:
            xall, h = down(_RowPlan(geom, 512), norm=(norm1_g[l + 1], modtabs[l + 1], (0, 1)))
    return _final_norm(geom, xall, final_g).reshape(batch, seq, D_MODEL)
```

```python
import functools
import math

import numpy as np
import jax
import jax.numpy as jnp
from jax import lax
from jax.experimental import pallas as pl
from jax.experimental.pallas import tpu as pltpu

F32 = jnp.float32
BF16 = jnp.bfloat16

D_MODEL = 2048
GRID_W = 64
NORM_EPS = 1e-6
N_BRANCH = 4
BRANCH_W = D_MODEL // N_BRANCH
FNO_GROUPS = 4
FNO_GROUP_W = BRANCH_W // FNO_GROUPS
RWKV_HEAD = 64
RWKV_HEADS = BRANCH_W // RWKV_HEAD
N_DIR = 2
DECAY_LORA = 64
AICL_LORA = 64
GATE_LORA = 128
DIR_LORA_W = DECAY_LORA + AICL_LORA
SHIFT_W = 3 * BRANCH_W + DIR_LORA_W
GN_EPS = 64e-5
ATT_HEAD = 64
ATT_HEADS = BRANCH_W // ATT_HEAD
ATT_KV_HEADS = 2
ATT_REP = ATT_HEADS // ATT_KV_HEADS
ATT_KV_W = ATT_KV_HEADS * ATT_HEAD
BLOCK = 128
ROPE_BASE = 10000.0
NEG_INF = -1e30
CONV_K = 31
CONV_PAD = (CONV_K - 1) // 2
LN_EPS = 1e-5

O_LORA = 3 * BRANCH_W
O_KV = O_LORA + N_DIR * DIR_LORA_W
CTX_STATE_COLS = O_KV + 2 * ATT_KV_W
O_G = CTX_STATE_COLS
O_Q = O_G + GATE_LORA
O_FNO = O_Q + BRANCH_W
O_CONV = O_FNO + BRANCH_W
O_GATE = O_CONV + 2 * BRANCH_W
IN_W = O_GATE + N_BRANCH * D_MODEL

LANES = 128
ROW_TILE = 256
SCAN_CHUNK = 64
PAIR_W = 2 * RWKV_HEAD
N_PAIRS = BRANCH_W // PAIR_W
HALO = 16
ATT_SLAB = 64
SUBLANES = 8
MXU_COLS = 256
CAST_ROWS = 256
CONV_ROWS = 32
PREP_ROWS = 64
SCAN_STEPS = 4
VMEM_LIMIT = 56 * 1024 * 1024

NT_DIMS = (((1,), (1,)), ((), ()))
NN_DIMS = (((1,), (0,)), ((), ()))


def _params(sem):
    return pltpu.CompilerParams(dimension_semantics=sem, vmem_limit_bytes=VMEM_LIMIT)


def _split2(a):
    hi = a.astype(BF16)
    lo = (a - hi.astype(F32)).astype(BF16)
    return hi, lo


def _dot3(a, b, dims=NN_DIMS):
    ah, al = _split2(a)
    bh, bl = _split2(b)
    dg = functools.partial(lax.dot_general, dimension_numbers=dims, preferred_element_type=F32)
    return dg(ah, bh) + (dg(ah, bl) + dg(al, bh))


def _dot_sel(a, sel_bf16):
    hi, lo = _split2(a)
    n = a.shape[0]
    both = jnp.dot(jnp.concatenate([hi, lo], axis=0), sel_bf16, preferred_element_type=F32)
    return both[:n] + both[n:]


def _sigmoid(x):
    return 1.0 / (1.0 + jnp.exp(-x))


def _ada_kernel(a_ref, w_ref, b_ref, o_ref):
    a = a_ref[...]
    s = a * _sigmoid(a)
    o_ref[...] = jnp.dot(s, w_ref[0], preferred_element_type=F32,
                         precision=lax.Precision.HIGHEST) + b_ref[...]


def _ada_mod(cond, w, layer, b):
    n = w.shape[2]
    tn = 1024
    return pl.pallas_call(
        _ada_kernel,
        grid=(n // tn,),
        in_specs=[pl.BlockSpec((8, D_MODEL), lambda j: (0, 0)),
                  pl.BlockSpec((1, D_MODEL, tn), lambda j: (layer, 0, j)),
                  pl.BlockSpec((1, tn), lambda j: (0, j))],
        out_specs=pl.BlockSpec((8, tn), lambda j: (0, j)),
        out_shape=jax.ShapeDtypeStruct((8, n), F32),
        compiler_params=_params(("parallel",)),
        name="ada_mod",
    )(cond, w, b.reshape(1, n))


class _Geom:
    def __init__(self, batch, ctx_len, seq):
        assert ctx_len % ROW_TILE == 0 and seq % ROW_TILE == 0
        assert seq % GRID_W == 0 and seq % BLOCK == 0 and ctx_len % BLOCK == 0
        self.batch = batch
        self.ctx_len = ctx_len
        self.seq = seq
        self.lt = ctx_len + seq
        self.m = batch * self.lt
        self.tiles_b = self.lt // ROW_TILE
        self.ctx_tiles = ctx_len // ROW_TILE
        self.tiles = batch * self.tiles_b

    def mod_row(self, i):
        return 2 * (i // self.tiles_b) + ((i % self.tiles_b) >= self.ctx_tiles).astype(jnp.int32)


def _norm_kernel(x_ref, g_ref, *rest, rows):
    x = x_ref[...]
    y = x * lax.rsqrt(jnp.mean(x * x, axis=-1, keepdims=True) + NORM_EPS) * g_ref[...]
    if rows is None:
        (o_ref,) = rest
    else:
        mod_ref, o_ref = rest
        mod = mod_ref[0]
        y = y * (1.0 + mod[rows[1]:rows[1] + 1]) + mod[rows[0]:rows[0] + 1]
    o_ref[...] = y.astype(o_ref.dtype)


def _norm_mod(geom, x, g, modtab, rows):
    return pl.pallas_call(
        functools.partial(_norm_kernel, rows=rows),
        grid=(geom.tiles,),
        in_specs=[pl.BlockSpec((ROW_TILE, D_MODEL), lambda i: (i, 0)),
                  pl.BlockSpec((1, D_MODEL), lambda i: (0, 0)),
                  pl.BlockSpec((1, 6, D_MODEL), lambda i: (geom.mod_row(i), 0, 0))],
        out_specs=pl.BlockSpec((ROW_TILE, D_MODEL), lambda i: (i, 0)),
        out_shape=jax.ShapeDtypeStruct((geom.m, D_MODEL), BF16),
        compiler_params=_params(("parallel",)),
        name="norm_mod",
    )(x, g.reshape(1, D_MODEL), modtab)


def _final_norm(geom, x, g):
    return pl.pallas_call(
        functools.partial(_norm_kernel, rows=None),
        grid=(geom.batch * geom.seq // ROW_TILE,),
        in_specs=[pl.BlockSpec((ROW_TILE, D_MODEL), lambda i: (i, 0)),
                  pl.BlockSpec((1, D_MODEL), lambda i: (0, 0))],
        out_specs=pl.BlockSpec((ROW_TILE, D_MODEL), lambda i: (i, 0)),
        out_shape=jax.ShapeDtypeStruct((geom.batch * geom.seq, D_MODEL), F32),
        compiler_params=_params(("parallel",)),
        name="final_norm",
    )(x, g.reshape(1, D_MODEL))


def _mm_store(acc, res_ref, mod_refs, o_ref, epilogue, gate_row, norm=None, cs=slice(None)):
    if epilogue == "resid":
        for s, mod_ref in enumerate(mod_refs):
            rs = slice(s * ROW_TILE, (s + 1) * ROW_TILE)
            xn = res_ref[rs, cs] + mod_ref[0, gate_row:gate_row + 1, cs] * acc[rs]
            o_ref[rs, cs] = xn
            if norm is not None:
                g_ref, nmod_refs, h_ref, rows = norm
                nmod = nmod_refs[s][0]
                y = xn * lax.rsqrt(jnp.mean(xn * xn, axis=-1, keepdims=True) + NORM_EPS) * g_ref[...]
                h_ref[rs] = (y * (1.0 + nmod[rows[1]:rows[1] + 1]) + nmod[rows[0]:rows[0] + 1]).astype(h_ref.dtype)
        return
    if epilogue == "sigmoid":
        acc = _sigmoid(acc)
    elif epilogue == "relu2":
        acc = jnp.square(jnp.maximum(acc, 0.0))
    o_ref[:, cs] = acc.astype(o_ref.dtype)


def _col_chunks(tn, norm):
    if norm is not None or tn % MXU_COLS:
        return [slice(None)]
    return [slice(c, c + MXU_COLS) for c in range(0, tn, MXU_COLS)]


def _mm_kernel(a_ref, b_ref, *rest, epilogue, gate_row, nk, n_sub, norm_rows):
    rest = list(rest)
    res_ref, mod_refs, norm = None, (), None
    if epilogue == "resid":
        res_ref, mod_refs, rest = rest[0], rest[1:1 + n_sub], rest[1 + n_sub:]
        if norm_rows is not None:
            norm = (rest[0], rest[1:1 + n_sub], rest[2 + n_sub], norm_rows)
            rest = [rest[1 + n_sub]] + rest[3 + n_sub:]
    o_ref = rest[0]
    chunks = _col_chunks(o_ref.shape[1], norm)
    if nk == 1:
        for cs in chunks:
            acc = jnp.dot(a_ref[...], b_ref[:, cs], preferred_element_type=F32)
            _mm_store(acc, res_ref, mod_refs, o_ref, epilogue, gate_row, norm, cs)
        return
    acc_ref = rest[1]
    k = pl.program_id(2)

    @pl.when(k == 0)
    def _():
        acc_ref[...] = jnp.dot(a_ref[...], b_ref[...], preferred_element_type=F32)

    @pl.when((k > 0) & (k < nk - 1))
    def _():
        acc_ref[...] += jnp.dot(a_ref[...], b_ref[...], preferred_element_type=F32)

    @pl.when(k == nk - 1)
    def _():
        for cs in chunks:
            acc = acc_ref[:, cs] + jnp.dot(a_ref[...], b_ref[:, cs], preferred_element_type=F32)
            _mm_store(acc, res_ref, mod_refs, o_ref, epilogue, gate_row, norm, cs)


def _mm_w32_kernel(a_ref, w_ref, o_ref, wb_ref, *, epilogue):
    @pl.when(pl.program_id(1) == 0)
    def _():
        def cast_rows(r, carry):
            rows = pl.ds(pl.multiple_of(r * CAST_ROWS, CAST_ROWS), CAST_ROWS)
            wb_ref[rows, :] = w_ref[0, rows, :].astype(BF16)
            return carry

        lax.fori_loop(0, wb_ref.shape[0] // CAST_ROWS, cast_rows, 0)

    for cs in _col_chunks(o_ref.shape[1], None):
        acc = jnp.dot(a_ref[...], wb_ref[:, cs], preferred_element_type=F32)
        _mm_store(acc, None, (), o_ref, epilogue, 0, cs=cs)


def _pick_tile(n, cap, unit=LANES):
    t = (min(cap, n) // unit) * unit
    while n % t:
        t -= unit
    return t


class _RowPlan:
    def __init__(self, geom, tm_cap, kind="all", rows=None):
        if kind in ("latent", "packed"):
            self.tm = _pick_tile(geom.seq, tm_cap, ROW_TILE)
            per_b = geom.seq // self.tm
            self.n = geom.batch * per_b
            self.rows = geom.m if kind == "latent" else geom.batch * geom.seq
            if kind == "latent":
                self.off = lambda i: (i // per_b) * geom.lt + geom.ctx_len + (i % per_b) * self.tm
                self.mod_row = geom.mod_row
            else:
                self.off = lambda i: i * self.tm
                self.mod_row = lambda t: 2 * (t // (geom.seq // ROW_TILE)) + 1
        else:
            self.rows = geom.m if kind == "all" else rows
            self.tm = _pick_tile(self.rows, tm_cap, ROW_TILE)
            self.n = self.rows // self.tm
            self.off = lambda i: i * self.tm
            self.mod_row = geom.mod_row if kind == "all" else None
        self.n_sub = self.tm // ROW_TILE

    def spec(self, width, at):
        def index(*g):
            i, col = at(*g)
            col = col if isinstance(col, int) else pl.multiple_of(col, LANES)
            return pl.multiple_of(self.off(i), ROW_TILE), col
        return pl.BlockSpec((pl.Element(self.tm), pl.Element(width)), index)

    def mod_specs(self, tn, at):
        return [pl.BlockSpec((1, 6, tn), lambda *g, s=s: (
            self.mod_row(self.off(at(*g)[0]) // ROW_TILE + s), 0, at(*g)[1])) for s in range(self.n_sub)]


def _matmul(plan, a, b, out_dtype, epilogue="none", res=None, modtab=None, gate_row=0, norm=None,
            res_plan=None, out_plan=None, tn_cap=1024, tk_cap=2048, name="matmul"):
    kdim = a.shape[1]
    n = b.shape[1]
    out_plan = out_plan or plan
    res_plan = res_plan or out_plan
    tm, m = plan.tm, out_plan.rows
    assert (out_plan.tm, out_plan.n, res_plan.tm, res_plan.n) == (tm, plan.n, tm, plan.n)
    tn = n if norm is not None else _pick_tile(n, tn_cap)
    tk = _pick_tile(kdim, tk_cap)
    nk = kdim // tk
    row_j = lambda i, j, k: (i, j * tn)
    tile_ij = lambda i, j, k: (i, j)
    in_specs = [plan.spec(tk, lambda i, j, k: (i, k * tk)),
                pl.BlockSpec((tk, tn), lambda i, j, k: (k, j))]
    args = [a, b]
    out_specs = out_plan.spec(tn, row_j)
    out_shape = jax.ShapeDtypeStruct((m, n), out_dtype)
    if epilogue == "resid":
        in_specs += [res_plan.spec(tn, row_j)] + out_plan.mod_specs(tn, tile_ij)
        args += [res] + [modtab] * plan.n_sub
        if norm is not None:
            gain, nmodtab, _ = norm
            in_specs += [pl.BlockSpec((1, tn), lambda i, j, k: (0, 0))] + out_plan.mod_specs(tn, tile_ij)
            args += [gain.reshape(1, n)] + [nmodtab] * plan.n_sub
            out_specs = [out_specs, out_plan.spec(tn, row_j)]
            out_shape = [out_shape, jax.ShapeDtypeStruct((m, n), BF16)]
    return pl.pallas_call(
        functools.partial(_mm_kernel, epilogue=epilogue, gate_row=gate_row, nk=nk, n_sub=plan.n_sub,
                          norm_rows=None if norm is None else norm[2]),
        grid=(plan.n, n // tn, nk),
        in_specs=in_specs,
        out_specs=out_specs,
        out_shape=out_shape,
        scratch_shapes=[pltpu.VMEM((tm, tn), F32)] if nk > 1 else [],
        compiler_params=_params(("parallel", "parallel", "arbitrary")),
        name=name,
    )(*args)


def _matmul_w32(plan, a, w, layer, out_dtype, epilogue="none", cols=None, out_plan=None, tn_cap=1024,
                name="matmul_w32"):
    kdim = a.shape[1]
    col0, n = cols if cols is not None else (0, w.shape[2])
    assert col0 % LANES == 0 and epilogue != "resid"
    out_plan = out_plan or plan
    assert (out_plan.tm, out_plan.n) == (plan.tm, plan.n)
    tn = _pick_tile(n, tn_cap)
    in_specs = [plan.spec(kdim, lambda j, i: (i, 0)),
                pl.BlockSpec((pl.Element(1), pl.Element(kdim), pl.Element(tn)),
                             lambda j, i: (layer, 0, pl.multiple_of(col0 + j * tn, LANES)))]
    args = [a, w]
    return pl.pallas_call(
        functools.partial(_mm_w32_kernel, epilogue=epilogue),
        grid=(n // tn, plan.n),
        in_specs=in_specs,
        out_specs=out_plan.spec(tn, lambda j, i: (i, j * tn)),
        out_shape=jax.ShapeDtypeStruct((out_plan.rows, n), out_dtype),
        scratch_shapes=[pltpu.VMEM((kdim, tn), BF16)],
        compiler_params=_params(("parallel", "arbitrary")),
        name=name,
    )(*args)


def _head_sum_matrix(width, head):
    idx = np.arange(width) // head
    return jnp.asarray((idx[:, None] == idx[None, :]).astype(np.float32), dtype=BF16)


def _prep_kernel(p_ref, lora_ref, halo_ref, hlora_ref, mu_ref, w0_ref, wup_ref, a0_ref, aup_ref,
                 kk_ref, ka_ref, hs_ref, r_o, lw_o, k_o, v_o, kkn_o, b_o, *, tiles_b, ctx_tiles):
    d = pl.program_id(0)
    j = pl.program_id(1) % tiles_b
    fwd = d == 0
    t = p_ref.shape[0]
    rkv_w = 3 * BRANCH_W
    at_start = (j == 0) | (j == ctx_tiles)
    at_end = (j == ctx_tiles - 1) | (j == tiles_b - 1)
    no_edge = (fwd & at_start) | (jnp.logical_not(fwd) & at_end)
    row = lax.broadcasted_iota(jnp.int32, (t, 1), 0)
    edge_row = jnp.where(fwd, 0, t - 1)

    def neighbour(x_ref, h_ref):
        edge = jnp.where(no_edge, 0.0, jnp.where(fwd, h_ref[7:8], h_ref[0:1]))
        x = x_ref[...]
        return jnp.where(row == edge_row, edge, jnp.where(fwd, pltpu.roll(x, 1, 0), pltpu.roll(x, t - 1, 0)))

    nb_rkv = neighbour(p_ref, halo_ref)
    nb_lora = neighbour(lora_ref, hlora_ref)
    mu = mu_ref[0]
    for c in range(t // PREP_ROWS):
        rs = slice(c * PREP_ROWS, (c + 1) * PREP_ROWS)
        f = p_ref[rs]
        f = f + mu[:, :rkv_w] * (nb_rkv[rs] - f)
        lo = lora_ref[rs]
        lo = lo + mu[:, rkv_w:] * (nb_lora[rs] - lo)
        r = f[:, 0:BRANCH_W]
        k = f[:, BRANCH_W:2 * BRANCH_W]
        v = f[:, 2 * BRANCH_W:rkv_w]
        w_raw = w0_ref[0] + _dot3(jnp.tanh(lo[:, :DECAY_LORA]), wup_ref[0])
        lw = -math.exp(-0.5) * _sigmoid(w_raw)
        a = _sigmoid(a0_ref[0] + _dot3(lo[:, DECAY_LORA:], aup_ref[0]))
        kk = k * kk_ref[...]
        norm = jnp.sqrt(_dot_sel(kk * kk, hs_ref[...]))
        kk = kk / jnp.maximum(norm, 1e-12)
        r_o[0, rs] = r.astype(r_o.dtype)
        lw_o[0, rs] = lw
        k_o[0, rs] = (k * (1.0 + (a - 1.0) * ka_ref[...])).astype(k_o.dtype)
        v_o[0, rs] = v.astype(v_o.dtype)
        kkn_o[0, rs] = kk.astype(kkn_o.dtype)
        b_o[0, rs] = (kk * a).astype(b_o.dtype)


def _rwkv_prep(geom, p1, lp):
    m = geom.m
    t = ROW_TILE
    tb = geom.tiles_b
    rkv_w = 3 * BRANCH_W
    lora_blk0 = O_LORA // DIR_LORA_W
    n_blk8 = m // 8

    def halo_idx(d, i):
        before = jnp.maximum(i * (t // 8) - 1, 0)
        after = jnp.minimum((i + 1) * (t // 8), n_blk8 - 1)
        return jnp.where(d == 0, before, after)

    out = jax.ShapeDtypeStruct((N_DIR, m, BRANCH_W), F32)
    ospec = pl.BlockSpec((1, t, BRANCH_W), lambda d, i: (d, i, 0))
    vec = lambda a: a.reshape(1, BRANCH_W)
    dvec = pl.BlockSpec((1, 1, BRANCH_W), lambda d, i: (d, 0, 0))
    return pl.pallas_call(
        functools.partial(_prep_kernel, tiles_b=tb, ctx_tiles=geom.ctx_tiles),
        grid=(N_DIR, geom.tiles),
        in_specs=[pl.BlockSpec((t, rkv_w), lambda d, i: (i, 0)),
                  pl.BlockSpec((t, DIR_LORA_W), lambda d, i: (i, lora_blk0 + d)),
                  pl.BlockSpec((8, rkv_w), lambda d, i: (halo_idx(d, i), 0)),
                  pl.BlockSpec((8, DIR_LORA_W), lambda d, i: (halo_idx(d, i), lora_blk0 + d)),
                  pl.BlockSpec((1, 1, SHIFT_W), lambda d, i: (d, 0, 0)),
                  dvec,
                  pl.BlockSpec((1, DECAY_LORA, BRANCH_W), lambda d, i: (d, 0, 0)),
                  dvec,
                  pl.BlockSpec((1, AICL_LORA, BRANCH_W), lambda d, i: (d, 0, 0)),
                  pl.BlockSpec((1, BRANCH_W), lambda d, i: (0, 0)),
                  pl.BlockSpec((1, BRANCH_W), lambda d, i: (0, 0)),
                  pl.BlockSpec((BRANCH_W, BRANCH_W), lambda d, i: (0, 0))],
        out_specs=[ospec] * 6,
        out_shape=[jax.ShapeDtypeStruct(out.shape, F32 if i == 1 else BF16) for i in range(6)],
        compiler_params=_params(("parallel", "parallel")),
        name="rwkv_prep",
    )(p1, p1, p1, p1, lp["rwkv_mu"].reshape(N_DIR, 1, SHIFT_W),
      lp["rwkv_w0"].reshape(N_DIR, 1, BRANCH_W), lp["rwkv_w_up"],
      lp["rwkv_a0"].reshape(N_DIR, 1, BRANCH_W), lp["rwkv_a_up"],
      vec(lp["rwkv_k_k"]), vec(lp["rwkv_k_a"]), _head_sum_matrix(BRANCH_W, RWKV_HEAD))


def _scan_kernel(*refs):
    yf_ref, yb_ref, h_ref = refs[12:]
    c = pl.program_id(1)
    C = SCAN_CHUNK
    W = 2 * C

    @pl.when(c == 0)
    def _():
        h_ref[...] = jnp.zeros_like(h_ref)

    rr = lax.broadcasted_iota(jnp.int32, (C, C), 0)
    cc = lax.broadcasted_iota(jnp.int32, (C, C), 1)
    lane = lax.broadcasted_iota(jnp.int32, (1, PAIR_W), 1)
    m_a = (lane < RWKV_HEAD).astype(F32)
    m_b = 1.0 - m_a
    r2 = lax.broadcasted_iota(jnp.int32, (W, W), 0)
    c2 = lax.broadcasted_iota(jnp.int32, (W, W), 1)
    same = (r2 // C) == (c2 // C)
    eye = (r2 == c2).astype(F32)

    def pairs(x):
        return [jnp.concatenate([x[:, p * PAIR_W:(p + 1) * PAIR_W] * m_a,
                                 x[:, p * PAIR_W:(p + 1) * PAIR_W] * m_b], axis=0) for p in range(N_PAIRS)]

    strict, incl, tri = [], [], []
    for sgn in (1, -1):
        dt = (r2 % C - c2 % C) * sgn
        strict += [(same & (dt > 0)).astype(F32)] * N_PAIRS
        incl += [(same & (dt >= 0)).astype(F32)] * N_PAIRS
        tri.append(jnp.where((rr - cc) * sgn >= 0, 1.0, 0.0).astype(BF16))
    strict = jnp.stack(strict)
    incl = jnp.stack(incl)

    h = h_ref[...]
    for step in range(SCAN_STEPS):
        rows = (pl.ds(step * C, C), pl.ds((SCAN_STEPS - 1 - step) * C, C))
        h = _scan_chunk(refs[:12], rows, tri, strict, incl, eye, pairs, h, (yf_ref, yb_ref))
    h_ref[...] = h


def _scan_chunk(in_refs, rows, tri, strict, incl, eye, pairs, h0, y_refs):
    C = SCAN_CHUNK
    W = 2 * C
    stacks = {name: [] for name in ("a", "b", "k", "r", "v", "bc", "kc", "pt")}
    for d in range(N_DIR):
        r_ref, lw_ref, k_ref, v_ref, kk_ref, b_ref = in_refs[d::N_DIR]
        rs = rows[d]
        lw = lw_ref[0, rs]
        lp_in = _dot_sel_lhs(tri[d], lw)
        tot = jnp.sum(lw, axis=0, keepdims=True)
        e_neg = jnp.exp(-lp_in)
        e_chk = jnp.exp(tot - lp_in)
        p_tot = jnp.exp(tot)
        stacks["a"] += pairs(-kk_ref[0, rs] * jnp.exp(lp_in - lw))
        stacks["b"] += pairs(b_ref[0, rs] * e_neg)
        stacks["k"] += pairs(k_ref[0, rs] * e_neg)
        stacks["r"] += pairs(r_ref[0, rs] * jnp.exp(lp_in))
        stacks["v"] += pairs(v_ref[0, rs])
        stacks["bc"] += pairs(b_ref[0, rs] * e_chk)
        stacks["kc"] += pairs(k_ref[0, rs] * e_chk)
        stacks["pt"] += [p_tot[:, p * PAIR_W:(p + 1) * PAIR_W] for p in range(N_PAIRS)]
    a_s, b_s, k_s, r_s, v_s, bc_s, kc_s, p_tot = (jnp.stack(stacks[n]) for n in
                                                  ("a", "b", "k", "r", "v", "bc", "kc", "pt"))

    a_b, b_b, k_b, r_b, v_b, bc_b, kc_b = (x.astype(BF16) for x in (a_s, b_s, k_s, r_s, v_s, bc_s, kc_s))
    big = _bdot(jnp.concatenate([a_b, r_b], axis=1), jnp.concatenate([b_b, k_b], axis=1), BNT_DIMS)
    l_ab = big[:, :W, :W] * strict
    l_ak = big[:, :W, W:] * strict
    m_rb = (big[:, W:, :W] * incl).astype(BF16)
    m_rk = big[:, W:, W:] * incl
    t_inv = eye + l_ab
    pw_b = l_ab.astype(BF16)
    pw_b = _bdot(pw_b, pw_b).astype(BF16)
    for _ in range(int(math.log2(C)) - 2):
        both = _bdot(jnp.concatenate([t_inv.astype(BF16), pw_b], axis=1), pw_b)
        t_inv = t_inv + both[:, :W]
        pw_b = both[:, W:].astype(BF16)
    t_inv = t_inv + _bdot(t_inv, pw_b)
    x1 = _bdot(t_inv, jnp.concatenate([a_b, _bdot(l_ak, v_b).astype(BF16)], axis=2)).astype(BF16)
    x2 = _bdot(m_rb, x1)
    r_hat = r_s + x2[:, :, :PAIR_W]
    y0 = x2[:, :, PAIR_W:] + _bdot(m_rk, v_b)
    x3 = _bdot(bc_b, x1, BTN_DIMS)
    g = eye * p_tot + x3[:, :, :PAIR_W]
    h_inc = x3[:, :, PAIR_W:] + _bdot(kc_b, v_b, BTN_DIMS)
    x4 = _bdot(jnp.concatenate([r_hat, g], axis=1), h0)
    ys = x4[:, :W] + y0
    for d, y_ref in enumerate(y_refs):
        for p in range(N_PAIRS):
            y_ref[rows[d], p * PAIR_W:(p + 1) * PAIR_W] = ys[d * N_PAIRS + p, :C] + ys[d * N_PAIRS + p, C:]
    return x4[:, W:] + h_inc


BNN_DIMS = (((2,), (1,)), ((0,), (0,)))
BNT_DIMS = (((2,), (2,)), ((0,), (0,)))
BTN_DIMS = (((1,), (1,)), ((0,), (0,)))


def _bdot(a, b, dims=BNN_DIMS):
    return lax.dot_general(a.astype(BF16), b.astype(BF16), dims, preferred_element_type=F32)


def _dot_sel_lhs(sel_bf16, a):
    hi = a.astype(BF16)
    r1 = a - hi.astype(F32)
    mid = r1.astype(BF16)
    lo = (r1 - mid.astype(F32)).astype(BF16)
    dg = functools.partial(jnp.dot, preferred_element_type=F32)
    return dg(sel_bf16, hi) + (dg(sel_bf16, mid) + dg(sel_bf16, lo))


def _rwkv_scan(geom, ins):
    C = SCAN_CHUNK * SCAN_STEPS
    assert geom.ctx_len % C == 0 and geom.seq % C == 0
    nch = geom.lt // C
    nctx = geom.ctx_len // C

    def rev(c):
        return jnp.where(c < nctx, nctx - 1 - c, nch - 1 + nctx - c)

    fwd = pl.BlockSpec((1, C, BRANCH_W), lambda b, c: (0, b * nch + c, 0))
    bwd = pl.BlockSpec((1, C, BRANCH_W), lambda b, c: (1, b * nch + rev(c), 0))
    out = jax.ShapeDtypeStruct((geom.m, BRANCH_W), F32)
    return pl.pallas_call(
        _scan_kernel,
        grid=(geom.batch, nch),
        in_specs=[fwd, bwd] * 6,
        out_specs=[pl.BlockSpec((C, BRANCH_W), lambda b, c: (b * nch + c, 0)),
                   pl.BlockSpec((C, BRANCH_W), lambda b, c: (b * nch + rev(c), 0))],
        out_shape=[out, out],
        scratch_shapes=[pltpu.VMEM((N_DIR * N_PAIRS, PAIR_W, PAIR_W), F32)],
        compiler_params=_params(("parallel", "arbitrary")),
        name="rwkv_scan",
    )(*[a for a in ins for _ in range(N_DIR)])


def _readout_kernel(yf_ref, yb_ref, r_ref, k_ref, v_ref, p2_ref, gup_ref, rk_ref, lg_ref, lb_ref, hs_ref, o_ref):
    hs = hs_ref[...]
    y = yf_ref[...] + yb_ref[...]
    inv_n = 1.0 / RWKV_HEAD
    mean = _dot_sel(y, hs) * inv_n
    yc = y - mean
    var = _dot_sel(yc * yc, hs) * inv_n
    yn = yc * lax.rsqrt(var + GN_EPS) * lg_ref[...] + lb_ref[...]
    bonus = jnp.zeros_like(y)
    for d in range(N_DIR):
        rk = r_ref[d].astype(F32) * k_ref[d].astype(F32) * rk_ref[d:d + 1]
        bonus = bonus + _dot_sel(rk, hs) * v_ref[d].astype(F32)
    g = _dot3(_sigmoid(p2_ref[:, 0:GATE_LORA]), gup_ref[...])
    o_ref[...] = ((yn + bonus) * g).astype(o_ref.dtype)


def _rwkv_readout(geom, y, ins, p2, lp):
    t = ROW_TILE
    dspec = pl.BlockSpec((N_DIR, t, BRANCH_W), lambda i: (0, i, 0))
    vspec = pl.BlockSpec((1, BRANCH_W), lambda i: (0, 0))
    r, _, k, v, _, _ = ins
    return pl.pallas_call(
        _readout_kernel,
        grid=(geom.tiles,),
        in_specs=[pl.BlockSpec((t, BRANCH_W), lambda i: (i, 0)),
                  pl.BlockSpec((t, BRANCH_W), lambda i: (i, 0)),
                  dspec, dspec, dspec,
                  pl.BlockSpec((t, GATE_LORA + BRANCH_W), lambda i: (i, 0)),
                  pl.BlockSpec((GATE_LORA, BRANCH_W), lambda i: (0, 0)),
                  pl.BlockSpec((N_DIR, BRANCH_W), lambda i: (0, 0)),
                  vspec, vspec,
                  pl.BlockSpec((BRANCH_W, BRANCH_W), lambda i: (0, 0))],
        out_specs=pl.BlockSpec((t, BRANCH_W), lambda i: (i, 0)),
        out_shape=jax.ShapeDtypeStruct((geom.m, BRANCH_W), BF16),
        compiler_params=_params(("parallel",)),
        name="rwkv_readout",
    )(y[0], y[1], r, k, v, p2, lp["rwkv_g_up"], lp["rwkv_r_k"].reshape(N_DIR, BRANCH_W),
      lp["rwkv_lnx_g"].reshape(1, BRANCH_W), lp["rwkv_lnx_b"].reshape(1, BRANCH_W),
      _head_sum_matrix(BRANCH_W, RWKV_HEAD))


def _rope_tables(geom):
    half = ATT_HEAD // 2
    nf = half // 2
    inv = ROPE_BASE ** (-jnp.arange(nf, dtype=F32) / nf)
    pos = jnp.arange(geom.seq, dtype=jnp.int32)
    row_ang = (pos // GRID_W).astype(F32)[:, None] * inv[None, :]
    col_ang = (pos % GRID_W).astype(F32)[:, None] * inv[None, :]
    cos = jnp.concatenate([jnp.cos(row_ang)] * 2 + [jnp.cos(col_ang)] * 2, axis=-1)
    sin = jnp.concatenate([-jnp.sin(row_ang), jnp.sin(row_ang), -jnp.sin(col_ang), jnp.sin(col_ang)], axis=-1)
    cos = jnp.concatenate([jnp.ones((geom.ctx_len, ATT_HEAD), F32), cos], axis=0)
    sin = jnp.concatenate([jnp.zeros((geom.ctx_len, ATT_HEAD), F32), sin], axis=0)
    return cos, sin


def _rotate(t, cos, sin):
    w = t.shape[-1]
    nf = ATT_HEAD // 4
    lane = lax.broadcasted_iota(jnp.int32, (1, w), 1)
    partner = jnp.where((lane % (2 * nf)) < nf, pltpu.roll(t, w - nf, 1), pltpu.roll(t, nf, 1))
    return t * cos + partner * sin


def _rope_kernel(p2_ref, kv_ref, cq_ref, sq_ref, ck_ref, sk_ref, q_o, k_o, v_o):
    q = p2_ref[:, GATE_LORA:]
    q_o[...] = (_rotate(q, cq_ref[...], sq_ref[...]) * (ATT_HEAD ** -0.5)).astype(q_o.dtype)
    kv = kv_ref[...]
    k_o[...] = _rotate(kv[:, :ATT_KV_W], ck_ref[...], sk_ref[...]).astype(k_o.dtype)
    v_o[...] = kv[:, ATT_KV_W:].astype(v_o.dtype)


def _rope(geom, p1, p2):
    t = ROW_TILE
    tb = geom.tiles_b
    cos, sin = _rope_tables(geom)
    cq, sq = jnp.tile(cos, (1, ATT_HEADS)), jnp.tile(sin, (1, ATT_HEADS))
    ck, sk = jnp.tile(cos, (1, ATT_KV_HEADS)), jnp.tile(sin, (1, ATT_KV_HEADS))
    qspec = pl.BlockSpec((t, BRANCH_W), lambda i: (i % tb, 0))
    kspec = pl.BlockSpec((t, ATT_KV_W), lambda i: (i % tb, 0))
    m = geom.m
    return pl.pallas_call(
        _rope_kernel,
        grid=(geom.tiles,),
        in_specs=[pl.BlockSpec((t, GATE_LORA + BRANCH_W), lambda i: (i, 0)),
                  pl.BlockSpec((t, 2 * ATT_KV_W), lambda i: (i, O_KV // (2 * ATT_KV_W))),
                  qspec, qspec, kspec, kspec],
        out_specs=[pl.BlockSpec((t, BRANCH_W), lambda i: (i, 0)),
                   pl.BlockSpec((t, ATT_KV_W), lambda i: (i, 0)),
                   pl.BlockSpec((t, ATT_KV_W), lambda i: (i, 0))],
        out_shape=[jax.ShapeDtypeStruct((m, BRANCH_W), BF16),
                   jax.ShapeDtypeStruct((m, ATT_KV_W), BF16),
                   jax.ShapeDtypeStruct((m, ATT_KV_W), BF16)],
        compiler_params=_params(("parallel",)),
        name="rope",
    )(p2, p1, cq, sq, ck, sk)


def _attn_kernel(q_ref, kc_ref, vc_ref, kp_ref, ko_ref, kn_ref, vp_ref, vo_ref, vn_ref, sink_ref, o_ref,
                 s_ref, p_ref, *, ctx_blocks, blocks_b):
    j = pl.program_id(1)
    q = q_ref[...]
    k_all = jnp.concatenate([kp_ref[...], ko_ref[...], kn_ref[...], kc_ref[...]], axis=0)
    v_all = jnp.concatenate([vp_ref[...], vo_ref[...], vn_ref[...], vc_ref[...]], axis=0)
    nloc = 3 * BLOCK
    slab = ATT_SLAB
    qi0 = lax.broadcasted_iota(jnp.int32, (slab, nloc), 0)
    ki = lax.broadcasted_iota(jnp.int32, (slab, nloc), 1)
    never = 4 * BLOCK
    prev_off = jnp.where(j > ctx_blocks, 0, never)
    own_hi = jnp.where(j >= ctx_blocks, 2 * BLOCK, BLOCK)
    next_off = 2 * BLOCK - jnp.where((j >= ctx_blocks) & (j < blocks_b - 1), 0, never)
    outs = []
    for g in range(ATT_KV_HEADS):
        gs = slice(g * ATT_HEAD, (g + 1) * ATT_HEAD)
        qg = jnp.concatenate([q[:, (g * ATT_REP + h) * ATT_HEAD:(g * ATT_REP + h + 1) * ATT_HEAD]
                              for h in range(ATT_REP)], axis=0)
        s_ref[g] = lax.dot_general(qg, k_all[:, gs], NT_DIMS, preferred_element_type=F32)
        dens = []
        for blk in range(ATT_REP * BLOCK // slab):
            rs = slice(blk * slab, (blk + 1) * slab)
            head = g * ATT_REP + blk * slab // BLOCK
            qi = qi0 + (blk * slab) % BLOCK
            valid = (((ki < BLOCK) & (ki >= qi + prev_off)) | ((ki >= BLOCK) & (ki < own_hi))
                     | ((ki >= 2 * BLOCK) & (ki <= qi + next_off)))
            s_loc = jnp.where(valid, s_ref[g, rs, :nloc], NEG_INF)
            s_ctx = s_ref[g, rs, nloc:]
            sink = sink_ref[head:head + 1, 0:1]
            mx = jnp.maximum(jnp.maximum(jnp.max(s_loc, axis=-1, keepdims=True),
                                         jnp.max(s_ctx, axis=-1, keepdims=True)), sink)
            e_loc = jnp.exp(s_loc - mx)
            e_ctx = jnp.exp(s_ctx - mx)
            dens.append(jnp.sum(e_loc, axis=-1, keepdims=True) + jnp.sum(e_ctx, axis=-1, keepdims=True)
                        + jnp.exp(sink - mx))
            p_ref[g, rs, :nloc] = e_loc.astype(BF16)
            p_ref[g, rs, nloc:] = e_ctx.astype(BF16)
        o = jnp.dot(p_ref[g], v_all[:, gs], preferred_element_type=F32) / jnp.concatenate(dens, axis=0)
        outs += [o[h * BLOCK:(h + 1) * BLOCK] for h in range(ATT_REP)]
    o_ref[...] = jnp.concatenate(outs, axis=-1).astype(o_ref.dtype)


def _attention(geom, q, k, v, sink):
    nb = geom.lt // BLOCK
    cb = geom.ctx_len // BLOCK
    row = lambda b, j: (b * nb + j, 0)
    prev = lambda b, j: (b * nb + jnp.maximum(j - 1, 0), 0)
    nxt = lambda b, j: (b * nb + jnp.minimum(j + 1, nb - 1), 0)
    ctx = lambda b, j: (b * (geom.lt // geom.ctx_len), 0)
    assert geom.lt % geom.ctx_len == 0
    kvs = lambda f: pl.BlockSpec((BLOCK, ATT_KV_W), f)
    cspec = pl.BlockSpec((geom.ctx_len, ATT_KV_W), ctx)
    sink_tab = jnp.broadcast_to(sink.astype(F32)[:, None], (ATT_HEADS, LANES))
    return pl.pallas_call(
        functools.partial(_attn_kernel, ctx_blocks=cb, blocks_b=nb),
        grid=(geom.batch, nb),
        in_specs=[pl.BlockSpec((BLOCK, BRANCH_W), row), cspec, cspec,
                  kvs(prev), kvs(row), kvs(nxt), kvs(prev), kvs(row), kvs(nxt),
                  pl.BlockSpec((ATT_HEADS, LANES), lambda b, j: (0, 0))],
        out_specs=pl.BlockSpec((BLOCK, BRANCH_W), row),
        out_shape=jax.ShapeDtypeStruct((geom.m, BRANCH_W), BF16),
        scratch_shapes=[pltpu.VMEM((ATT_KV_HEADS, ATT_REP * BLOCK, 3 * BLOCK + geom.ctx_len), F32),
                        pltpu.VMEM((ATT_KV_HEADS, ATT_REP * BLOCK, 3 * BLOCK + geom.ctx_len), BF16)],
        compiler_params=_params(("parallel", "parallel")),
        name="attention",
    )(q, k, v, k, k, k, v, v, v, sink_tab)


def _conv_kernel(u_ref, up_ref, un_ref, dw_ref, db_ref, lg_ref, lb_ref, o_ref, hp_ref, sh_ref,
                 *, tiles_b, ctx_tiles):
    j = pl.program_id(0) % tiles_b
    t = u_ref.shape[0]

    def glu(u):
        return u[:, :BRANCH_W] * _sigmoid(u[:, BRANCH_W:])

    at_start = (j == 0) | (j == ctx_tiles)
    at_end = (j == ctx_tiles - 1) | (j == tiles_b - 1)
    hp_ref[0:HALO] = jnp.where(at_start, 0.0, glu(up_ref[...]))
    hp_ref[HALO:HALO + t] = glu(u_ref[...])
    hp_ref[HALO + t:] = jnp.where(at_end, 0.0, glu(un_ref[...]))
    span = t + 2 * HALO - SUBLANES
    for r in range(1, SUBLANES):
        sh_ref[r, 0:span] = hp_ref[pl.ds(r, span)]
    for chunk in range(t // CONV_ROWS):
        acc = jnp.zeros((CONV_ROWS, BRANCH_W), F32) + db_ref[...]
        for tap in range(CONV_K):
            q, r = divmod(HALO - CONV_PAD + tap, SUBLANES)
            rows = pl.ds(chunk * CONV_ROWS + q * SUBLANES, CONV_ROWS)
            src = hp_ref[rows] if r == 0 else sh_ref[r, rows]
            acc = acc + src * dw_ref[tap:tap + 1]
        mean = jnp.mean(acc, axis=-1, keepdims=True)
        cen = acc - mean
        var = jnp.mean(cen * cen, axis=-1, keepdims=True)
        h = cen * lax.rsqrt(var + LN_EPS) * lg_ref[...] + lb_ref[...]
        o_ref[chunk * CONV_ROWS:(chunk + 1) * CONV_ROWS] = (h * _sigmoid(h)).astype(o_ref.dtype)


def _conv(geom, p4, lp):
    t = ROW_TILE
    nh = geom.m // HALO
    vspec = pl.BlockSpec((1, BRANCH_W), lambda i: (0, 0))
    return pl.pallas_call(
        functools.partial(_conv_kernel, tiles_b=geom.tiles_b, ctx_tiles=geom.ctx_tiles),
        grid=(geom.tiles,),
        in_specs=[pl.BlockSpec((t, 2 * BRANCH_W), lambda i: (i, 0)),
                  pl.BlockSpec((HALO, 2 * BRANCH_W), lambda i: (jnp.maximum(i * (t // HALO) - 1, 0), 0)),
                  pl.BlockSpec((HALO, 2 * BRANCH_W), lambda i: (jnp.minimum((i + 1) * (t // HALO), nh - 1), 0)),
                  pl.BlockSpec((CONV_K, BRANCH_W), lambda i: (0, 0)),
                  vspec, vspec, vspec],
        out_specs=pl.BlockSpec((t, BRANCH_W), lambda i: (i, 0)),
        out_shape=jax.ShapeDtypeStruct((geom.m, BRANCH_W), BF16),
        scratch_shapes=[pltpu.VMEM((t + 2 * HALO, BRANCH_W), F32),
                        pltpu.VMEM((SUBLANES, t + 2 * HALO, BRANCH_W), F32)],
        compiler_params=_params(("parallel",)),
        name="conformer_conv",
    )(p4, p4, p4, lp["conv_dw"], lp["conv_dw_b"].reshape(1, BRANCH_W),
      lp["conv_ln_g"].reshape(1, BRANCH_W), lp["conv_ln_b"].reshape(1, BRANCH_W))


def _dft_cos_sin(n, scale):
    idx = np.arange(n, dtype=np.int64)
    ang = 2.0 * np.pi * ((idx[:, None] * idx[None, :]) % n).astype(np.float64) / n
    return np.cos(ang) * scale, np.sin(ang) * scale


def _channel_dft():
    c, s = _dft_cos_sin(FNO_GROUP_W, FNO_GROUP_W ** -0.5)
    eye = np.eye(FNO_GROUPS)
    return jnp.asarray(np.concatenate([np.kron(eye, c), np.kron(eye, s)], axis=1), dtype=F32).astype(BF16)


def _dft_pos_kernel(c_ref, s_ref, gc_ref, gs_ref, o_ref):
    o_ref[0] = (jnp.dot(c_ref[...], gc_ref[0], preferred_element_type=F32)
                + jnp.dot(s_ref[...], gs_ref[0], preferred_element_type=F32)).astype(o_ref.dtype)


def _fourier(u):
    bsz, length, _ = u.shape
    gcs = _matmul(_RowPlan(None, 1024, "plain", bsz * length), u.reshape(bsz * length, BRANCH_W), _channel_dft(),
                  BF16, name="dft_channels")
    gcs = gcs.reshape(bsz, length, 2 * BRANCH_W)
    c, s = _dft_cos_sin(length, length ** -0.5)
    tm = _pick_tile(length, 512, ROW_TILE)
    return pl.pallas_call(
        _dft_pos_kernel,
        grid=(length // tm, bsz),
        in_specs=[pl.BlockSpec((tm, length), lambda i, b: (i, 0)),
                  pl.BlockSpec((tm, length), lambda i, b: (i, 0)),
                  pl.BlockSpec((1, length, BRANCH_W), lambda i, b: (b, 0, 0)),
                  pl.BlockSpec((1, length, BRANCH_W), lambda i, b: (b, 0, 1))],
        out_specs=pl.BlockSpec((1, tm, BRANCH_W), lambda i, b: (b, i, 0)),
        out_shape=jax.ShapeDtypeStruct((bsz, length, BRANCH_W), BF16),
        compiler_params=_params(("parallel", "parallel")),
        name="dft_positions",
    )(jnp.asarray(c, dtype=F32).astype(BF16), jnp.asarray(-s, dtype=F32).astype(BF16), gcs, gcs)


def _merge_kernel(f0, f1, f2, f3, w_ref, g0, g1, g2, g3, o_ref):
    acc = None
    for i, (f, g) in enumerate(((f0, g0), (f1, g1), (f2, g2), (f3, g3))):
        term = jnp.dot(f[...], w_ref[i], preferred_element_type=F32) * g[...].astype(F32)
        acc = term if acc is None else acc + term
    o_ref[...] = acc.astype(o_ref.dtype)


def _merge(feat_plan, plan, feats, w_branch, gate):
    assert (feat_plan.tm, feat_plan.n) == (plan.tm, plan.n)
    tn = 1024
    fspec = feat_plan.spec(BRANCH_W, lambda i, j: (i, 0))
    gspec = lambda br: plan.spec(tn, lambda i, j: (i, br * D_MODEL + j * tn))
    return pl.pallas_call(
        _merge_kernel,
        grid=(plan.n, D_MODEL // tn),
        in_specs=[fspec] * 4 + [pl.BlockSpec((N_BRANCH, BRANCH_W, tn), lambda i, j: (0, 0, j))]
        + [gspec(br) for br in range(N_BRANCH)],
        out_specs=plan.spec(tn, lambda i, j: (i, j * tn)),
        out_shape=jax.ShapeDtypeStruct((plan.rows, D_MODEL), BF16),
        compiler_params=_params(("parallel", "parallel")),
        name="branch_merge",
    )(*feats, w_branch, gate, gate, gate, gate)


def _mixer(geom, h, xall, modtab, lp, w_in, layer, norm2_g, latent_only):
    src, dst = ("latent", "packed") if latent_only else ("all", "all")
    rows_all = _RowPlan(geom, 1024)
    proj = lambda lo, hi, dt, name, **kw: _matmul_w32(kw.pop("plan", rows_all), h, w_in, layer, dt,
                                                      cols=(lo, hi - lo), name=name, **kw)
    p1 = proj(0, CTX_STATE_COLS, F32, "in_proj_state")
    p2 = proj(O_G, O_FNO, F32, "in_proj_gq")
    p3 = proj(O_FNO, O_CONV, BF16, "in_proj_fno")
    p4 = proj(O_CONV, O_GATE, F32, "in_proj_conv")
    gate = proj(O_GATE, IN_W, BF16, "in_proj_gate", epilogue="sigmoid", plan=_RowPlan(geom, 1024, src),
                out_plan=_RowPlan(geom, 1024, dst))

    ins = _rwkv_prep(geom, p1, lp)
    y = _rwkv_scan(geom, ins)
    rw = _rwkv_readout(geom, y, ins, p2, lp)

    q, k, v = _rope(geom, p1, p2)
    att = _attention(geom, q, k, v, lp["att_sink"])

    cv = _conv(geom, p4, lp)

    p3 = p3.reshape(geom.batch, geom.lt, BRANCH_W)
    fno = jnp.concatenate([_fourier(p3[:, :geom.ctx_len]), _fourier(p3[:, geom.ctx_len:])], axis=1)
    fno = fno.reshape(geom.m, BRANCH_W)

    half_src, half_dst = _RowPlan(geom, 512, src), _RowPlan(geom, 512, dst)
    mixed = _merge(half_src, half_dst, (fno, rw, att, cv), lp["w_branch"].astype(BF16), gate)
    return _matmul(half_dst, mixed, lp["w_out"].astype(BF16), F32, epilogue="resid", res=xall, res_plan=half_src,
                   modtab=modtab, gate_row=2, norm=(norm2_g, modtab, (3, 4)), name="out_proj")


def kernel(x, c, ctx, c_ctx, ada_w, ada_b, norm1_g, norm2_g, w_in, rwkv_mu, rwkv_w0, rwkv_w_up, rwkv_a0, rwkv_a_up, rwkv_k_k, rwkv_k_a, rwkv_r_k, rwkv_g_up, rwkv_lnx_g, rwkv_lnx_b, att_sink, conv_dw, conv_dw_b, conv_ln_g, conv_ln_b, w_branch, w_out, w_mlp1, w_mlp2, final_g):
    batch, seq, _ = x.shape
    geom = _Geom(batch, ctx.shape[1], seq)
    depth = w_in.shape[0]
    assert batch + 1 <= 8
    cond = jnp.zeros((8, D_MODEL), F32).at[:batch].set(c).at[batch].set(c_ctx)
    xall = jnp.concatenate([ctx, x], axis=1).reshape(geom.m, D_MODEL)
    modtabs = []
    for l in range(depth):
        mod = _ada_mod(cond, ada_w, l, ada_b[l])
        mod_x = mod[:batch].reshape(batch, 1, 6, D_MODEL)
        mod_c = jnp.broadcast_to(mod[batch].reshape(1, 1, 6, D_MODEL), (batch, 1, 6, D_MODEL))
        modtabs.append(jnp.concatenate([mod_c, mod_x], axis=1).reshape(2 * batch, 6, D_MODEL))
    h = _norm_mod(geom, xall, norm1_g[0], modtabs[0], rows=(0, 1))
    for l in range(depth):
        modtab = modtabs[l]
        lp = {
            "rwkv_mu": rwkv_mu[l], "rwkv_w0": rwkv_w0[l], "rwkv_w_up": rwkv_w_up[l],
            "rwkv_a0": rwkv_a0[l], "rwkv_a_up": rwkv_a_up[l], "rwkv_k_k": rwkv_k_k[l],
            "rwkv_k_a": rwkv_k_a[l], "rwkv_r_k": rwkv_r_k[l], "rwkv_g_up": rwkv_g_up[l],
            "rwkv_lnx_g": rwkv_lnx_g[l], "rwkv_lnx_b": rwkv_lnx_b[l], "att_sink": att_sink[l],
            "conv_dw": conv_dw[l], "conv_dw_b": conv_dw_b[l], "conv_ln_g": conv_ln_g[l],
            "conv_ln_b": conv_ln_b[l], "w_branch": w_branch[l], "w_out": w_out[l],
        }
        last = l + 1 == depth
        xall, h2 = _mixer(geom, h, xall, modtab, lp, w_in, l, norm2_g[l], latent_only=last)
        kind = "packed" if last else "all"
        hid = _matmul_w32(_RowPlan(geom, 1024, kind), h2, w_mlp1, l, BF16, epilogue="relu2", name="mlp_up")
        down = functools.partial(_matmul, a=hid, b=w_mlp2[l].astype(BF16), out_dtype=F32, epilogue="resid",
                                 res=xall, modtab=modtab, gate_row=5, name="mlp_down")
        if last:
            xall = down(_RowPlan(geom, 1024, kind))
        else:
            xall, h = down(_RowPlan(geom, 512), norm=(norm1_g[l + 1], modtabs[l + 1], (0, 1)))
    return _final_norm(geom, xall, final_g).reshape(batch, seq, D_MODEL)
```
